```python
import math
import jax, jax.numpy as jnp
from jax import lax
import numpy as np

D_MODEL = 1024
BATCH = 8
SEQ = 2048
DEPTH = 4
DEC_BATCH = 32
DEC_SEQ = 8
PAST_LEN = 8192
PAGE_SIZE = 128

N_MIXERS = 4
N_CONV_LAYERS = (DEPTH + 3) // N_MIXERS
N_DIFF_LAYERS = (DEPTH + 2) // N_MIXERS
N_DSA_LAYERS = (DEPTH + 1) // N_MIXERS
N_HGRN_LAYERS = DEPTH // N_MIXERS

D_FF = 4 * D_MODEL
NORM_EPS = 1e-6
LN_EPS = 1e-5
ROPE_THETA = 10000.0
ROPE_DIM = 64

CONV_W = 31

DA_HEADS = D_MODEL // 128
DA_HD = 64
DA_Q_BLOCK = 128

DSA_HEADS = D_MODEL // 64
DSA_HD = 64
IDX_HEADS = 8
IDX_HD = 64
DSA_TOPK = 256
DSA_Q_BLOCK = 32
DSA_IN = 3 * DSA_HEADS * DSA_HD + IDX_HEADS * IDX_HD + IDX_HD + IDX_HEADS

HG_EXPAND = 128
HG_HEADS = D_MODEL // HG_EXPAND
HG_HEAD_V = D_MODEL // HG_HEADS
HG_CHUNK = 64

kernel_name = 'hybrid_conv_diffattn_dsa_hgrn2_step'


def rms_norm(x, g, eps=NORM_EPS):
    xf = x.astype(jnp.float32)
    y = xf * lax.rsqrt(jnp.mean(xf * xf, axis=-1, keepdims=True) + eps)
    return (y * g.astype(jnp.float32)).astype(x.dtype)


def layer_norm(x, g, b, eps=LN_EPS):
    xf = x.astype(jnp.float32)
    mu = jnp.mean(xf, axis=-1, keepdims=True)
    var = jnp.mean(jnp.square(xf - mu), axis=-1, keepdims=True)
    y = (xf - mu) * lax.rsqrt(var + eps)
    return (y * g.astype(jnp.float32) + b.astype(jnp.float32)).astype(x.dtype)


def rope_tables(pos):
    inv = ROPE_THETA ** (-jnp.arange(0, ROPE_DIM, 2, dtype=jnp.float32) / ROPE_DIM)
    ang = pos.astype(jnp.float32)[:, None] * inv[None, :]
    return jnp.cos(ang), jnp.sin(ang)


def apply_rope(x, cos, sin):
    shape = (1, cos.shape[0]) + (1,) * (x.ndim - 3) + (cos.shape[1],)
    c = cos.reshape(shape)
    s = sin.reshape(shape)
    x1, x2 = jnp.split(x.astype(jnp.float32), 2, axis=-1)
    return jnp.concatenate([x1 * c - x2 * s, x2 * c + x1 * s], axis=-1).astype(x.dtype)


def gather_rows(rows, idx):
    return jax.vmap(lambda r, i: r[i])(rows, idx)


def sqrelu_mlp(h, w1, w2):
    return jnp.square(jax.nn.relu(h @ w1)) @ w2


def conv_module(h, buf, w1, b1, dw, dwb, ln_g, ln_b, w2, b2):
    a, gate = jnp.split(h @ w1 + b1, 2, axis=-1)
    u = a * jax.nn.sigmoid(gate)
    ext = jnp.concatenate([buf.astype(u.dtype), u], axis=1)
    y = lax.conv_general_dilated(ext, dw[:, None, :].astype(u.dtype), window_strides=(1,), padding='VALID',
                                 dimension_numbers=('NWC', 'WIO', 'NWC'), feature_group_count=D_MODEL) + dwb
    y = jax.nn.silu(layer_norm(y, ln_g, ln_b))
    return y @ w2 + b2, ext[:, -(CONV_W - 1):]


def diff_qkv(h, w_in, cos, sin):
    B, T, _ = h.shape
    q, k, v = jnp.split(h @ w_in, 3, axis=-1)
    q = apply_rope(q.reshape(B, T, DA_HEADS, 2, DA_HD), cos, sin)
    k = apply_rope(k.reshape(B, T, DA_HEADS, 2, DA_HD), cos, sin)
    return q, k, v.reshape(B, T, DA_HEADS, 2 * DA_HD)


def diff_combine(s, lam):
    p = jax.nn.softmax(s, axis=-1)
    return p[:, :, 0] - lam * p[:, :, 1]


def diff_attn_prompt(q, k, v, lam):
    B, T = q.shape[:2]
    nb = T // DA_Q_BLOCK
    kpos = jnp.arange(T)
    vf = v.astype(jnp.float32)
    qb = q.reshape(B, nb, DA_Q_BLOCK, DA_HEADS, 2, DA_HD).swapaxes(0, 1)

    def block(args):
        qi, i = args
        qpos = i * DA_Q_BLOCK + jnp.arange(DA_Q_BLOCK)
        s = jnp.einsum('bqhmd,bkhmd->bhmqk', qi, k).astype(jnp.float32) * DA_HD ** -0.5
        s = jnp.where(kpos[None, :] <= qpos[:, None], s, -jnp.inf)
        return jnp.einsum('bhqk,bkhe->bqhe', diff_combine(s, lam), vf)

    o = lax.map(block, (qb, jnp.arange(nb)))
    return o.swapaxes(0, 1).reshape(B, T, DA_HEADS, 2 * DA_HD).astype(q.dtype)


def diff_attn_sample(q, k_new, v_new, k_pool, v_pool, slot, page_table, lam):
    Bd, Tn = q.shape[:2]
    n_pages = page_table.shape[1]
    past_len = n_pages * PAGE_SIZE

    def page_scores(p):
        kp = k_pool[slot, page_table[:, p]].reshape(Bd, PAGE_SIZE, DA_HEADS, 2, DA_HD)
        return jnp.einsum('bqhmd,bkhmd->bhmqk', q, kp.astype(q.dtype)).astype(jnp.float32)

    s_past = lax.map(page_scores, jnp.arange(n_pages))
    s_past = jnp.moveaxis(s_past, 0, 4).reshape(Bd, DA_HEADS, 2, Tn, past_len)
    s_new = jnp.einsum('bqhmd,bkhmd->bhmqk', q, k_new).astype(jnp.float32)
    s_new = jnp.where(jnp.tril(jnp.ones((Tn, Tn), dtype=bool)), s_new, -jnp.inf)
    a = diff_combine(jnp.concatenate([s_past, s_new], axis=-1) * DA_HD ** -0.5, lam)
    a_past = jnp.moveaxis(a[..., :past_len].reshape(Bd, DA_HEADS, Tn, n_pages, PAGE_SIZE), 3, 0)

    def acc_page(acc, xs):
        p, ap = xs
        vp = v_pool[slot, page_table[:, p]].astype(jnp.float32)
        return acc + jnp.einsum('bhqk,bkhe->bqhe', ap, vp), None

    acc0 = jnp.einsum('bhqk,bkhe->bqhe', a[..., past_len:], v_new.astype(jnp.float32))
    acc, _ = lax.scan(acc_page, acc0, (jnp.arange(n_pages), a_past))
    return acc.astype(q.dtype)


def diff_out(o, g, wo, lam_init):
    B, T = o.shape[:2]
    o = rms_norm(o, g) * (1.0 - lam_init)
    return o.reshape(B, T, D_MODEL) @ wo


def dsa_proj(h, w_in, cos, sin):
    B, T, _ = h.shape
    dq = DSA_HEADS * DSA_HD
    di = IDX_HEADS * IDX_HD
    cuts = [dq, 2 * dq, 3 * dq, 3 * dq + di, 3 * dq + di + IDX_HD]
    q, k, v, qi, ki, wi = jnp.split(h @ w_in, cuts, axis=-1)
    q = apply_rope(q.reshape(B, T, DSA_HEADS, DSA_HD), cos, sin)
    k = apply_rope(k.reshape(B, T, DSA_HEADS, DSA_HD), cos, sin)
    qi = apply_rope(qi.reshape(B, T, IDX_HEADS, IDX_HD), cos, sin)
    ki = apply_rope(ki, cos, sin)
    wt = wi * IDX_HEADS ** -0.5
    return q, k, v.reshape(B, T, DSA_HEADS, DSA_HD), qi, ki, wt


def indexer_select(qi, wt, ki, qpos, k_sel):
    dots = jnp.einsum('bqhd,bkd->bqhk', qi, ki).astype(jnp.float32) * IDX_HD ** -0.5
    score = jnp.einsum('bqh,bqhk->bqk', wt.astype(jnp.float32), jax.nn.relu(dots))
    kpos = jnp.arange(ki.shape[1])
    score = jnp.where(kpos[None, None, :] <= qpos[None, :, None], score, -jnp.inf)
    _, idx = lax.top_k(score, k_sel)
    return idx, idx <= qpos[None, :, None]


def sparse_attend(q, kg, vg, valid):
    s = jnp.einsum('bqhd,bqkhd->bhqk', q, kg).astype(jnp.float32) * DSA_HD ** -0.5
    s = jnp.where(valid[:, None], s, -jnp.inf)
    p = jax.nn.softmax(s, axis=-1)
    return jnp.einsum('bhqk,bqkhd->bqhd', p.astype(vg.dtype), vg)


def dsa_prompt(q, k, v, qi, ki, wt, k_sel):
    B, T = q.shape[:2]
    nb = T // DSA_Q_BLOCK

    def split(a):
        return a.reshape((B, nb, DSA_Q_BLOCK) + a.shape[2:]).swapaxes(0, 1)

    def block(args):
        qb, qib, wb, i = args
        qpos = i * DSA_Q_BLOCK + jnp.arange(DSA_Q_BLOCK)
        idx, valid = indexer_select(qib, wb, ki, qpos, k_sel)
        return sparse_attend(qb, gather_rows(k, idx), gather_rows(v, idx), valid)

    o = lax.map(block, (split(q), split(qi), split(wt), jnp.arange(nb)))
    return o.swapaxes(0, 1).reshape(B, T, DSA_HEADS, DSA_HD)


def dsa_sample(q, k_new, v_new, qi, ki_new, wt, k_pool, v_pool, idx_pool, slot, page_table, k_sel):
    Bd, Tn = q.shape[:2]
    past_len = page_table.shape[1] * PAGE_SIZE
    ki_past = idx_pool[slot, page_table].reshape(Bd, past_len, IDX_HD)
    ki_all = jnp.concatenate([ki_past, ki_new.astype(ki_past.dtype)], axis=1)
    qpos = past_len + jnp.arange(Tn)
    idx, valid = indexer_select(qi, wt, ki_all, qpos, k_sel)
    in_past = idx < past_len
    pidx = jnp.where(in_past, idx, 0)
    phys = jax.vmap(lambda pt, i: pt[i])(page_table, pidx // PAGE_SIZE)
    off = pidx % PAGE_SIZE
    nidx = jnp.where(in_past, 0, idx - past_len)

    def pick(pool, new):
        return jnp.where(in_past[..., None, None], pool[slot, phys, off], gather_rows(new, nidx))

    return sparse_attend(q, pick(k_pool, k_new), pick(v_pool, v_new), valid)


def gla_recurrence(q, k, v, log_f, S0, chunk):
    B, T, H, E = q.shape
    Dv = v.shape[-1]
    n = T // chunk

    def to_chunks(a):
        return a.astype(jnp.float32).reshape(B, n, chunk, H, a.shape[-1]).transpose(1, 0, 3, 2, 4)

    tri = jnp.tril(jnp.ones((chunk, chunk), dtype=bool))

    def step(S, xs):
        qc, kc, vc, lc = xs
        b = jnp.cumsum(lc, axis=2)
        rel = jnp.where(tri[None, None, :, :, None], b[:, :, :, None, :] - b[:, :, None, :, :], -jnp.inf)
        A = jnp.einsum('bhte,bhse,bhtse->bhts', qc, kc, jnp.exp(rel))
        o = jnp.einsum('bhts,bhsv->bhtv', A, vc) + jnp.einsum('bhte,bhev->bhtv', qc * jnp.exp(b), S)
        b_last = b[:, :, -1:, :]
        S_new = jnp.exp(b_last[:, :, 0, :])[..., None] * S + jnp.einsum('bhse,bhsv->bhev', kc * jnp.exp(b_last - b), vc)
        return S_new, o

    S, o = lax.scan(step, S0.astype(jnp.float32), (to_chunks(q), to_chunks(k), to_chunks(v), to_chunks(log_f)))
    return o.transpose(1, 0, 3, 2, 4).reshape(B, T, H, Dv), S


def hgrn2_mixer(h, S0, w_in, lb, g_norm, wo):
    B, T, _ = h.shape
    q, fz, i, g = jnp.split(h @ w_in, 4, axis=-1)
    f = lb + (1.0 - lb) * jax.nn.sigmoid(fz.astype(jnp.float32))

    def heads(a):
        return a.reshape(B, T, HG_HEADS, -1)

    o, S = gla_recurrence(heads(jax.nn.silu(q)), heads(1.0 - f), heads(i), heads(jnp.log(f)), S0,
                          math.gcd(T, HG_CHUNK))
    o = rms_norm(o.astype(h.dtype), g_norm) * jax.nn.silu(heads(g))
    return o.reshape(B, T, D_MODEL) @ wo, S


def setup_inputs(seed: int = 0) -> dict:
    key = jax.random.key(seed)
    ks = iter(jax.random.split(key, 48))

    def nrm(shape, scale=1.0):
        return jax.random.normal(next(ks), shape, jnp.float32) * scale

    def gain(shape):
        return 1.0 + nrm(shape, 0.02)

    d = D_MODEL
    n_pages = PAST_LEN // PAGE_SIZE
    n_used = DEC_BATCH * n_pages
    n_pool = n_used + (n_used + 3) // 4
    return {
        'x_prompt': nrm((BATCH, SEQ, d)),
        'x_sample': nrm((DEC_BATCH, DEC_SEQ, d)),
        'state_conv': nrm((N_CONV_LAYERS, DEC_BATCH, CONV_W - 1, d), 0.5),
        'cache_da_k': nrm((N_DIFF_LAYERS, n_pool, PAGE_SIZE, DA_HEADS, 2 * DA_HD)),
        'cache_da_v': nrm((N_DIFF_LAYERS, n_pool, PAGE_SIZE, DA_HEADS, 2 * DA_HD)),
        'cache_dsa_k': nrm((N_DSA_LAYERS, n_pool, PAGE_SIZE, DSA_HEADS, DSA_HD)),
        'cache_dsa_v': nrm((N_DSA_LAYERS, n_pool, PAGE_SIZE, DSA_HEADS, DSA_HD)),
        'cache_dsa_idx_k': nrm((N_DSA_LAYERS, n_pool, PAGE_SIZE, IDX_HD)),
        'state_hgrn': nrm((N_HGRN_LAYERS, DEC_BATCH, HG_HEADS, HG_EXPAND, HG_HEAD_V), 0.5),
        'page_table': jax.random.permutation(next(ks), n_pool)[:n_used].reshape(DEC_BATCH, n_pages).astype(jnp.int32),
        'norm1_g': gain((DEPTH, d)),
        'norm2_g': gain((DEPTH, d)),
        'final_g': gain((d,)),
        'mlp_w1': nrm((DEPTH, d, D_FF), d ** -0.5),
        'mlp_w2': nrm((DEPTH, D_FF, d), D_FF ** -0.5),
        'cv_w1': nrm((N_CONV_LAYERS, d, 2 * d), d ** -0.5),
        'cv_b1': nrm((N_CONV_LAYERS, 2 * d), 0.01),
        'cv_dw': nrm((N_CONV_LAYERS, CONV_W, d), CONV_W ** -0.5),
        'cv_dwb': nrm((N_CONV_LAYERS, d), 0.01),
        'cv_ln_g': gain((N_CONV_LAYERS, d)),
        'cv_ln_b': nrm((N_CONV_LAYERS, d), 0.01),
        'cv_w2': nrm((N_CONV_LAYERS, d, d), d ** -0.5),
        'cv_b2': nrm((N_CONV_LAYERS, d), 0.01),
        'da_w_in': nrm((N_DIFF_LAYERS, d, 3 * d), d ** -0.5),
        'da_lq1': nrm((N_DIFF_LAYERS, DA_HD), 0.1),
        'da_lk1': nrm((N_DIFF_LAYERS, DA_HD), 0.1),
        'da_lq2': nrm((N_DIFF_LAYERS, DA_HD), 0.1),
        'da_lk2': nrm((N_DIFF_LAYERS, DA_HD), 0.1),
        'da_subln_g': gain((N_DIFF_LAYERS, 2 * DA_HD)),
        'da_wo': nrm((N_DIFF_LAYERS, d, d), d ** -0.5),
        'dsa_w_in': nrm((N_DSA_LAYERS, d, DSA_IN), d ** -0.5),
        'dsa_wo': nrm((N_DSA_LAYERS, d, d), d ** -0.5),
        'hg_w_in': nrm((N_HGRN_LAYERS, d, 4 * d), d ** -0.5),
        'hg_lb': nrm((DEPTH, HG_HEADS * HG_EXPAND), 0.1),
        'hg_norm_g': gain((N_HGRN_LAYERS, HG_HEAD_V)),
        'hg_wo': nrm((N_HGRN_LAYERS, d, d), d ** -0.5),
    }


def reference(x_prompt, x_sample, state_conv, cache_da_k, cache_da_v, cache_dsa_k, cache_dsa_v, cache_dsa_idx_k,
              state_hgrn, page_table, norm1_g, norm2_g, final_g, mlp_w1, mlp_w2,
              cv_w1, cv_b1, cv_dw, cv_dwb, cv_ln_g, cv_ln_b, cv_w2, cv_b2,
              da_w_in, da_lq1, da_lk1, da_lq2, da_lk2, da_subln_g, da_wo,
              dsa_w_in, dsa_wo, hg_w_in, hg_lb, hg_norm_g, hg_wo):
    Bp, T, _ = x_prompt.shape
    Bd, Tn, _ = x_sample.shape
    past_len = page_table.shape[1] * PAGE_SIZE
    cos_p, sin_p = rope_tables(jnp.arange(T))
    cos_s, sin_s = rope_tables(past_len + jnp.arange(Tn))
    k_sel_p = min(DSA_TOPK, T // 4)
    k_sel_s = min(DSA_TOPK, (past_len + Tn) // 4)
    lb_w = jax.nn.softmax(hg_lb.astype(jnp.float32), axis=0)
    lb_all = jnp.cumsum(lb_w, axis=0) - lb_w[0]

    new = {n: [] for n in ('conv_p', 'conv_s', 'dak_p', 'dav_p', 'dak_s', 'dav_s', 'dsak_p', 'dsav_p', 'dsai_p',
                           'dsak_s', 'dsav_s', 'dsai_s', 'hg_p', 'hg_s')}
    xp, xs = x_prompt, x_sample
    for layer in range(DEPTH):
        kind, j = layer % N_MIXERS, layer // N_MIXERS
        hp = rms_norm(xp, norm1_g[layer])
        hs = rms_norm(xs, norm1_g[layer])
        if kind == 0:
            cw = (cv_w1[j], cv_b1[j], cv_dw[j], cv_dwb[j], cv_ln_g[j], cv_ln_b[j], cv_w2[j], cv_b2[j])
            mp, bp = conv_module(hp, jnp.zeros((Bp, CONV_W - 1, D_MODEL), hp.dtype), *cw)
            ms, bs = conv_module(hs, state_conv[j], *cw)
            new['conv_p'].append(bp)
            new['conv_s'].append(bs)
        elif kind == 1:
            lam_init = 0.8 - 0.6 * math.exp(-0.3 * layer)
            lam = (jnp.exp(jnp.sum(da_lq1[j] * da_lk1[j]).astype(jnp.float32))
                   - jnp.exp(jnp.sum(da_lq2[j] * da_lk2[j]).astype(jnp.float32)) + lam_init)
            qp, kp, vp = diff_qkv(hp, da_w_in[j], cos_p, sin_p)
            qs, ks, vs = diff_qkv(hs, da_w_in[j], cos_s, sin_s)
            op = diff_attn_prompt(qp, kp, vp, lam)
            os_ = diff_attn_sample(qs, ks, vs, cache_da_k, cache_da_v, j, page_table, lam)
            mp = diff_out(op, da_subln_g[j], da_wo[j], lam_init)
            ms = diff_out(os_, da_subln_g[j], da_wo[j], lam_init)
            new['dak_p'].append(kp.reshape(Bp, T, DA_HEADS, 2 * DA_HD))
            new['dav_p'].append(vp)
            new['dak_s'].append(ks.reshape(Bd, Tn, DA_HEADS, 2 * DA_HD))
            new['dav_s'].append(vs)
        elif kind == 2:
            qp, kp, vp, qip, kip, wp = dsa_proj(hp, dsa_w_in[j], cos_p, sin_p)
            qs, ks, vs, qis, kis, ws = dsa_proj(hs, dsa_w_in[j], cos_s, sin_s)
            op = dsa_prompt(qp, kp, vp, qip, kip, wp, k_sel_p)
            os_ = dsa_sample(qs, ks, vs, qis, kis, ws, cache_dsa_k, cache_dsa_v, cache_dsa_idx_k, j, page_table, k_sel_s)
            mp = op.reshape(Bp, T, D_MODEL) @ dsa_wo[j]
            ms = os_.reshape(Bd, Tn, D_MODEL) @ dsa_wo[j]
            new['dsak_p'].append(kp)
            new['dsav_p'].append(vp)
            new['dsai_p'].append(kip)
            new['dsak_s'].append(ks)
            new['dsav_s'].append(vs)
            new['dsai_s'].append(kis)
        else:
            hw = (hg_w_in[j], lb_all[layer], hg_norm_g[j], hg_wo[j])
            mp, sp = hgrn2_mixer(hp, jnp.zeros((Bp, HG_HEADS, HG_EXPAND, HG_HEAD_V), jnp.float32), *hw)
            ms, ss = hgrn2_mixer(hs, state_hgrn[j], *hw)
            new['hg_p'].append(sp.astype(state_hgrn.dtype))
            new['hg_s'].append(ss.astype(state_hgrn.dtype))
        xp = xp + mp
        xs = xs + ms
        xp = xp + sqrelu_mlp(rms_norm(xp, norm2_g[layer]), mlp_w1[layer], mlp_w2[layer])
        xs = xs + sqrelu_mlp(rms_norm(xs, norm2_g[layer]), mlp_w1[layer], mlp_w2[layer])

    y_prompt = rms_norm(xp, final_g)
    y_sample = rms_norm(xs, final_g)
    return (y_prompt, y_sample,
            jnp.stack(new['conv_p']), jnp.stack(new['conv_s']),
            jnp.stack(new['dak_p']), jnp.stack(new['dav_p']), jnp.stack(new['dak_s']), jnp.stack(new['dav_s']),
            jnp.stack(new['dsak_p']), jnp.stack(new['dsav_p']), jnp.stack(new['dsai_p']),
            jnp.stack(new['dsak_s']), jnp.stack(new['dsav_s']), jnp.stack(new['dsai_s']),
            jnp.stack(new['hg_p']), jnp.stack(new['hg_s']))
```

```python
import functools
import math

import jax
import jax.numpy as jnp
from jax import lax
from jax.experimental import pallas as pl
from jax.experimental.pallas import tpu as pltpu

F32 = jnp.float32
BF16 = jnp.bfloat16
I32 = jnp.int32

NORM_EPS = 1e-6
LN_EPS = 1e-5
ROPE_THETA = 10000.0
ROPE_DIM = 64
CONV_W = 31
PAGE = 128
HEAD64 = 64
IDX_HEADS = 8
DSA_TOPK = 256

LANES = 128
SUBLANES = 8
CONV_HALO = 32
CONV_ROWS = 32
ROW_TILE = 256
ATT_TQ = 256
DSA_TQ = 128
HG_C = 128
HG_SUB = 16
HG_TB = 512
MLP_FCHUNK = 1024
VMEM_LIMIT = 56 * 1024 * 1024
NEG_BIG = -1e30
INT_MIN = -2 ** 31


def _cparams(*sem):
    return pltpu.CompilerParams(dimension_semantics=sem, vmem_limit_bytes=VMEM_LIMIT)


def _const_spec(shape):
    nd = len(shape)
    return pl.BlockSpec(shape, lambda *_: (0,) * nd, pipeline_mode=pl.Buffered(1))


def _dot(a, b):
    return jnp.dot(a, b, preferred_element_type=F32)


def _dot_nt(a, b):
    return lax.dot_general(a, b, (((1,), (1,)), ((), ())), preferred_element_type=F32)


def _rms(x, g):
    return x * lax.rsqrt(jnp.mean(x * x, axis=-1, keepdims=True) + NORM_EPS) * g


def _sigmoid(x):
    return 1.0 / (1.0 + jnp.exp(-x))


def _lane_iota(shape=(1, LANES)):
    return lax.broadcasted_iota(I32, shape, len(shape) - 1)


def _rope(xc, cos, sin_signed):
    first_half = (_lane_iota() & 32) == 0
    partner = jnp.where(first_half, pltpu.roll(xc, 96, 1), pltpu.roll(xc, 32, 1))
    return xc * cos + partner * sin_signed


def _rope_tables(pos):
    inv = ROPE_THETA ** (-jnp.arange(0, ROPE_DIM, 2, dtype=F32) / ROPE_DIM)
    ang = pos.astype(F32)[:, None] * inv[None, :]
    cos, sin = jnp.cos(ang), jnp.sin(ang)
    return jnp.tile(cos, (1, 4)), jnp.tile(jnp.concatenate([-sin, sin], axis=1), (1, 2))


def _post_kernel(x_ref, o_ref, wo_ref, bo_ref, g_ref, w1_ref, w2_ref, fg_ref, y_ref, h_ref, *, final):
    x1 = x_ref[...] + _dot(o_ref[...], wo_ref[...]) + bo_ref[...]
    h = _rms(x1, g_ref[...]).astype(BF16)
    f = w1_ref.shape[1]
    fc = min(f, MLP_FCHUNK)
    for c in range(f // fc):
        a = jnp.maximum(_dot(h, w1_ref[:, c * fc:(c + 1) * fc]), 0.0)
        h_ref[:, c * fc:(c + 1) * fc] = (a * a).astype(BF16)
    y = x1 + _dot(h_ref[...], w2_ref[...])
    if final:
        y = _rms(y, fg_ref[...])
    y_ref[...] = y


def _post(x, o, wo, bo, g2, w1, w2, fg, final):
    r, d = x.shape
    f = w1.shape[1]
    tm = min(r, ROW_TILE)
    row = lambda w: pl.BlockSpec((tm, w), lambda i: (i, 0))
    return pl.pallas_call(
        functools.partial(_post_kernel, final=final),
        grid=(r // tm,),
        in_specs=[row(d), row(d), _const_spec((d, d)), _const_spec((1, d)), _const_spec((1, d)),
                  _const_spec((d, f)), _const_spec((f, d)), _const_spec((1, d))],
        out_specs=row(d),
        out_shape=jax.ShapeDtypeStruct((r, d), F32),
        scratch_shapes=[pltpu.VMEM((tm, f), BF16)],
        compiler_params=_cparams("parallel"),
        name="post_mlp",
    )(x, o, wo, bo, g2, w1, w2, fg)


def _conv_kernel(x_ref, buf_ref, g_ref, w1_ref, b1_ref, dw_ref, dwb_ref, lng_ref, lnb_ref,
                 o_ref, tail_ref, ext_ref, y_ref, *, nb, tt, d, carry):
    @pl.when(pl.program_id(1) == 0)
    def _():
        ext_ref[:, 0:CONV_HALO, :] = buf_ref[...]

    x = x_ref[...].reshape(nb * tt, d)
    h = _rms(x, g_ref[...]).astype(BF16)
    ag = _dot(h, w1_ref[...]) + b1_ref[...]
    u = ag[:, :d] * _sigmoid(ag[:, d:])
    ext_ref[:, CONV_HALO:CONV_HALO + tt, :] = u.reshape(nb, tt, d)

    rs = min(tt, CONV_ROWS)
    for r in range(tt // rs):
        acc = jnp.zeros((nb, rs, d), F32)
        for k in range(CONV_W):
            off = r * rs + k + CONV_HALO - (CONV_W - 1)
            acc = acc + dw_ref[k] * ext_ref[:, off:off + rs, :]
        y_ref[:, r * rs:(r + 1) * rs, :] = acc + dwb_ref[...]

    y = y_ref[...]
    mu = jnp.mean(y, axis=-1, keepdims=True)
    yc = y - mu
    var = jnp.mean(yc * yc, axis=-1, keepdims=True)
    z = yc * lax.rsqrt(var + LN_EPS) * lng_ref[...] + lnb_ref[...]
    o_ref[...] = (z * _sigmoid(z)).astype(BF16)
    tail = ext_ref[:, tt:tt + CONV_HALO, :]
    tail_ref[...] = tail
    if carry:
        ext_ref[:, 0:CONV_HALO, :] = tail


def _conv_mixer(x3, buf, g1, w1, b1, dw, dwb, lng, lnb, nb, tt):
    b, t, d = x3.shape
    nt = t // tt
    kern = functools.partial(_conv_kernel, nb=nb, tt=tt, d=d, carry=nt > 1)
    return pl.pallas_call(
        kern,
        grid=(b // nb, nt),
        in_specs=[pl.BlockSpec((nb, tt, d), lambda i, j: (i, j, 0)),
                  pl.BlockSpec((nb, CONV_HALO, d), lambda i, j: (i, 0, 0)),
                  _const_spec((1, d)), _const_spec((d, 2 * d)), _const_spec((1, 2 * d)),
                  _const_spec((CONV_HALO, 1, d)), _const_spec((1, d)), _const_spec((1, d)),
                  _const_spec((1, d))],
        out_specs=[pl.BlockSpec((nb, tt, d), lambda i, j: (i, j, 0)),
                   pl.BlockSpec((nb, CONV_HALO, d), lambda i, j: (i, 0, 0))],
        out_shape=[jax.ShapeDtypeStruct((b, t, d), BF16),
                   jax.ShapeDtypeStruct((b, CONV_HALO, d), F32)],
        scratch_shapes=[pltpu.VMEM((nb, CONV_HALO + tt, d), F32), pltpu.VMEM((nb, tt, d), F32)],
        compiler_params=_cparams("parallel", "arbitrary"),
        name="conv_mixer",
    )(x3, buf, g1, w1, b1, dw, dwb, lng, lnb)


def _da_in_kernel(x_ref, g_ref, w_ref, cos_ref, sin_ref, q_ref, k_ref, v_ref, kb_ref, vb_ref, *, d):
    h = _rms(x_ref[...], g_ref[...]).astype(BF16)
    y = _dot(h, w_ref[...])
    cos, sin = cos_ref[...], sin_ref[...]
    for c in range(d // LANES):
        sl = slice(c * LANES, (c + 1) * LANES)
        q = _rope(y[:, c * LANES:(c + 1) * LANES], cos, sin)
        q_ref[:, sl] = (q * HEAD64 ** -0.5).astype(BF16)
        k = _rope(y[:, d + c * LANES:d + (c + 1) * LANES], cos, sin)
        k_ref[:, sl] = k
        kb_ref[:, sl] = k.astype(BF16)
        v = y[:, 2 * d + c * LANES:2 * d + (c + 1) * LANES]
        v_ref[:, sl] = v
        vb_ref[:, sl] = v.astype(BF16)


def _da_in(x, g1, w, cos, sin):
    r, d = x.shape
    tm = min(r, ROW_TILE)
    ntab = cos.shape[0] // tm
    row = lambda: pl.BlockSpec((tm, d), lambda i: (i, 0))
    tab = lambda: pl.BlockSpec((tm, LANES), lambda i: (i % ntab, 0))
    return pl.pallas_call(
        functools.partial(_da_in_kernel, d=d),
        grid=(r // tm,),
        in_specs=[row(), _const_spec((1, d)), _const_spec((d, 3 * d)), tab(), tab()],
        out_specs=[row(), row(), row(), row(), row()],
        out_shape=[jax.ShapeDtypeStruct((r, d), BF16), jax.ShapeDtypeStruct((r, d), F32),
                   jax.ShapeDtypeStruct((r, d), F32), jax.ShapeDtypeStruct((r, d), BF16),
                   jax.ShapeDtypeStruct((r, d), BF16)],
        compiler_params=_cparams("parallel"),
        name="da_in",
    )(x, g1, w, cos, sin)


def _da_lambda(lq1, lk1, lq2, lk2, lam_init):
    return (jnp.exp(jnp.sum(lq1[...] * lk1[...], axis=-1, keepdims=True))
            - jnp.exp(jnp.sum(lq2[...] * lk2[...], axis=-1, keepdims=True)) + lam_init)


def _da_prompt_kernel(lq1, lk1, lq2, lk2, subg_ref, q_ref, k_ref, v_ref, o_ref, *, tq, lam_init):
    i = pl.program_id(2)
    lam = _da_lambda(lq1, lk1, lq2, lk2, lam_init)
    q = q_ref[...]
    lane = _lane_iota()
    zero = jnp.zeros_like(q)
    q1 = jnp.where(lane < HEAD64, q, zero)
    q2 = jnp.where(lane >= HEAD64, q, zero)
    keep = (lax.broadcasted_iota(I32, (tq, tq), 0) >= lax.broadcasted_iota(I32, (tq, tq), 1))

    def update(s, v, m, l, acc):
        m_new = jnp.maximum(m, jnp.max(s, axis=-1, keepdims=True))
        alpha = jnp.exp(m - m_new)
        p = jnp.exp(s - m_new)
        l = alpha * l + jnp.sum(p, axis=-1, keepdims=True)
        acc = alpha * acc + _dot(p.astype(BF16), v)
        return m_new, l, acc

    def block(j, carry, diag):
        start = pl.multiple_of(j * tq, tq)
        kblk = k_ref[pl.ds(start, tq), :]
        vblk = v_ref[pl.ds(start, tq), :]
        s1 = _dot_nt(q1, kblk)
        s2 = _dot_nt(q2, kblk)
        if diag:
            s1 = jnp.where(keep, s1, NEG_BIG)
            s2 = jnp.where(keep, s2, NEG_BIG)
        m1, l1, a1, m2, l2, a2 = carry
        m1, l1, a1 = update(s1, vblk, m1, l1, a1)
        m2, l2, a2 = update(s2, vblk, m2, l2, a2)
        return m1, l1, a1, m2, l2, a2

    m0 = jnp.full((tq, 1), NEG_BIG, F32)
    l0 = jnp.zeros((tq, 1), F32)
    a0 = jnp.zeros((tq, LANES), F32)
    carry = lax.fori_loop(0, i, lambda j, c: block(j, c, False), (m0, l0, a0, m0, l0, a0))
    m1, l1, a1, m2, l2, a2 = block(i, carry, True)
    o = a1 / l1 - lam * (a2 / l2)
    o_ref[...] = (_rms(o, subg_ref[...]) * (1.0 - lam_init)).astype(BF16)


def _da_prompt(q, kb, vb, lams, subg, bsz, t, lam_init):
    r, d = q.shape
    tq = min(t, ATT_TQ)
    nq = t // tq
    lam_spec = _const_spec((1, HEAD64))
    return pl.pallas_call(
        functools.partial(_da_prompt_kernel, tq=tq, lam_init=lam_init),
        grid=(bsz, d // LANES, nq),
        in_specs=[lam_spec, lam_spec, lam_spec, lam_spec, _const_spec((1, LANES)),
                  pl.BlockSpec((tq, LANES), lambda b, h, i: (b * nq + i, h)),
                  pl.BlockSpec((t, LANES), lambda b, h, i: (b, h)),
                  pl.BlockSpec((t, LANES), lambda b, h, i: (b, h))],
        out_specs=pl.BlockSpec((tq, LANES), lambda b, h, i: (b * nq + i, h)),
        out_shape=jax.ShapeDtypeStruct((r, d), BF16),
        compiler_params=_cparams("parallel", "parallel", "arbitrary"),
        name="da_prompt",
    )(*lams, subg, q, kb, vb)


def _block_diag_queries(q, groups, tn):
    d = q.shape[1]
    qt = jnp.concatenate([q] * groups, axis=0)
    shift = int(math.log2(tn))
    rg = lax.broadcasted_iota(I32, (groups * tn, d), 0) >> shift
    cg = lax.broadcasted_iota(I32, (groups * tn, d), 1) >> 6
    return jnp.where(rg == cg, qt, 0.0).astype(BF16)


def _online_update(s, v, valid, m_sc, l_sc, acc_sc):
    m_old = m_sc[...]
    m_new = jnp.maximum(m_old, jnp.max(s, axis=-1, keepdims=True))
    alpha = jnp.exp(m_old - m_new)
    p = jnp.exp(s - m_new)
    if valid is not None:
        p = jnp.where(valid, p, 0.0)
    l_sc[...] = alpha * l_sc[...] + jnp.sum(p, axis=-1, keepdims=True)
    acc_sc[...] = alpha * acc_sc[...] + _dot(p.astype(BF16), v)
    m_sc[...] = m_new


def _pad_rows(x, rows):
    return jnp.concatenate([x, jnp.zeros((rows - x.shape[0], x.shape[1]), x.dtype)], axis=0)


def _da_sample_kernel(pt_ref, lq1, lk1, lq2, lk2, subg_ref, q_ref, kn_ref, vn_ref, kp_ref, vp_ref,
                      o_ref, qbd_sc, m_sc, l_sc, acc_sc, *, tn, d, lam_init, n_pages):
    del pt_ref
    p = pl.program_id(1)
    groups = d // HEAD64
    rows = groups * tn

    @pl.when(p == 0)
    def _():
        qbd_sc[...] = _block_diag_queries(q_ref[0].astype(F32), groups, tn)
        m_sc[...] = jnp.full((rows, 1), NEG_BIG, F32)
        l_sc[...] = jnp.zeros((rows, 1), F32)
        acc_sc[...] = jnp.zeros((rows, d), F32)

    s = _dot_nt(qbd_sc[...], kp_ref[...].astype(BF16))
    _online_update(s, vp_ref[...].astype(BF16), None, m_sc, l_sc, acc_sc)

    @pl.when(p == n_pages - 1)
    def _():
        kn = _pad_rows(kn_ref[0], PAGE).astype(BF16)
        vn = _pad_rows(vn_ref[0], PAGE).astype(BF16)
        s_new = _dot_nt(qbd_sc[...], kn)
        qpos = lax.broadcasted_iota(I32, (rows, PAGE), 0) & (tn - 1)
        valid = lax.broadcasted_iota(I32, (rows, PAGE), 1) <= qpos
        _online_update(jnp.where(valid, s_new, NEG_BIG), vn, valid, m_sc, l_sc, acc_sc)
        lam = _da_lambda(lq1, lk1, lq2, lk2, lam_init)
        for h in range(d // LANES):
            cols = slice(h * LANES, (h + 1) * LANES)
            r1 = slice(2 * h * tn, (2 * h + 1) * tn)
            r2 = slice((2 * h + 1) * tn, (2 * h + 2) * tn)
            o = acc_sc[r1, cols] / l_sc[r1, :] - lam * (acc_sc[r2, cols] / l_sc[r2, :])
            o_ref[0, :, cols] = (_rms(o, subg_ref[...]) * (1.0 - lam_init)).astype(BF16)


def _da_sample(page_table, q3, kn3, vn3, k_pool, v_pool, lams, subg, layer, lam_init):
    bd, tn, d = q3.shape
    n_pages = page_table.shape[1]
    rows = (d // HEAD64) * tn
    seq = lambda: pl.BlockSpec((1, tn, d), lambda b, p, pt: (b, 0, 0))
    pool = lambda: pl.BlockSpec((None, None, PAGE, d), lambda b, p, pt: (layer, pt[b, p], 0, 0))
    lam_spec = _const_spec((1, HEAD64))
    kern = functools.partial(_da_sample_kernel, tn=tn, d=d, lam_init=lam_init, n_pages=n_pages)
    return pl.pallas_call(
        kern,
        grid_spec=pltpu.PrefetchScalarGridSpec(
            num_scalar_prefetch=1,
            grid=(bd, n_pages),
            in_specs=[lam_spec, lam_spec, lam_spec, lam_spec, _const_spec((1, LANES)),
                      seq(), seq(), seq(), pool(), pool()],
            out_specs=seq(),
            scratch_shapes=[pltpu.VMEM((rows, d), BF16), pltpu.VMEM((rows, 1), F32),
                            pltpu.VMEM((rows, 1), F32), pltpu.VMEM((rows, d), F32)]),
        out_shape=jax.ShapeDtypeStruct((bd, tn, d), BF16),
        compiler_params=_cparams("parallel", "arbitrary"),
        name="da_sample",
    )(page_table, *lams, subg, q3, kn3, vn3, k_pool, v_pool)


def _dsa_in_kernel(x_ref, g_ref, w_ref, cos_ref, sin_ref, q_ref, k_ref, v_ref, kb_ref, vb_ref,
                   qi_ref, misc_ref, *, d):
    h = _rms(x_ref[...], g_ref[...]).astype(BF16)
    y = _dot(h, w_ref[...])
    cos, sin = cos_ref[...], sin_ref[...]
    for c in range(d // LANES):
        sl = slice(c * LANES, (c + 1) * LANES)
        q = _rope(y[:, c * LANES:(c + 1) * LANES], cos, sin)
        q_ref[:, sl] = (q * HEAD64 ** -0.5).astype(BF16)
        k = _rope(y[:, d + c * LANES:d + (c + 1) * LANES], cos, sin)
        k_ref[:, sl] = k
        kb_ref[:, sl] = k.astype(BF16)
        v = y[:, 2 * d + c * LANES:2 * d + (c + 1) * LANES]
        v_ref[:, sl] = v
        vb_ref[:, sl] = v.astype(BF16)
    for c in range(IDX_HEADS * HEAD64 // LANES):
        qi = _rope(y[:, 3 * d + c * LANES:3 * d + (c + 1) * LANES], cos, sin)
        qi_ref[:, c * LANES:(c + 1) * LANES] = (qi * HEAD64 ** -0.5).astype(BF16)
    is_key = _lane_iota() < HEAD64
    base = 3 * d + IDX_HEADS * HEAD64
    misc_ref[...] = _rope(y[:, base:base + LANES], jnp.where(is_key, cos, 1.0), jnp.where(is_key, sin, 0.0))


def _dsa_in(x, g1, w, cos, sin):
    r, d = x.shape
    tm = min(r, ROW_TILE)
    ntab = cos.shape[0] // tm
    nqi = IDX_HEADS * HEAD64
    row = lambda w_: pl.BlockSpec((tm, w_), lambda i: (i, 0))
    tab = lambda: pl.BlockSpec((tm, LANES), lambda i: (i % ntab, 0))
    return pl.pallas_call(
        functools.partial(_dsa_in_kernel, d=d),
        grid=(r // tm,),
        in_specs=[row(d), _const_spec((1, d)), _const_spec(w.shape), tab(), tab()],
        out_specs=[row(d), row(d), row(d), row(d), row(d), row(nqi), row(LANES)],
        out_shape=[jax.ShapeDtypeStruct((r, d), BF16), jax.ShapeDtypeStruct((r, d), F32),
                   jax.ShapeDtypeStruct((r, d), F32), jax.ShapeDtypeStruct((r, d), BF16),
                   jax.ShapeDtypeStruct((r, d), BF16), jax.ShapeDtypeStruct((r, nqi), BF16),
                   jax.ShapeDtypeStruct((r, LANES), F32)],
        compiler_params=_cparams("parallel"),
        name="dsa_in",
    )(x, g1, w, cos, sin)


def _sortable_key(score):
    bits = lax.bitcast_convert_type(score + 0.0, I32)
    return bits ^ ((bits >> 31) & 0x7FFFFFFF)


def _select_topk(key, pos, krow, count, pos_bits):
    def ge_step(it, ans):
        cand = ans | lax.shift_left(jnp.int32(1), 30 - it)
        return jnp.where(count(key >= cand) >= krow, cand, ans)

    ans = jnp.where(count(key >= 0) >= krow, jnp.int32(0), jnp.int32(INT_MIN))
    ans = lax.fori_loop(0, 31, ge_step, ans)
    gt = key > ans
    eq = key == ans
    need = krow - count(gt)

    def pos_step(it, cut):
        cand = cut | lax.shift_left(jnp.int32(1), pos_bits - 1 - it)
        return jnp.where(count(eq & (pos < cand)) < need, cand, cut)

    cut = lax.fori_loop(0, pos_bits, pos_step, jnp.zeros(krow.shape, I32))
    return gt | (eq & (pos <= cut))


def _dsa_prompt_kernel(qi_ref, wq_ref, kim_ref, q_ref, k_ref, v_ref, o_ref, kid_sc, *, tq, t, d, ksel):
    i = pl.program_id(1)
    lane = _lane_iota()
    lo_half = lane < HEAD64

    @pl.when(i == 0)
    def _():
        lo = jnp.where(lo_half, kim_ref[...], 0.0)
        kid_sc[...] = (lo + pltpu.roll(lo, HEAD64, 1)).astype(BF16)

    wt = wq_ref[...] * IDX_HEADS ** -0.5
    kid = kid_sc[...]
    score = jnp.zeros((tq, t), F32)
    for h in range(IDX_HEADS):
        qc = qi_ref[:, (h // 2) * LANES:(h // 2 + 1) * LANES]
        qp = jnp.where(lo_half if h % 2 == 0 else ~lo_half, qc, jnp.zeros_like(qc))
        dots = _dot_nt(qp, kid)
        score = score + wt[:, HEAD64 + h:HEAD64 + h + 1] * jnp.maximum(dots, 0.0)

    qpos = i * tq + lax.broadcasted_iota(I32, (tq, 1), 0)
    kpos = lax.broadcasted_iota(I32, (tq, t), 1)
    key = jnp.where(kpos <= qpos, _sortable_key(score), INT_MIN)
    krow = jnp.minimum(qpos + 1, ksel).astype(F32)
    count = lambda mask: jnp.sum(jnp.where(mask, 1.0, 0.0), axis=-1, keepdims=True)
    sel = _select_topk(key, kpos, krow, count, max(1, (t - 1).bit_length()))
    bias = jnp.where(sel, 0.0, NEG_BIG)

    for c in range(d // LANES):
        cols = slice(c * LANES, (c + 1) * LANES)
        qc = q_ref[:, cols]
        kc = k_ref[:, cols]
        vc = v_ref[:, cols]
        halves = []
        for half in range(2):
            qp = jnp.where(lo_half if half == 0 else ~lo_half, qc, jnp.zeros_like(qc))
            s = _dot_nt(qp, kc) + bias
            p = jnp.exp(s - jnp.max(s, axis=-1, keepdims=True))
            l = jnp.sum(p, axis=-1, keepdims=True)
            halves.append(_dot(p.astype(BF16), vc) / l)
        o_ref[:, cols] = jnp.where(lo_half, halves[0], halves[1]).astype(BF16)


def _dsa_prompt(qi, misc, q, kb, vb, bsz, t, ksel):
    r, d = q.shape
    tq = min(t, DSA_TQ)
    nq = t // tq
    nqi = qi.shape[1]
    qrow = lambda w: pl.BlockSpec((tq, w), lambda b, i: (b * nq + i, 0))
    seq = lambda w: pl.BlockSpec((t, w), lambda b, i: (b, 0))
    return pl.pallas_call(
        functools.partial(_dsa_prompt_kernel, tq=tq, t=t, d=d, ksel=ksel),
        grid=(bsz, nq),
        in_specs=[qrow(nqi), qrow(LANES), seq(LANES), qrow(d), seq(d), seq(d)],
        out_specs=qrow(d),
        out_shape=jax.ShapeDtypeStruct((r, d), BF16),
        scratch_shapes=[pltpu.VMEM((t, LANES), BF16)],
        compiler_params=_cparams("parallel", "arbitrary"),
        name="dsa_prompt",
    )(qi, misc, misc, q, kb, vb)


def _dsa_sample_scores_kernel(pt_ref, qi_ref, misc_ref, pool_ref, sc_ref, qst_sc, w_sc, *, tn, n_pages):
    del pt_ref
    p = pl.program_id(1)

    @pl.when(p == 0)
    def _():
        qi = qi_ref[0].astype(F32)
        misc = misc_ref[0]
        qst_sc[...] = jnp.concatenate(
            [qi[:, h * HEAD64:(h + 1) * HEAD64] for h in range(IDX_HEADS)], axis=0).astype(BF16)
        w_sc[...] = jnp.concatenate(
            [misc[:, HEAD64 + h:HEAD64 + h + 1] for h in range(IDX_HEADS)], axis=0) * IDX_HEADS ** -0.5

    def scores(keys):
        dots = _dot_nt(qst_sc[...], keys.astype(BF16))
        sc = w_sc[...] * jnp.maximum(dots, 0.0)
        out = sc[0:tn, :]
        for h in range(1, IDX_HEADS):
            out = out + sc[h * tn:(h + 1) * tn, :]
        return out

    @pl.when(p < n_pages)
    def _():
        sc_ref[0, 0] = scores(pool_ref[...])

    @pl.when(p == n_pages)
    def _():
        sc_ref[0, 0] = scores(_pad_rows(misc_ref[0][:, :HEAD64], PAGE))


def _dsa_sample_scores(page_table, qi3, misc3, idx_pool, layer):
    bd, tn, nqi = qi3.shape
    n_pages = page_table.shape[1]
    kern = functools.partial(_dsa_sample_scores_kernel, tn=tn, n_pages=n_pages)
    return pl.pallas_call(
        kern,
        grid_spec=pltpu.PrefetchScalarGridSpec(
            num_scalar_prefetch=1,
            grid=(bd, n_pages + 1),
            in_specs=[pl.BlockSpec((1, tn, nqi), lambda b, p, pt: (b, 0, 0)),
                      pl.BlockSpec((1, tn, LANES), lambda b, p, pt: (b, 0, 0)),
                      pl.BlockSpec((None, None, PAGE, HEAD64),
                                   lambda b, p, pt: (layer, pt[b, jnp.minimum(p, n_pages - 1)], 0, 0))],
            out_specs=pl.BlockSpec((1, 1, tn, PAGE), lambda b, p, pt: (b, p, 0, 0)),
            scratch_shapes=[pltpu.VMEM((IDX_HEADS * tn, HEAD64), BF16),
                            pltpu.VMEM((IDX_HEADS * tn, 1), F32)]),
        out_shape=jax.ShapeDtypeStruct((bd, n_pages + 1, tn, PAGE), F32),
        compiler_params=_cparams("parallel", "arbitrary"),
        name="dsa_sample_scores",
    )(page_table, qi3, misc3, idx_pool)


def _dsa_sample_kernel(pt_ref, sc_ref, q_ref, kn_ref, vn_ref, kp_ref, vp_ref, o_ref,
                       bias_sc, qbd_sc, m_sc, l_sc, acc_sc, *, tn, d, n_pages, ksel):
    del pt_ref
    p = pl.program_id(1)
    groups = d // HEAD64
    rows = groups * tn

    @pl.when(p == 0)
    def _():
        shape = (n_pages + 1, tn, PAGE)
        page = lax.broadcasted_iota(I32, shape, 0)
        qidx = lax.broadcasted_iota(I32, shape, 1)
        lane = lax.broadcasted_iota(I32, shape, 2)
        admissible = (page < n_pages) | (lane <= qidx)
        key = jnp.where(admissible, _sortable_key(sc_ref[0]), INT_MIN)
        krow = jnp.full((1, tn, 1), float(ksel), F32)
        count = lambda mask: jnp.sum(jnp.sum(jnp.where(mask, 1.0, 0.0), axis=0, keepdims=True),
                                     axis=-1, keepdims=True)
        pos = page * PAGE + lane
        sel = _select_topk(key, pos, krow, count, ((n_pages + 1) * PAGE - 1).bit_length())
        bias_sc[...] = jnp.where(sel, 0.0, NEG_BIG)
        qbd_sc[...] = _block_diag_queries(q_ref[0].astype(F32), groups, tn)
        m_sc[...] = jnp.full((rows, 1), NEG_BIG, F32)
        l_sc[...] = jnp.zeros((rows, 1), F32)
        acc_sc[...] = jnp.zeros((rows, d), F32)

    def attend(keys, vals):
        bias = jnp.concatenate([bias_sc[p]] * groups, axis=0)
        s = _dot_nt(qbd_sc[...], keys.astype(BF16)) + bias
        _online_update(s, vals.astype(BF16), bias == 0.0, m_sc, l_sc, acc_sc)

    @pl.when(p < n_pages)
    def _():
        attend(kp_ref[...], vp_ref[...])

    @pl.when(p == n_pages)
    def _():
        attend(_pad_rows(kn_ref[0], PAGE), _pad_rows(vn_ref[0], PAGE))
        lo_half = _lane_iota() < HEAD64
        for c in range(d // LANES):
            cols = slice(c * LANES, (c + 1) * LANES)
            r0 = slice(2 * c * tn, (2 * c + 1) * tn)
            r1 = slice((2 * c + 1) * tn, (2 * c + 2) * tn)
            o = jnp.where(lo_half, acc_sc[r0, cols] / l_sc[r0, :], acc_sc[r1, cols] / l_sc[r1, :])
            o_ref[0, :, cols] = o.astype(BF16)


def _dsa_sample(page_table, scores, q3, kn3, vn3, k_pool, v_pool, layer, ksel):
    bd, tn, d = q3.shape
    n_pages = page_table.shape[1]
    rows = (d // HEAD64) * tn
    seq = lambda: pl.BlockSpec((1, tn, d), lambda b, p, pt: (b, 0, 0))
    pool = lambda: pl.BlockSpec((None, None, PAGE, d),
                                lambda b, p, pt: (layer, pt[b, jnp.minimum(p, n_pages - 1)], 0, 0))
    kern = functools.partial(_dsa_sample_kernel, tn=tn, d=d, n_pages=n_pages, ksel=ksel)
    return pl.pallas_call(
        kern,
        grid_spec=pltpu.PrefetchScalarGridSpec(
            num_scalar_prefetch=1,
            grid=(bd, n_pages + 1),
            in_specs=[pl.BlockSpec((1, n_pages + 1, tn, PAGE), lambda b, p, pt: (b, 0, 0, 0)),
                      seq(), seq(), seq(), pool(), pool()],
            out_specs=seq(),
            scratch_shapes=[pltpu.VMEM((n_pages + 1, tn, PAGE), F32), pltpu.VMEM((rows, d), BF16),
                            pltpu.VMEM((rows, 1), F32), pltpu.VMEM((rows, 1), F32),
                            pltpu.VMEM((rows, d), F32)]),
        out_shape=jax.ShapeDtypeStruct((bd, tn, d), BF16),
        compiler_params=_cparams("parallel", "arbitrary"),
        name="dsa_sample",
    )(page_table, scores, q3, kn3, vn3, k_pool, v_pool)


def _hg_in_kernel(x_ref, g_ref, w_ref, lbw_ref, q_ref, k_ref, lf_ref, v_ref, gs_ref, *, d, layer):
    h = _rms(x_ref[...], g_ref[...]).astype(BF16)
    y = _dot(h, w_ref[...])
    lbw = lbw_ref[...]
    e = jnp.exp(lbw - jnp.max(lbw, axis=0, keepdims=True))
    sm = e / jnp.sum(e, axis=0, keepdims=True)
    lb = jnp.sum(sm[1:layer + 1, :], axis=0, keepdims=True)
    q, fz, v, g = y[:, :d], y[:, d:2 * d], y[:, 2 * d:3 * d], y[:, 3 * d:]
    f = lb + (1.0 - lb) * _sigmoid(fz)
    q_ref[...] = q * _sigmoid(q)
    k_ref[...] = 1.0 - f
    lf_ref[...] = jnp.log(f)
    v_ref[...] = v
    gs_ref[...] = g * _sigmoid(g)


def _hg_in(x, g1, w, lbw, layer):
    r, d = x.shape
    tm = min(r, ROW_TILE)
    row = lambda: pl.BlockSpec((tm, d), lambda i: (i, 0))
    return pl.pallas_call(
        functools.partial(_hg_in_kernel, d=d, layer=layer),
        grid=(r // tm,),
        in_specs=[row(), _const_spec((1, d)), _const_spec((d, 4 * d)), _const_spec(lbw.shape)],
        out_specs=[row()] * 5,
        out_shape=[jax.ShapeDtypeStruct((r, d), F32)] * 5,
        compiler_params=_cparams("parallel"),
        name="hg_in",
    )(x, g1, w, lbw)


def _hg_rec_kernel(q_ref, k_ref, lf_ref, v_ref, gs_ref, s0_ref, gn_ref, o_ref, s_ref, st_sc, *, tb, nt):
    j = pl.program_id(2)

    @pl.when(j == 0)
    def _():
        st_sc[...] = s0_ref[...].T

    c = HG_C
    r_io = lax.broadcasted_iota(I32, (c, c), 0)
    c_io = lax.broadcasted_iota(I32, (c, c), 1)
    causal = r_io >= c_io
    tri = jnp.where(causal, 1.0, 0.0)
    row = lax.broadcasted_iota(I32, (c, 1), 0)
    nch = max(1, tb // c)
    for ci in range(nch):
        if tb >= c:
            rows = slice(ci * c, (ci + 1) * c)
            load = lambda ref: ref[rows, :]
        else:
            load = lambda ref: _pad_rows(ref[...], c)
        q, k, lf, v = load(q_ref), load(k_ref), load(lf_ref), load(v_ref)
        b = jnp.dot(tri, lf, preferred_element_type=F32, precision=lax.Precision.HIGHEST)
        st = st_sc[...]
        o = _dot_nt((q * jnp.exp(b)).astype(BF16), st.astype(BF16))
        slabs = []
        for blk in range(c // HG_SUB):
            lo, hi = blk * HG_SUB, (blk + 1) * HG_SUB
            anchor = b[lo - 1:lo, :] if blk > 0 else jnp.zeros((1, LANES), F32)
            qb = q[lo:hi, :] * jnp.exp(b[lo:hi, :] - anchor)
            kb = k * jnp.exp(jnp.where(row < hi, anchor - b, -jnp.inf))
            slabs.append(_dot_nt(qb.astype(BF16), kb.astype(BF16)))
        a = jnp.where(causal, jnp.concatenate(slabs, axis=0), 0.0)
        o = o + _dot(a.astype(BF16), v.astype(BF16))
        b_last = b[c - 1:c, :]
        kd = k * jnp.exp(b_last - b)
        st_sc[...] = st * jnp.exp(b_last) + _dot(v.T.astype(BF16), kd.astype(BF16))
        og = _rms(o, gn_ref[...])
        if tb >= c:
            o_ref[rows, :] = (og * gs_ref[rows, :]).astype(BF16)
        else:
            o_ref[...] = (og[:tb, :] * gs_ref[...]).astype(BF16)

    @pl.when(j == nt - 1)
    def _():
        s_ref[...] = st_sc[...].T


def _hg_rec(q, k, lf, v, gs, s0, gn, bsz, t):
    r, d = q.shape
    heads = d // LANES
    tb = min(t, HG_TB)
    nt = t // tb
    blk = lambda: pl.BlockSpec((tb, LANES), lambda b, h, j: (b * nt + j, h))
    state = lambda: pl.BlockSpec((None, None, LANES, LANES), lambda b, h, j: (b, h, 0, 0))
    return pl.pallas_call(
        functools.partial(_hg_rec_kernel, tb=tb, nt=nt),
        grid=(bsz, heads, nt),
        in_specs=[blk(), blk(), blk(), blk(), blk(), state(), _const_spec((1, LANES))],
        out_specs=[blk(), state()],
        out_shape=[jax.ShapeDtypeStruct((r, d), BF16),
                   jax.ShapeDtypeStruct((bsz, heads, LANES, LANES), F32)],
        scratch_shapes=[pltpu.VMEM((LANES, LANES), F32)],
        compiler_params=_cparams("parallel", "parallel", "arbitrary"),
        name="hg_rec",
    )(q, k, lf, v, gs, s0, gn)


def kernel(x_prompt, x_sample, state_conv, cache_da_k, cache_da_v, cache_dsa_k, cache_dsa_v, cache_dsa_idx_k, state_hgrn, page_table, norm1_g, norm2_g, final_g, mlp_w1, mlp_w2, cv_w1, cv_b1, cv_dw, cv_dwb, cv_ln_g, cv_ln_b, cv_w2, cv_b2, da_w_in, da_lq1, da_lk1, da_lq2, da_lk2, da_subln_g, da_wo, dsa_w_in, dsa_wo, hg_w_in, hg_lb, hg_norm_g, hg_wo):
    bp, t, d = x_prompt.shape
    bd, tn, _ = x_sample.shape
    n_pages = page_table.shape[1]
    past_len = n_pages * PAGE
    depth = norm1_g.shape[0]
    assert d % LANES == 0 and t % ROW_TILE == 0 and tn & (tn - 1) == 0 and tn <= SUBLANES

    xp = x_prompt.reshape(bp * t, d)
    xs = x_sample.reshape(bd * tn, d)
    cos_p, sin_p = _rope_tables(jnp.arange(t))
    cos_s, sin_s = _rope_tables(past_len + jnp.arange(tn))
    cos_s, sin_s = jnp.tile(cos_s, (bd, 1)), jnp.tile(sin_s, (bd, 1))
    ksel_p = min(DSA_TOPK, t // 4)
    ksel_s = min(DSA_TOPK, (past_len + tn) // 4)
    zero_bias = jnp.zeros((1, d), F32)
    row1 = lambda a: a.reshape(1, -1)
    new = {n: [] for n in ('conv_p', 'conv_s', 'dak_p', 'dav_p', 'dak_s', 'dav_s', 'dsak_p', 'dsav_p', 'dsai_p',
                           'dsak_s', 'dsav_s', 'dsai_s', 'hg_p', 'hg_s')}

    for layer in range(depth):
        kind, j = layer % 4, layer // 4
        g1 = row1(norm1_g[layer])
        bo = zero_bias
        if kind == 0:
            pad = lambda a, n: jnp.concatenate([jnp.zeros(a.shape[:1] + (n,) + a.shape[2:], a.dtype), a], axis=1)
            dw = jnp.concatenate([cv_dw[j], jnp.zeros((CONV_HALO - CONV_W, d), F32)], axis=0)[:, None, :]
            cw = (g1, cv_w1[j].astype(BF16), row1(cv_b1[j]), dw, row1(cv_dwb[j]), row1(cv_ln_g[j]),
                  row1(cv_ln_b[j]))
            halo_pad = CONV_HALO - (CONV_W - 1)
            op, tail_p = _conv_mixer(xp.reshape(bp, t, d), jnp.zeros((bp, CONV_HALO, d), F32), *cw,
                                     nb=1, tt=ROW_TILE)
            os_, tail_s = _conv_mixer(xs.reshape(bd, tn, d), pad(state_conv[j], halo_pad), *cw, nb=bd, tt=tn)
            op, os_ = op.reshape(bp * t, d), os_.reshape(bd * tn, d)
            new['conv_p'].append(tail_p[:, halo_pad:])
            new['conv_s'].append(tail_s[:, halo_pad:])
            wo, bo = cv_w2[j].astype(BF16), row1(cv_b2[j])
        elif kind == 1:
            lam_init = 0.8 - 0.6 * math.exp(-0.3 * layer)
            lams = (row1(da_lq1[j]), row1(da_lk1[j]), row1(da_lq2[j]), row1(da_lk2[j]))
            subg = row1(da_subln_g[j])
            w_in = da_w_in[j].astype(BF16)
            qp, kp, vp, kbp, vbp = _da_in(xp, g1, w_in, cos_p, sin_p)
            qs, ks, vs, _, _ = _da_in(xs, g1, w_in, cos_s, sin_s)
            op = _da_prompt(qp, kbp, vbp, lams, subg, bp, t, lam_init)
            pool_shape = cache_da_k.shape[:3] + (d,)
            os_ = _da_sample(page_table, qs.reshape(bd, tn, d), ks.reshape(bd, tn, d), vs.reshape(bd, tn, d),
                             cache_da_k.reshape(pool_shape), cache_da_v.reshape(pool_shape), lams, subg, j,
                             lam_init).reshape(bd * tn, d)
            hd = (d // LANES, LANES)
            new['dak_p'].append(kp.reshape((bp, t) + hd))
            new['dav_p'].append(vp.reshape((bp, t) + hd))
            new['dak_s'].append(ks.reshape((bd, tn) + hd))
            new['dav_s'].append(vs.reshape((bd, tn) + hd))
            wo = da_wo[j].astype(BF16)
        elif kind == 2:
            w_in = dsa_w_in[j]
            w_in = jnp.concatenate([w_in, jnp.zeros((d, -w_in.shape[1] % LANES), F32)], axis=1).astype(BF16)
            qp, kp, vp, kbp, vbp, qip, mp_ = _dsa_in(xp, g1, w_in, cos_p, sin_p)
            qs, ks, vs, _, _, qis, ms_ = _dsa_in(xs, g1, w_in, cos_s, sin_s)
            op = _dsa_prompt(qip, mp_, qp, kbp, vbp, bp, t, ksel_p)
            pool_shape = cache_dsa_k.shape[:3] + (d,)
            scores = _dsa_sample_scores(page_table, qis.reshape(bd, tn, -1), ms_.reshape(bd, tn, LANES),
                                        cache_dsa_idx_k, j)
            os_ = _dsa_sample(page_table, scores, qs.reshape(bd, tn, d), ks.reshape(bd, tn, d),
                              vs.reshape(bd, tn, d), cache_dsa_k.reshape(pool_shape),
                              cache_dsa_v.reshape(pool_shape), j, ksel_s).reshape(bd * tn, d)
            hd = (d // HEAD64, HEAD64)
            new['dsak_p'].append(kp.reshape((bp, t) + hd))
            new['dsav_p'].append(vp.reshape((bp, t) + hd))
            new['dsai_p'].append(mp_[:, :HEAD64].reshape(bp, t, HEAD64))
            new['dsak_s'].append(ks.reshape((bd, tn) + hd))
            new['dsav_s'].append(vs.reshape((bd, tn) + hd))
            new['dsai_s'].append(ms_[:, :HEAD64].reshape(bd, tn, HEAD64))
            wo = dsa_wo[j].astype(BF16)
        else:
            w_in = hg_w_in[j].astype(BF16)
            gn = row1(hg_norm_g[j])
            heads = d // LANES
            hp = _hg_in(xp, g1, w_in, hg_lb, layer)
            hs = _hg_in(xs, g1, w_in, hg_lb, layer)
            op, sp = _hg_rec(*hp, jnp.zeros((bp, heads, LANES, LANES), F32), gn, bp, t)
            os_, ss = _hg_rec(*hs, state_hgrn[j], gn, bd, tn)
            new['hg_p'].append(sp.astype(state_hgrn.dtype))
            new['hg_s'].append(ss.astype(state_hgrn.dtype))
            wo = hg_wo[j].astype(BF16)

        final = layer == depth - 1
        post = (wo, bo, row1(norm2_g[layer]), mlp_w1[layer].astype(BF16), mlp_w2[layer].astype(BF16),
                row1(final_g))
        xp = _post(xp, op, *post, final=final)
        xs = _post(xs, os_, *post, final=final)

    return (xp.reshape(bp, t, d), xs.reshape(bd, tn, d),
            jnp.stack(new['conv_p']), jnp.stack(new['conv_s']),
            jnp.stack(new['dak_p']), jnp.stack(new['dav_p']), jnp.stack(new['dak_s']), jnp.stack(new['dav_s']),
            jnp.stack(new['dsak_p']), jnp.stack(new['dsav_p']), jnp.stack(new['dsai_p']),
            jnp.stack(new['dsak_s']), jnp.stack(new['dsav_s']), jnp.stack(new['dsai_s']),
            jnp.stack(new['hg_p']), jnp.stack(new['hg_s']))
```

```python
import functools
import math

import jax
import jax.numpy as jnp
from jax import lax
from jax.experimental import pallas as pl
from jax.experimental.pallas import tpu as pltpu

F32 = jnp.float32
BF16 = jnp.bfloat16
I32 = jnp.int32

NORM_EPS = 1e-6
LN_EPS = 1e-5
ROPE_THETA = 10000.0
ROPE_DIM = 64
CONV_W = 31
PAGE = 128
HEAD64 = 64
IDX_HEADS = 8
DSA_TOPK = 256

LANES = 128
SUBLANES = 8
CONV_HALO = 32
CONV_ROWS = 32
ROW_TILE = 256
ATT_TQ = 256
DSA_TQ = 128
DSA_KV_GROUPS = 4
SAMPLE_PP = 4
SCORE_PP = 8
HG_C = 128
HG_SUB = 16
HG_TB = 512
MLP_FCHUNK = 1024
VMEM_LIMIT = 56 * 1024 * 1024
NEG_BIG = -1e30
INT_MIN = -2 ** 31


def _cparams(*sem):
    return pltpu.CompilerParams(dimension_semantics=sem, vmem_limit_bytes=VMEM_LIMIT)


def _const_spec(shape):
    nd = len(shape)
    return pl.BlockSpec(shape, lambda *_: (0,) * nd, pipeline_mode=pl.Buffered(1))


def _dot(a, b):
    return jnp.dot(a, b, preferred_element_type=F32)


def _dot_nt(a, b):
    return lax.dot_general(a, b, (((1,), (1,)), ((), ())), preferred_element_type=F32)


def _rms(x, g):
    return x * lax.rsqrt(jnp.mean(x * x, axis=-1, keepdims=True) + NORM_EPS) * g


def _sigmoid(x):
    return 1.0 / (1.0 + jnp.exp(-x))


def _lane_iota(shape=(1, LANES)):
    return lax.broadcasted_iota(I32, shape, len(shape) - 1)


def _rope(xc, cos, sin_signed):
    first_half = (_lane_iota() & 32) == 0
    partner = jnp.where(first_half, pltpu.roll(xc, 96, 1), pltpu.roll(xc, 32, 1))
    return xc * cos + partner * sin_signed


def _rope_tables(pos):
    inv = ROPE_THETA ** (-jnp.arange(0, ROPE_DIM, 2, dtype=F32) / ROPE_DIM)
    ang = pos.astype(F32)[:, None] * inv[None, :]
    cos, sin = jnp.cos(ang), jnp.sin(ang)
    return jnp.tile(cos, (1, 4)), jnp.tile(jnp.concatenate([-sin, sin], axis=1), (1, 2))


def _pad_rows(x, rows):
    return jnp.concatenate([x, jnp.zeros((rows - x.shape[0], x.shape[1]), x.dtype)], axis=0)


def _post_kernel(x_ref, o_ref, wo_ref, bo_ref, g_ref, w1_ref, w2_ref, fg_ref, y_ref, h_ref, *, final):
    x1 = x_ref[...] + _dot(o_ref[...], wo_ref[...]) + bo_ref[...]
    h = _rms(x1, g_ref[...]).astype(BF16)
    f = w1_ref.shape[1]
    fc = min(f, MLP_FCHUNK)
    for c in range(f // fc):
        a = jnp.maximum(_dot(h, w1_ref[:, c * fc:(c + 1) * fc]), 0.0)
        h_ref[:, c * fc:(c + 1) * fc] = (a * a).astype(BF16)
    y = x1 + _dot(h_ref[...], w2_ref[...])
    if final:
        y = _rms(y, fg_ref[...])
    y_ref[...] = y


def _post(x, o, wo, bo, g2, w1, w2, fg, final):
    r, d = x.shape
    f = w1.shape[1]
    tm = min(r, ROW_TILE)
    row = lambda w: pl.BlockSpec((tm, w), lambda i: (i, 0))
    return pl.pallas_call(
        functools.partial(_post_kernel, final=final),
        grid=(r // tm,),
        in_specs=[row(d), row(d), _const_spec((d, d)), _const_spec((1, d)), _const_spec((1, d)),
                  _const_spec((d, f)), _const_spec((f, d)), _const_spec((1, d))],
        out_specs=row(d),
        out_shape=jax.ShapeDtypeStruct((r, d), F32),
        scratch_shapes=[pltpu.VMEM((tm, f), BF16)],
        compiler_params=_cparams("parallel"),
        name="post_mlp",
    )(x, o, wo, bo, g2, w1, w2, fg)


def _conv_kernel(x_ref, buf_ref, g_ref, w1_ref, b1_ref, dw_ref, dwb_ref, lng_ref, lnb_ref,
                 o_ref, tail_ref, ext_ref, y_ref, *, nb, tt, d, carry):
    @pl.when(pl.program_id(1) == 0)
    def _():
        ext_ref[:, 0:CONV_HALO, :] = buf_ref[...]

    x = x_ref[...].reshape(nb * tt, d)
    h = _rms(x, g_ref[...]).astype(BF16)
    ag = _dot(h, w1_ref[...]) + b1_ref[...]
    u = ag[:, :d] * _sigmoid(ag[:, d:])
    ext_ref[:, CONV_HALO:CONV_HALO + tt, :] = u.reshape(nb, tt, d)

    rs = min(tt, CONV_ROWS)
    for r in range(tt // rs):
        acc = jnp.zeros((nb, rs, d), F32)
        for k in range(CONV_W):
            off = r * rs + k + CONV_HALO - (CONV_W - 1)
            acc = acc + dw_ref[k] * ext_ref[:, off:off + rs, :]
        y_ref[:, r * rs:(r + 1) * rs, :] = acc + dwb_ref[...]

    y = y_ref[...]
    mu = jnp.mean(y, axis=-1, keepdims=True)
    yc = y - mu
    var = jnp.mean(yc * yc, axis=-1, keepdims=True)
    z = yc * lax.rsqrt(var + LN_EPS) * lng_ref[...] + lnb_ref[...]
    o_ref[...] = (z * _sigmoid(z)).astype(BF16)
    tail = ext_ref[:, tt:tt + CONV_HALO, :]
    tail_ref[...] = tail
    if carry:
        ext_ref[:, 0:CONV_HALO, :] = tail


def _conv_mixer(x3, buf, g1, w1, b1, dw, dwb, lng, lnb, nb, tt):
    b, t, d = x3.shape
    nt = t // tt
    kern = functools.partial(_conv_kernel, nb=nb, tt=tt, d=d, carry=nt > 1)
    return pl.pallas_call(
        kern,
        grid=(b // nb, nt),
        in_specs=[pl.BlockSpec((nb, tt, d), lambda i, j: (i, j, 0)),
                  pl.BlockSpec((nb, CONV_HALO, d), lambda i, j: (i, 0, 0)),
                  _const_spec((1, d)), _const_spec((d, 2 * d)), _const_spec((1, 2 * d)),
                  _const_spec((CONV_HALO, 1, d)), _const_spec((1, d)), _const_spec((1, d)),
                  _const_spec((1, d))],
        out_specs=[pl.BlockSpec((nb, tt, d), lambda i, j: (i, j, 0)),
                   pl.BlockSpec((nb, CONV_HALO, d), lambda i, j: (i, 0, 0))],
        out_shape=[jax.ShapeDtypeStruct((b, t, d), BF16),
                   jax.ShapeDtypeStruct((b, CONV_HALO, d), F32)],
        scratch_shapes=[pltpu.VMEM((nb, CONV_HALO + tt, d), F32), pltpu.VMEM((nb, tt, d), F32)],
        compiler_params=_cparams("parallel", "arbitrary"),
        name="conv_mixer",
    )(x3, buf, g1, w1, b1, dw, dwb, lng, lnb)


def _da_in_kernel(x_ref, g_ref, w_ref, cos_ref, sin_ref, q_ref, k_ref, v_ref, kb_ref, vb_ref, *, d):
    h = _rms(x_ref[...], g_ref[...]).astype(BF16)
    y = _dot(h, w_ref[...])
    cos, sin = cos_ref[...], sin_ref[...]
    for c in range(d // LANES):
        sl = slice(c * LANES, (c + 1) * LANES)
        q = _rope(y[:, c * LANES:(c + 1) * LANES], cos, sin)
        q_ref[:, sl] = (q * HEAD64 ** -0.5).astype(BF16)
        k = _rope(y[:, d + c * LANES:d + (c + 1) * LANES], cos, sin)
        k_ref[:, sl] = k
        kb_ref[:, sl] = k.astype(BF16)
        v = y[:, 2 * d + c * LANES:2 * d + (c + 1) * LANES]
        v_ref[:, sl] = v
        vb_ref[:, sl] = v.astype(BF16)


def _da_in(x, g1, w, cos, sin):
    r, d = x.shape
    tm = min(r, ROW_TILE)
    ntab = cos.shape[0] // tm
    row = lambda: pl.BlockSpec((tm, d), lambda i: (i, 0))
    tab = lambda: pl.BlockSpec((tm, LANES), lambda i: (i % ntab, 0))
    return pl.pallas_call(
        functools.partial(_da_in_kernel, d=d),
        grid=(r // tm,),
        in_specs=[row(), _const_spec((1, d)), _const_spec((d, 3 * d)), tab(), tab()],
        out_specs=[row(), row(), row(), row(), row()],
        out_shape=[jax.ShapeDtypeStruct((r, d), BF16), jax.ShapeDtypeStruct((r, d), F32),
                   jax.ShapeDtypeStruct((r, d), F32), jax.ShapeDtypeStruct((r, d), BF16),
                   jax.ShapeDtypeStruct((r, d), BF16)],
        compiler_params=_cparams("parallel"),
        name="da_in",
    )(x, g1, w, cos, sin)


def _da_lambda(lq1, lk1, lq2, lk2, lam_init):
    return (jnp.exp(jnp.sum(lq1[...] * lk1[...], axis=-1, keepdims=True))
            - jnp.exp(jnp.sum(lq2[...] * lk2[...], axis=-1, keepdims=True)) + lam_init)


def _da_prompt_kernel(lq1, lk1, lq2, lk2, subg_ref, q_ref, k_ref, v_ref, o_ref, *, tq, nq, lam_init):
    i = pl.program_id(2)
    lam = _da_lambda(lq1, lk1, lq2, lk2, lam_init)
    q = q_ref[...]
    lane = _lane_iota()
    zero = jnp.zeros_like(q)
    q1 = jnp.where(lane < HEAD64, q, zero)
    q2 = jnp.where(lane >= HEAD64, q, zero)
    keep = (lax.broadcasted_iota(I32, (tq, tq), 0) >= lax.broadcasted_iota(I32, (tq, tq), 1))

    def body(c):
        off = c * tq

        def softmax_parts(qm):
            sd = jnp.where(keep, _dot_nt(qm, k_ref[off:off + tq, :]), NEG_BIG)
            m = jnp.max(sd, axis=-1, keepdims=True)
            so = None
            if c > 0:
                so = _dot_nt(qm, k_ref[0:off, :])
                m = jnp.maximum(m, jnp.max(so, axis=-1, keepdims=True))
            pd = jnp.exp(sd - m)
            l = jnp.sum(pd, axis=-1, keepdims=True)
            po = None
            if c > 0:
                po = jnp.exp(so - m)
                l = l + jnp.sum(po, axis=-1, keepdims=True)
            return pd, po, l

        pd1, po1, l1 = softmax_parts(q1)
        pd2, po2, l2 = softmax_parts(q2)
        w1 = 1.0 / l1
        w2 = lam / l2
        o = _dot((pd1 * w1 - pd2 * w2).astype(BF16), v_ref[off:off + tq, :])
        if c > 0:
            o = o + _dot((po1 * w1 - po2 * w2).astype(BF16), v_ref[0:off, :])
        o_ref[...] = (_rms(o, subg_ref[...]) * (1.0 - lam_init)).astype(BF16)

    for c in range(nq):
        pl.when(i == c)(functools.partial(body, c))


def _da_prompt(q, kb, vb, lams, subg, bsz, t, lam_init):
    r, d = q.shape
    tq = min(t, ATT_TQ)
    nq = t // tq
    lam_spec = _const_spec((1, HEAD64))
    return pl.pallas_call(
        functools.partial(_da_prompt_kernel, tq=tq, nq=nq, lam_init=lam_init),
        grid=(bsz, d // LANES, nq),
        in_specs=[lam_spec, lam_spec, lam_spec, lam_spec, _const_spec((1, LANES)),
                  pl.BlockSpec((tq, LANES), lambda b, h, i: (b * nq + i, h)),
                  pl.BlockSpec((t, LANES), lambda b, h, i: (b, h)),
                  pl.BlockSpec((t, LANES), lambda b, h, i: (b, h))],
        out_specs=pl.BlockSpec((tq, LANES), lambda b, h, i: (b * nq + i, h)),
        out_shape=jax.ShapeDtypeStruct((r, d), BF16),
        compiler_params=_cparams("parallel", "parallel", "arbitrary"),
        name="da_prompt",
    )(*lams, subg, q, kb, vb)


def _online_update(s, valid, pv_fn, m_sc, l_sc, acc_sc):
    m_old = m_sc[...]
    m_new = jnp.maximum(m_old, jnp.max(s, axis=-1, keepdims=True))
    alpha = jnp.exp(m_old - m_new)
    p = jnp.exp(s - m_new)
    if valid is not None:
        p = jnp.where(valid, p, 0.0)
    l_sc[...] = alpha * l_sc[...] + jnp.sum(p, axis=-1, keepdims=True)
    acc_sc[...] = alpha * acc_sc[...] + pv_fn(p.astype(BF16))
    m_sc[...] = m_new


def _da_sample_kernel(pt_ref, lq1, lk1, lq2, lk2, subg_ref, q_ref, kn_ref, vn_ref, *rest,
                      tn, d, lam_init, nsteps, pp):
    del pt_ref
    kp_refs, vp_refs = rest[:pp], rest[pp:2 * pp]
    o_ref, qh_sc, own_sc, m_sc, l_sc, acc_sc = rest[2 * pp:]
    step = pl.program_id(1)
    heads = d // LANES
    gr = 2 * tn
    rows = heads * gr
    ncol = pp * PAGE * heads

    @pl.when(step == 0)
    def _():
        q = q_ref[0].astype(F32)
        first_map = lax.broadcasted_iota(I32, (gr, LANES), 0) < tn
        lo_half = _lane_iota((gr, LANES)) < HEAD64
        for h in range(heads):
            qc = q[:, h * LANES:(h + 1) * LANES]
            qh_sc[h * gr:(h + 1) * gr, :] = jnp.where(
                first_map == lo_half, jnp.concatenate([qc, qc], axis=0), 0.0).astype(BF16)
        row_head = lax.broadcasted_iota(I32, (rows, ncol), 0) >> int(math.log2(gr))
        col_head = lax.broadcasted_iota(I32, (rows, ncol), 1) & (heads - 1)
        own_sc[...] = jnp.where(row_head == col_head, 0.0, NEG_BIG)
        m_sc[...] = jnp.full((rows, 1), NEG_BIG, F32)
        l_sc[...] = jnp.zeros((rows, 1), F32)
        acc_sc[...] = jnp.zeros((rows, LANES), F32)

    flat = lambda refs: jnp.concatenate([r[...] for r in refs], axis=0).astype(BF16)
    s = _dot_nt(qh_sc[...], flat(kp_refs)) + own_sc[...]
    _online_update(s, None, lambda pb: _dot(pb, flat(vp_refs)), m_sc, l_sc, acc_sc)

    @pl.when(step == nsteps - 1)
    def _():
        fresh = lambda ref, h: _pad_rows(ref[0][:, h * LANES:(h + 1) * LANES], PAGE).astype(BF16)
        qpos = lax.broadcasted_iota(I32, (rows, PAGE), 0) & (tn - 1)
        valid = lax.broadcasted_iota(I32, (rows, PAGE), 1) <= qpos
        s_new = jnp.concatenate(
            [_dot_nt(qh_sc[h * gr:(h + 1) * gr, :], fresh(kn_ref, h)) for h in range(heads)], axis=0)
        pv = lambda pb: jnp.concatenate(
            [_dot(pb[h * gr:(h + 1) * gr, :], fresh(vn_ref, h)) for h in range(heads)], axis=0)
        _online_update(jnp.where(valid, s_new, NEG_BIG), valid, pv, m_sc, l_sc, acc_sc)
        lam = _da_lambda(lq1, lk1, lq2, lk2, lam_init)
        for h in range(heads):
            r1 = slice(h * gr, h * gr + tn)
            r2 = slice(h * gr + tn, (h + 1) * gr)
            o = acc_sc[r1, :] / l_sc[r1, :] - lam * (acc_sc[r2, :] / l_sc[r2, :])
            o_ref[0, :, h * LANES:(h + 1) * LANES] = (_rms(o, subg_ref[...]) * (1.0 - lam_init)).astype(BF16)


def _da_sample(page_table, q3, kn3, vn3, k_pool, v_pool, lams, subg, layer, lam_init):
    bd, tn, d = q3.shape
    n_pages = page_table.shape[1]
    heads = d // LANES
    pp = math.gcd(n_pages, SAMPLE_PP)
    nsteps = n_pages // pp
    rows = heads * 2 * tn
    assert heads & (heads - 1) == 0
    flat_shape = k_pool.shape[:2] + (PAGE * heads, LANES)
    k_pool, v_pool = k_pool.reshape(flat_shape), v_pool.reshape(flat_shape)
    seq = lambda: pl.BlockSpec((1, tn, d), lambda b, s, pt: (b, 0, 0))
    pool = lambda i: pl.BlockSpec((None, None, PAGE * heads, LANES),
                                  lambda b, s, pt: (layer, pt[b, s * pp + i], 0, 0))
    lam_spec = _const_spec((1, HEAD64))
    kern = functools.partial(_da_sample_kernel, tn=tn, d=d, lam_init=lam_init, nsteps=nsteps, pp=pp)
    return pl.pallas_call(
        kern,
        grid_spec=pltpu.PrefetchScalarGridSpec(
            num_scalar_prefetch=1,
            grid=(bd, nsteps),
            in_specs=[lam_spec, lam_spec, lam_spec, lam_spec, _const_spec((1, LANES)), seq(), seq(), seq()]
            + [pool(i) for i in range(pp)] + [pool(i) for i in range(pp)],
            out_specs=seq(),
            scratch_shapes=[pltpu.VMEM((rows, LANES), BF16), pltpu.VMEM((rows, pp * PAGE * heads), F32),
                            pltpu.VMEM((rows, 1), F32), pltpu.VMEM((rows, 1), F32),
                            pltpu.VMEM((rows, LANES), F32)]),
        out_shape=jax.ShapeDtypeStruct((bd, tn, d), BF16),
        compiler_params=_cparams("parallel", "arbitrary"),
        name="da_sample",
    )(page_table, *lams, subg, q3, kn3, vn3, *([k_pool] * pp), *([v_pool] * pp))


def _dsa_in_kernel(x_ref, g_ref, w_ref, cos_ref, sin_ref, q_ref, k_ref, v_ref, kb_ref, vb_ref,
                   qi_ref, misc_ref, *, d):
    h = _rms(x_ref[...], g_ref[...]).astype(BF16)
    y = _dot(h, w_ref[...])
    cos, sin = cos_ref[...], sin_ref[...]
    for c in range(d // LANES):
        sl = slice(c * LANES, (c + 1) * LANES)
        q = _rope(y[:, c * LANES:(c + 1) * LANES], cos, sin)
        q_ref[:, sl] = (q * HEAD64 ** -0.5).astype(BF16)
        k = _rope(y[:, d + c * LANES:d + (c + 1) * LANES], cos, sin)
        k_ref[:, sl] = k
        kb_ref[:, sl] = k.astype(BF16)
        v = y[:, 2 * d + c * LANES:2 * d + (c + 1) * LANES]
        v_ref[:, sl] = v
        vb_ref[:, sl] = v.astype(BF16)
    for c in range(IDX_HEADS * HEAD64 // LANES):
        qi = _rope(y[:, 3 * d + c * LANES:3 * d + (c + 1) * LANES], cos, sin)
        qi_ref[:, c * LANES:(c + 1) * LANES] = (qi * HEAD64 ** -0.5).astype(BF16)
    is_key = _lane_iota() < HEAD64
    base = 3 * d + IDX_HEADS * HEAD64
    misc_ref[...] = _rope(y[:, base:base + LANES], jnp.where(is_key, cos, 1.0), jnp.where(is_key, sin, 0.0))


def _dsa_in(x, g1, w, cos, sin):
    r, d = x.shape
    tm = min(r, ROW_TILE)
    ntab = cos.shape[0] // tm
    nqi = IDX_HEADS * HEAD64
    row = lambda w_: pl.BlockSpec((tm, w_), lambda i: (i, 0))
    tab = lambda: pl.BlockSpec((tm, LANES), lambda i: (i % ntab, 0))
    return pl.pallas_call(
        functools.partial(_dsa_in_kernel, d=d),
        grid=(r // tm,),
        in_specs=[row(d), _const_spec((1, d)), _const_spec(w.shape), tab(), tab()],
        out_specs=[row(d), row(d), row(d), row(d), row(d), row(nqi), row(LANES)],
        out_shape=[jax.ShapeDtypeStruct((r, d), BF16), jax.ShapeDtypeStruct((r, d), F32),
                   jax.ShapeDtypeStruct((r, d), F32), jax.ShapeDtypeStruct((r, d), BF16),
                   jax.ShapeDtypeStruct((r, d), BF16), jax.ShapeDtypeStruct((r, nqi), BF16),
                   jax.ShapeDtypeStruct((r, LANES), F32)],
        compiler_params=_cparams("parallel"),
        name="dsa_in",
    )(x, g1, w, cos, sin)


def _sortable_key(score):
    bits = lax.bitcast_convert_type(score + 0.0, I32)
    return bits ^ ((bits >> 31) & 0x7FFFFFFF)


def _select_topk(key, pos, krow, count, pos_bits):
    def ge_step(it, ans):
        cand = ans | lax.shift_left(jnp.int32(1), 30 - it)
        return jnp.where(count(key >= cand) >= krow, cand, ans)

    ans = jnp.where(count(key >= 0) >= krow, jnp.int32(0), jnp.int32(INT_MIN))
    ans = lax.fori_loop(0, 31, ge_step, ans)
    gt = key > ans
    eq = key == ans
    need = krow - count(gt)

    def pos_step(it, cut):
        cand = cut | lax.shift_left(jnp.int32(1), pos_bits - 1 - it)
        return jnp.where(count(eq & (pos < cand)) < need, cand, cut)

    cut = lax.fori_loop(0, pos_bits, pos_step, jnp.zeros(krow.shape, I32))
    return gt | (eq & (pos <= cut))


def _dsa_prompt_kernel(qi_ref, wq_ref, kim_ref, q_ref, k_ref, v_ref, o_ref, kid_sc, *, tq, nq, d, ksel):
    i = pl.program_id(1)
    lane = _lane_iota()
    lo_half = lane < HEAD64

    @pl.when(i == 0)
    def _():
        lo = jnp.where(lo_half, kim_ref[...], 0.0)
        kid_sc[...] = (lo + pltpu.roll(lo, HEAD64, 1)).astype(BF16)

    def body(kv):
        wt = wq_ref[...] * IDX_HEADS ** -0.5
        kid = kid_sc[0:kv, :]
        score = jnp.zeros((tq, kv), F32)
        for h in range(IDX_HEADS):
            qc = qi_ref[:, (h // 2) * LANES:(h // 2 + 1) * LANES]
            qp = jnp.where(lo_half if h % 2 == 0 else ~lo_half, qc, jnp.zeros_like(qc))
            dots = _dot_nt(qp, kid)
            score = score + wt[:, HEAD64 + h:HEAD64 + h + 1] * jnp.maximum(dots, 0.0)

        qpos = i * tq + lax.broadcasted_iota(I32, (tq, 1), 0)
        kpos = lax.broadcasted_iota(I32, (tq, kv), 1)
        key = jnp.where(kpos <= qpos, _sortable_key(score), INT_MIN)
        krow = jnp.minimum(qpos + 1, ksel).astype(F32)
        count = lambda mask: jnp.sum(jnp.where(mask, 1.0, 0.0), axis=-1, keepdims=True)
        sel = _select_topk(key, kpos, krow, count, max(1, (kv - 1).bit_length()))
        bias = jnp.where(sel, 0.0, NEG_BIG)

        for c in range(d // LANES):
            cols = slice(c * LANES, (c + 1) * LANES)
            qc = q_ref[:, cols]
            kc = k_ref[0:kv, cols]
            vc = v_ref[0:kv, cols]
            halves = []
            for half in range(2):
                qp = jnp.where(lo_half if half == 0 else ~lo_half, qc, jnp.zeros_like(qc))
                s = _dot_nt(qp, kc) + bias
                p = jnp.exp(s - jnp.max(s, axis=-1, keepdims=True))
                l = jnp.sum(p, axis=-1, keepdims=True)
                halves.append(_dot(p.astype(BF16), vc) / l)
            o_ref[:, cols] = jnp.where(lo_half, halves[0], halves[1]).astype(BF16)

    groups = math.gcd(nq, DSA_KV_GROUPS)
    per = nq // groups
    for g in range(groups):
        pl.when((i >= g * per) & (i < (g + 1) * per))(functools.partial(body, (g + 1) * per * tq))


def _dsa_prompt(qi, misc, q, kb, vb, bsz, t, ksel):
    r, d = q.shape
    tq = min(t, DSA_TQ)
    nq = t // tq
    nqi = qi.shape[1]
    qrow = lambda w: pl.BlockSpec((tq, w), lambda b, i: (b * nq + i, 0))
    seq = lambda w: pl.BlockSpec((t, w), lambda b, i: (b, 0))
    return pl.pallas_call(
        functools.partial(_dsa_prompt_kernel, tq=tq, nq=nq, d=d, ksel=ksel),
        grid=(bsz, nq),
        in_specs=[qrow(nqi), qrow(LANES), seq(LANES), qrow(d), seq(d), seq(d)],
        out_specs=qrow(d),
        out_shape=jax.ShapeDtypeStruct((r, d), BF16),
        scratch_shapes=[pltpu.VMEM((t, LANES), BF16)],
        compiler_params=_cparams("parallel", "arbitrary"),
        name="dsa_prompt",
    )(qi, misc, misc, q, kb, vb)


def _dsa_sample_scores_kernel(pt_ref, qi_ref, misc_ref, *rest, tn, nsteps, pp):
    del pt_ref
    pool_refs = rest[:pp]
    sc_ref, scn_ref, qst_sc, w_sc = rest[pp:]
    step = pl.program_id(1)

    @pl.when(step == 0)
    def _():
        qi = qi_ref[0].astype(F32)
        misc = misc_ref[0]
        qst_sc[...] = jnp.concatenate(
            [qi[:, h * HEAD64:(h + 1) * HEAD64] for h in range(IDX_HEADS)], axis=0).astype(BF16)
        w_sc[...] = jnp.concatenate(
            [misc[:, HEAD64 + h:HEAD64 + h + 1] for h in range(IDX_HEADS)], axis=0) * IDX_HEADS ** -0.5

    def head_sum(dots):
        sc = w_sc[...] * jnp.maximum(dots, 0.0)
        out = sc[0:tn, :]
        for h in range(1, IDX_HEADS):
            out = out + sc[h * tn:(h + 1) * tn, :]
        return out

    for i in range(pp):
        sc_ref[0, i] = head_sum(_dot(qst_sc[...], pool_refs[i][...].astype(BF16)))

    @pl.when(step == nsteps - 1)
    def _():
        fresh = _pad_rows(misc_ref[0][:, :HEAD64], PAGE).astype(BF16)
        scn_ref[0] = head_sum(_dot_nt(qst_sc[...], fresh))


def _dsa_sample_scores(page_table, qi3, misc3, idx_pool_t, layer):
    bd, tn, nqi = qi3.shape
    n_pages = page_table.shape[1]
    pp = math.gcd(n_pages, SCORE_PP)
    nsteps = n_pages // pp
    pool = lambda i: pl.BlockSpec((None, None, HEAD64, PAGE),
                                  lambda b, s, pt: (layer, pt[b, s * pp + i], 0, 0))
    kern = functools.partial(_dsa_sample_scores_kernel, tn=tn, nsteps=nsteps, pp=pp)
    return pl.pallas_call(
        kern,
        grid_spec=pltpu.PrefetchScalarGridSpec(
            num_scalar_prefetch=1,
            grid=(bd, nsteps),
            in_specs=[pl.BlockSpec((1, tn, nqi), lambda b, s, pt: (b, 0, 0)),
                      pl.BlockSpec((1, tn, LANES), lambda b, s, pt: (b, 0, 0))]
            + [pool(i) for i in range(pp)],
            out_specs=[pl.BlockSpec((1, pp, tn, PAGE), lambda b, s, pt: (b, s, 0, 0)),
                       pl.BlockSpec((1, tn, PAGE), lambda b, s, pt: (b, 0, 0))],
            scratch_shapes=[pltpu.VMEM((IDX_HEADS * tn, HEAD64), BF16),
                            pltpu.VMEM((IDX_HEADS * tn, 1), F32)]),
        out_shape=[jax.ShapeDtypeStruct((bd, n_pages, tn, PAGE), F32),
                   jax.ShapeDtypeStruct((bd, tn, PAGE), F32)],
        compiler_params=_cparams("parallel", "arbitrary"),
        name="dsa_sample_scores",
    )(page_table, qi3, misc3, *([idx_pool_t] * pp))


def _block_diag_queries(q, groups, tn):
    d = q.shape[1]
    qt = jnp.concatenate([q] * groups, axis=0)
    shift = int(math.log2(tn))
    rg = lax.broadcasted_iota(I32, (groups * tn, d), 0) >> shift
    cg = lax.broadcasted_iota(I32, (groups * tn, d), 1) >> 6
    return jnp.where(rg == cg, qt, 0.0).astype(BF16)


def _dsa_sample_kernel(pt_ref, sc_ref, scn_ref, q_ref, kn_ref, vn_ref, *rest,
                       tn, d, n_pages, nsteps, pp, ksel):
    del pt_ref
    kp_refs, vp_refs = rest[:pp], rest[pp:2 * pp]
    o_ref, bias_sc, qbd_sc, m_sc, l_sc, acc_sc = rest[2 * pp:]
    step = pl.program_id(1)
    groups = d // HEAD64
    rows = groups * tn

    @pl.when(step == 0)
    def _():
        shape = (n_pages + 1, tn, PAGE)
        page = lax.broadcasted_iota(I32, shape, 0)
        qidx = lax.broadcasted_iota(I32, shape, 1)
        lane = lax.broadcasted_iota(I32, shape, 2)
        score = jnp.concatenate([sc_ref[0], scn_ref[...]], axis=0)
        key = jnp.where((page < n_pages) | (lane <= qidx), _sortable_key(score), INT_MIN)
        krow = jnp.full((1, tn, 1), float(ksel), F32)
        count = lambda mask: jnp.sum(jnp.sum(jnp.where(mask, 1.0, 0.0), axis=0, keepdims=True),
                                     axis=-1, keepdims=True)
        sel = _select_topk(key, page * PAGE + lane, krow, count, ((n_pages + 1) * PAGE - 1).bit_length())
        bias_sc[...] = jnp.where(sel, 0.0, NEG_BIG)
        qbd_sc[...] = _block_diag_queries(q_ref[0].astype(F32), groups, tn)
        m_sc[...] = jnp.full((rows, 1), NEG_BIG, F32)
        l_sc[...] = jnp.zeros((rows, 1), F32)
        acc_sc[...] = jnp.zeros((rows, d), F32)

    def tiled_bias(page):
        return jnp.concatenate([bias_sc[page]] * groups, axis=0)

    @pl.when(step < nsteps)
    def _():
        bias = jnp.concatenate([tiled_bias(step * pp + i) for i in range(pp)], axis=1)
        s = jnp.concatenate([_dot(qbd_sc[...], r[...].reshape(d, PAGE).astype(BF16)) for r in kp_refs],
                            axis=1) + bias
        pv = lambda pb: sum(_dot_nt(pb[:, i * PAGE:(i + 1) * PAGE], vp_refs[i][...].reshape(d, PAGE).astype(BF16))
                            for i in range(pp))
        _online_update(s, bias == 0.0, pv, m_sc, l_sc, acc_sc)

    @pl.when(step == nsteps)
    def _():
        bias = tiled_bias(n_pages)
        s = _dot_nt(qbd_sc[...], _pad_rows(kn_ref[0], PAGE).astype(BF16)) + bias
        pv = lambda pb: _dot(pb, _pad_rows(vn_ref[0], PAGE).astype(BF16))
        _online_update(s, bias == 0.0, pv, m_sc, l_sc, acc_sc)
        lo_half = _lane_iota() < HEAD64
        for c in range(d // LANES):
            cols = slice(c * LANES, (c + 1) * LANES)
            r0 = slice(2 * c * tn, (2 * c + 1) * tn)
            r1 = slice((2 * c + 1) * tn, (2 * c + 2) * tn)
            o = jnp.where(lo_half, acc_sc[r0, cols] / l_sc[r0, :], acc_sc[r1, cols] / l_sc[r1, :])
            o_ref[0, :, cols] = o.astype(BF16)


def _dsa_sample(page_table, scores, scores_new, q3, kn3, vn3, k_pool_t, v_pool_t, layer, ksel):
    bd, tn, d = q3.shape
    n_pages = page_table.shape[1]
    groups = d // HEAD64
    rows = groups * tn
    pp = math.gcd(n_pages, SAMPLE_PP)
    nsteps = n_pages // pp
    seq = lambda: pl.BlockSpec((1, tn, d), lambda b, s, pt: (b, 0, 0))
    pool = lambda i: pl.BlockSpec(
        (None, None, groups, HEAD64, PAGE),
        lambda b, s, pt: (layer, pt[b, jnp.minimum(s, nsteps - 1) * pp + i], 0, 0, 0))
    kern = functools.partial(_dsa_sample_kernel, tn=tn, d=d, n_pages=n_pages, nsteps=nsteps, pp=pp, ksel=ksel)
    return pl.pallas_call(
        kern,
        grid_spec=pltpu.PrefetchScalarGridSpec(
            num_scalar_prefetch=1,
            grid=(bd, nsteps + 1),
            in_specs=[pl.BlockSpec((1, n_pages, tn, PAGE), lambda b, s, pt: (b, 0, 0, 0)),
                      pl.BlockSpec((1, tn, PAGE), lambda b, s, pt: (b, 0, 0)),
                      seq(), seq(), seq()] + [pool(i) for i in range(pp)] + [pool(i) for i in range(pp)],
            out_specs=seq(),
            scratch_shapes=[pltpu.VMEM((n_pages + 1, tn, PAGE), F32), pltpu.VMEM((rows, d), BF16),
                            pltpu.VMEM((rows, 1), F32), pltpu.VMEM((rows, 1), F32),
                            pltpu.VMEM((rows, d), F32)]),
        out_shape=jax.ShapeDtypeStruct((bd, tn, d), BF16),
        compiler_params=_cparams("parallel", "arbitrary"),
        name="dsa_sample",
    )(page_table, scores, scores_new, q3, kn3, vn3, *([k_pool_t] * pp), *([v_pool_t] * pp))


def _hg_in_kernel(x_ref, g_ref, w_ref, lbw_ref, q_ref, k_ref, lf_ref, v_ref, gs_ref, *, d, layer):
    h = _rms(x_ref[...], g_ref[...]).astype(BF16)
    y = _dot(h, w_ref[...])
    lbw = lbw_ref[...]
    e = jnp.exp(lbw - jnp.max(lbw, axis=0, keepdims=True))
    sm = e / jnp.sum(e, axis=0, keepdims=True)
    lb = jnp.sum(sm[1:layer + 1, :], axis=0, keepdims=True)
    q, fz, v, g = y[:, :d], y[:, d:2 * d], y[:, 2 * d:3 * d], y[:, 3 * d:]
    f = lb + (1.0 - lb) * _sigmoid(fz)
    q_ref[...] = q * _sigmoid(q)
    k_ref[...] = 1.0 - f
    lf_ref[...] = jnp.log(f)
    v_ref[...] = v
    gs_ref[...] = g * _sigmoid(g)


def _hg_in(x, g1, w, lbw, layer):
    r, d = x.shape
    tm = min(r, ROW_TILE)
    row = lambda: pl.BlockSpec((tm, d), lambda i: (i, 0))
    return pl.pallas_call(
        functools.partial(_hg_in_kernel, d=d, layer=layer),
        grid=(r // tm,),
        in_specs=[row(), _const_spec((1, d)), _const_spec((d, 4 * d)), _const_spec(lbw.shape)],
        out_specs=[row()] * 5,
        out_shape=[jax.ShapeDtypeStruct((r, d), F32)] * 5,
        compiler_params=_cparams("parallel"),
        name="hg_in",
    )(x, g1, w, lbw)


def _hg_rec_kernel(q_ref, k_ref, lf_ref, v_ref, gs_ref, s0_ref, gn_ref, o_ref, s_ref, st_sc, *, tb, nt):
    j = pl.program_id(2)

    @pl.when(j == 0)
    def _():
        st_sc[...] = s0_ref[...].T

    c = HG_C
    r_io = lax.broadcasted_iota(I32, (c, c), 0)
    c_io = lax.broadcasted_iota(I32, (c, c), 1)
    causal = r_io >= c_io
    tri = jnp.where(causal, 1.0, 0.0)
    row = lax.broadcasted_iota(I32, (c, 1), 0)
    nch = max(1, tb // c)
    for ci in range(nch):
        if tb >= c:
            rows = slice(ci * c, (ci + 1) * c)
            load = lambda ref: ref[rows, :]
        else:
            load = lambda ref: _pad_rows(ref[...], c)
        q, k, lf, v = load(q_ref), load(k_ref), load(lf_ref), load(v_ref)
        b = jnp.dot(tri, lf, preferred_element_type=F32, precision=lax.Precision.HIGHEST)
        st = st_sc[...]
        o = _dot_nt((q * jnp.exp(b)).astype(BF16), st.astype(BF16))
        slabs = []
        for blk in range(c // HG_SUB):
            lo, hi = blk * HG_SUB, (blk + 1) * HG_SUB
            anchor = b[lo - 1:lo, :] if blk > 0 else jnp.zeros((1, LANES), F32)
            qb = q[lo:hi, :] * jnp.exp(b[lo:hi, :] - anchor)
            kb = k * jnp.exp(jnp.where(row < hi, anchor - b, -jnp.inf))
            slabs.append(_dot_nt(qb.astype(BF16), kb.astype(BF16)))
        a = jnp.where(causal, jnp.concatenate(slabs, axis=0), 0.0)
        o = o + _dot(a.astype(BF16), v.astype(BF16))
        b_last = b[c - 1:c, :]
        kd = k * jnp.exp(b_last - b)
        st_sc[...] = st * jnp.exp(b_last) + _dot(v.T.astype(BF16), kd.astype(BF16))
        og = _rms(o, gn_ref[...])
        if tb >= c:
            o_ref[rows, :] = (og * gs_ref[rows, :]).astype(BF16)
        else:
            o_ref[...] = (og[:tb, :] * gs_ref[...]).astype(BF16)

    @pl.when(j == nt - 1)
    def _():
        s_ref[...] = st_sc[...].T


def _hg_rec(q, k, lf, v, gs, s0, gn, bsz, t):
    r, d = q.shape
    heads = d // LANES
    tb = min(t, HG_TB)
    nt = t // tb
    blk = lambda: pl.BlockSpec((tb, LANES), lambda b, h, j: (b * nt + j, h))
    state = lambda: pl.BlockSpec((None, None, LANES, LANES), lambda b, h, j: (b, h, 0, 0))
    return pl.pallas_call(
        functools.partial(_hg_rec_kernel, tb=tb, nt=nt),
        grid=(bsz, heads, nt),
        in_specs=[blk(), blk(), blk(), blk(), blk(), state(), _const_spec((1, LANES))],
        out_specs=[blk(), state()],
        out_shape=[jax.ShapeDtypeStruct((r, d), BF16),
                   jax.ShapeDtypeStruct((bsz, heads, LANES, LANES), F32)],
        scratch_shapes=[pltpu.VMEM((LANES, LANES), F32)],
        compiler_params=_cparams("parallel", "parallel", "arbitrary"),
        name="hg_rec",
    )(q, k, lf, v, gs, s0, gn)


def kernel(x_prompt, x_sample, state_conv, cache_da_k, cache_da_v, cache_dsa_k, cache_dsa_v, cache_dsa_idx_k, state_hgrn, page_table, norm1_g, norm2_g, final_g, mlp_w1, mlp_w2, cv_w1, cv_b1, cv_dw, cv_dwb, cv_ln_g, cv_ln_b, cv_w2, cv_b2, da_w_in, da_lq1, da_lk1, da_lq2, da_lk2, da_subln_g, da_wo, dsa_w_in, dsa_wo, hg_w_in, hg_lb, hg_norm_g, hg_wo):
    bp, t, d = x_prompt.shape
    bd, tn, _ = x_sample.shape
    n_pages = page_table.shape[1]
    past_len = n_pages * PAGE
    depth = norm1_g.shape[0]
    assert d % LANES == 0 and t % ROW_TILE == 0 and tn & (tn - 1) == 0 and tn <= SUBLANES

    xp = x_prompt.reshape(bp * t, d)
    xs = x_sample.reshape(bd * tn, d)
    cos_p, sin_p = _rope_tables(jnp.arange(t))
    cos_s, sin_s = _rope_tables(past_len + jnp.arange(tn))
    cos_s, sin_s = jnp.tile(cos_s, (bd, 1)), jnp.tile(sin_s, (bd, 1))
    ksel_p = min(DSA_TOPK, t // 4)
    ksel_s = min(DSA_TOPK, (past_len + tn) // 4)
    zero_bias = jnp.zeros((1, d), F32)
    row1 = lambda a: a.reshape(1, -1)
    new = {n: [] for n in ('conv_p', 'conv_s', 'dak_p', 'dav_p', 'dak_s', 'dav_s', 'dsak_p', 'dsav_p', 'dsai_p',
                           'dsak_s', 'dsav_s', 'dsai_s', 'hg_p', 'hg_s')}

    for layer in range(depth):
        kind, j = layer % 4, layer // 4
        g1 = row1(norm1_g[layer])
        bo = zero_bias
        if kind == 0:
            pad = lambda a, n: jnp.concatenate([jnp.zeros(a.shape[:1] + (n,) + a.shape[2:], a.dtype), a], axis=1)
            dw = jnp.concatenate([cv_dw[j], jnp.zeros((CONV_HALO - CONV_W, d), F32)], axis=0)[:, None, :]
            cw = (g1, cv_w1[j].astype(BF16), row1(cv_b1[j]), dw, row1(cv_dwb[j]), row1(cv_ln_g[j]),
                  row1(cv_ln_b[j]))
            halo_pad = CONV_HALO - (CONV_W - 1)
            op, tail_p = _conv_mixer(xp.reshape(bp, t, d), jnp.zeros((bp, CONV_HALO, d), F32), *cw,
                                     nb=1, tt=ROW_TILE)
            os_, tail_s = _conv_mixer(xs.reshape(bd, tn, d), pad(state_conv[j], halo_pad), *cw, nb=bd, tt=tn)
            op, os_ = op.reshape(bp * t, d), os_.reshape(bd * tn, d)
            new['conv_p'].append(tail_p[:, halo_pad:])
            new['conv_s'].append(tail_s[:, halo_pad:])
            wo, bo = cv_w2[j].astype(BF16), row1(cv_b2[j])
        elif kind == 1:
            lam_init = 0.8 - 0.6 * math.exp(-0.3 * layer)
            lams = (row1(da_lq1[j]), row1(da_lk1[j]), row1(da_lq2[j]), row1(da_lk2[j]))
            subg = row1(da_subln_g[j])
            w_in = da_w_in[j].astype(BF16)
            qp, kp, vp, kbp, vbp = _da_in(xp, g1, w_in, cos_p, sin_p)
            qs, ks, vs, _, _ = _da_in(xs, g1, w_in, cos_s, sin_s)
            op = _da_prompt(qp, kbp, vbp, lams, subg, bp, t, lam_init)
            os_ = _da_sample(page_table, qs.reshape(bd, tn, d), ks.reshape(bd, tn, d), vs.reshape(bd, tn, d),
                             cache_da_k, cache_da_v, lams, subg, j, lam_init).reshape(bd * tn, d)
            hd = (d // LANES, LANES)
            new['dak_p'].append(kp.reshape((bp, t) + hd))
            new['dav_p'].append(vp.reshape((bp, t) + hd))
            new['dak_s'].append(ks.reshape((bd, tn) + hd))
            new['dav_s'].append(vs.reshape((bd, tn) + hd))
            wo = da_wo[j].astype(BF16)
        elif kind == 2:
            w_in = dsa_w_in[j]
            w_in = jnp.concatenate([w_in, jnp.zeros((d, -w_in.shape[1] % LANES), F32)], axis=1).astype(BF16)
            qp, kp, vp, kbp, vbp, qip, mp_ = _dsa_in(xp, g1, w_in, cos_p, sin_p)
            qs, ks, vs, _, _, qis, ms_ = _dsa_in(xs, g1, w_in, cos_s, sin_s)
            op = _dsa_prompt(qip, mp_, qp, kbp, vbp, bp, t, ksel_p)
            scores, scores_new = _dsa_sample_scores(
                page_table, qis.reshape(bd, tn, -1), ms_.reshape(bd, tn, LANES),
                jnp.transpose(cache_dsa_idx_k, (0, 1, 3, 2)), j)
            os_ = _dsa_sample(page_table, scores, scores_new, qs.reshape(bd, tn, d), ks.reshape(bd, tn, d),
                              vs.reshape(bd, tn, d), jnp.transpose(cache_dsa_k, (0, 1, 3, 4, 2)),
                              jnp.transpose(cache_dsa_v, (0, 1, 3, 4, 2)), j, ksel_s).reshape(bd * tn, d)
            hd = (d // HEAD64, HEAD64)
            new['dsak_p'].append(kp.reshape((bp, t) + hd))
            new['dsav_p'].append(vp.reshape((bp, t) + hd))
            new['dsai_p'].append(mp_[:, :HEAD64].reshape(bp, t, HEAD64))
            new['dsak_s'].append(ks.reshape((bd, tn) + hd))
            new['dsav_s'].append(vs.reshape((bd, tn) + hd))
            new['dsai_s'].append(ms_[:, :HEAD64].reshape(bd, tn, HEAD64))
            wo = dsa_wo[j].astype(BF16)
        else:
            w_in = hg_w_in[j].astype(BF16)
            gn = row1(hg_norm_g[j])
            heads = d // LANES
            hp = _hg_in(xp, g1, w_in, hg_lb, layer)
            hs = _hg_in(xs, g1, w_in, hg_lb, layer)
            op, sp = _hg_rec(*hp, jnp.zeros((bp, heads, LANES, LANES), F32), gn, bp, t)
            os_, ss = _hg_rec(*hs, state_hgrn[j], gn, bd, tn)
            new['hg_p'].append(sp.astype(state_hgrn.dtype))
            new['hg_s'].append(ss.astype(state_hgrn.dtype))
            wo = hg_wo[j].astype(BF16)

        final = layer == depth - 1
        post = (wo, bo, row1(norm2_g[layer]), mlp_w1[layer].astype(BF16), mlp_w2[layer].astype(BF16),
                row1(final_g))
        xp = _post(xp, op, *post, final=final)
        xs = _post(xs, os_, *post, final=final)

    return (xp.reshape(bp, t, d), xs.reshape(bd, tn, d),
            jnp.stack(new['conv_p']), jnp.stack(new['conv_s']),
            jnp.stack(new['dak_p']), jnp.stack(new['dav_p']), jnp.stack(new['dak_s']), jnp.stack(new['dav_s']),
            jnp.stack(new['dsak_p']), jnp.stack(new['dsav_p']), jnp.stack(new['dsai_p']),
            jnp.stack(new['dsak_s']), jnp.stack(new['dsav_s']), jnp.stack(new['dsai_s']),
            jnp.stack(new['hg_p']), jnp.stack(new['hg_s']))
```

```python
import functools
import math

import jax
import jax.numpy as jnp
from jax import lax
from jax.experimental import pallas as pl
from jax.experimental.pallas import tpu as pltpu

F32 = jnp.float32
BF16 = jnp.bfloat16
I32 = jnp.int32

NORM_EPS = 1e-6
LN_EPS = 1e-5
ROPE_THETA = 10000.0
ROPE_DIM = 64
CONV_W = 31
PAGE = 128
HEAD64 = 64
IDX_HEADS = 8
DSA_TOPK = 256

LANES = 128
SUBLANES = 8
CONV_HALO = 32
CONV_ROWS = 32
ROW_TILE = 256
ATT_TQ = 256
DSA_TQ = 256
DSA_KV_GROUPS = 4
SAMPLE_PP = 4
SCORE_PP = 8
HG_C = 128
HG_SUB = 16
HG_TB = 512
HG_HP = 4
SELECT_NB = 8
MLP_FCHUNK = 1024
VMEM_LIMIT = 56 * 1024 * 1024
NEG_BIG = -1e30
INT_MIN = -2 ** 31


def _cparams(*sem):
    return pltpu.CompilerParams(dimension_semantics=sem, vmem_limit_bytes=VMEM_LIMIT)


def _const_spec(shape):
    nd = len(shape)
    return pl.BlockSpec(shape, lambda *_: (0,) * nd, pipeline_mode=pl.Buffered(1))


def _dot(a, b):
    return jnp.dot(a, b, preferred_element_type=F32)


def _dot_nt(a, b):
    return lax.dot_general(a, b, (((1,), (1,)), ((), ())), preferred_element_type=F32)


def _rms(x, g):
    return x * lax.rsqrt(jnp.mean(x * x, axis=-1, keepdims=True) + NORM_EPS) * g


def _sigmoid(x):
    return 1.0 / (1.0 + jnp.exp(-x))


def _lane_iota(shape=(1, LANES)):
    return lax.broadcasted_iota(I32, shape, len(shape) - 1)


def _rope(xc, cos, sin_signed):
    first_half = (_lane_iota() & 32) == 0
    partner = jnp.where(first_half, pltpu.roll(xc, 96, 1), pltpu.roll(xc, 32, 1))
    return xc * cos + partner * sin_signed


def _rope_tables(pos):
    inv = ROPE_THETA ** (-jnp.arange(0, ROPE_DIM, 2, dtype=F32) / ROPE_DIM)
    ang = pos.astype(F32)[:, None] * inv[None, :]
    cos, sin = jnp.cos(ang), jnp.sin(ang)
    return jnp.tile(cos, (1, 4)), jnp.tile(jnp.concatenate([-sin, sin], axis=1), (1, 2))


def _pad_rows(x, rows):
    return jnp.concatenate([x, jnp.zeros((rows - x.shape[0], x.shape[1]), x.dtype)], axis=0)


def _post_kernel(x_ref, o_ref, wo_ref, bo_ref, g_ref, w1_ref, w2_ref, fg_ref, y_ref, h_ref, *, final):
    x1 = x_ref[...] + _dot(o_ref[...], wo_ref[...]) + bo_ref[...]
    h = _rms(x1, g_ref[...]).astype(BF16)
    f = w1_ref.shape[1]
    fc = min(f, MLP_FCHUNK)
    for c in range(f // fc):
        a = jnp.maximum(_dot(h, w1_ref[:, c * fc:(c + 1) * fc]), 0.0)
        h_ref[:, c * fc:(c + 1) * fc] = (a * a).astype(BF16)
    y = x1 + _dot(h_ref[...], w2_ref[...])
    if final:
        y = _rms(y, fg_ref[...])
    y_ref[...] = y


def _post(x, o, wo, bo, g2, w1, w2, fg, final):
    r, d = x.shape
    f = w1.shape[1]
    tm = min(r, ROW_TILE)
    row = lambda w: pl.BlockSpec((tm, w), lambda i: (i, 0))
    return pl.pallas_call(
        functools.partial(_post_kernel, final=final),
        grid=(r // tm,),
        in_specs=[row(d), row(d), _const_spec((d, d)), _const_spec((1, d)), _const_spec((1, d)),
                  _const_spec((d, f)), _const_spec((f, d)), _const_spec((1, d))],
        out_specs=row(d),
        out_shape=jax.ShapeDtypeStruct((r, d), F32),
        scratch_shapes=[pltpu.VMEM((tm, f), BF16)],
        compiler_params=_cparams("parallel"),
        name="post_mlp",
    )(x, o, wo, bo, g2, w1, w2, fg)


def _conv_kernel(x_ref, buf_ref, g_ref, w1_ref, b1_ref, dw_ref, dwb_ref, lng_ref, lnb_ref,
                 o_ref, tail_ref, ext_ref, y_ref, sh_ref, *, nb, tt, d, carry, preshift):
    @pl.when(pl.program_id(1) == 0)
    def _():
        ext_ref[:, 0:CONV_HALO, :] = buf_ref[...]

    x = x_ref[...].reshape(nb * tt, d)
    h = _rms(x, g_ref[...]).astype(BF16)
    ag = _dot(h, w1_ref[...]) + b1_ref[...]
    u = ag[:, :d] * _sigmoid(ag[:, d:])
    ext_ref[:, CONV_HALO:CONV_HALO + tt, :] = u.reshape(nb, tt, d)

    if preshift:
        for a in range(1, SUBLANES):
            sh_ref[a - 1] = ext_ref[0, a:a + sh_ref.shape[1], :]

    rs = min(tt, CONV_ROWS)
    for r in range(tt // rs):
        acc = jnp.zeros((nb, rs, d), F32)
        for k in range(CONV_W):
            off = k + CONV_HALO - (CONV_W - 1)
            a = off % SUBLANES
            if preshift and a:
                start = r * rs + off - a
                window = sh_ref[a - 1, start:start + rs, :][None]
            else:
                window = ext_ref[:, r * rs + off:r * rs + off + rs, :]
            acc = acc + dw_ref[k] * window
        y_ref[:, r * rs:(r + 1) * rs, :] = acc + dwb_ref[...]

    y = y_ref[...]
    mu = jnp.mean(y, axis=-1, keepdims=True)
    yc = y - mu
    var = jnp.mean(yc * yc, axis=-1, keepdims=True)
    z = yc * lax.rsqrt(var + LN_EPS) * lng_ref[...] + lnb_ref[...]
    o_ref[...] = (z * _sigmoid(z)).astype(BF16)
    tail = ext_ref[:, tt:tt + CONV_HALO, :]
    tail_ref[...] = tail
    if carry:
        ext_ref[:, 0:CONV_HALO, :] = tail


def _conv_mixer(x3, buf, g1, w1, b1, dw, dwb, lng, lnb, nb, tt):
    b, t, d = x3.shape
    nt = t // tt
    preshift = nb == 1 and tt > CONV_ROWS
    sh_shape = (SUBLANES - 1, tt + CONV_HALO - SUBLANES, d) if preshift else (1, SUBLANES, LANES)
    kern = functools.partial(_conv_kernel, nb=nb, tt=tt, d=d, carry=nt > 1, preshift=preshift)
    return pl.pallas_call(
        kern,
        grid=(b // nb, nt),
        in_specs=[pl.BlockSpec((nb, tt, d), lambda i, j: (i, j, 0)),
                  pl.BlockSpec((nb, CONV_HALO, d), lambda i, j: (i, 0, 0)),
                  _const_spec((1, d)), _const_spec((d, 2 * d)), _const_spec((1, 2 * d)),
                  _const_spec((CONV_HALO, 1, d)), _const_spec((1, d)), _const_spec((1, d)),
                  _const_spec((1, d))],
        out_specs=[pl.BlockSpec((nb, tt, d), lambda i, j: (i, j, 0)),
                   pl.BlockSpec((nb, CONV_HALO, d), lambda i, j: (i, 0, 0))],
        out_shape=[jax.ShapeDtypeStruct((b, t, d), BF16),
                   jax.ShapeDtypeStruct((b, CONV_HALO, d), F32)],
        scratch_shapes=[pltpu.VMEM((nb, CONV_HALO + tt, d), F32), pltpu.VMEM((nb, tt, d), F32),
                        pltpu.VMEM(sh_shape, F32)],
        compiler_params=_cparams("parallel", "arbitrary"),
        name="conv_mixer",
    )(x3, buf, g1, w1, b1, dw, dwb, lng, lnb)


def _da_in_kernel(x_ref, g_ref, w_ref, cos_ref, sin_ref, q_ref, k_ref, v_ref, kb_ref, vb_ref, *, d):
    h = _rms(x_ref[...], g_ref[...]).astype(BF16)
    y = _dot(h, w_ref[...])
    cos, sin = cos_ref[...], sin_ref[...]
    for c in range(d // LANES):
        sl = slice(c * LANES, (c + 1) * LANES)
        q = _rope(y[:, c * LANES:(c + 1) * LANES], cos, sin)
        q_ref[:, sl] = (q * HEAD64 ** -0.5).astype(BF16)
        k = _rope(y[:, d + c * LANES:d + (c + 1) * LANES], cos, sin)
        k_ref[:, sl] = k
        kb_ref[:, sl] = k.astype(BF16)
        v = y[:, 2 * d + c * LANES:2 * d + (c + 1) * LANES]
        v_ref[:, sl] = v
        vb_ref[:, sl] = v.astype(BF16)


def _da_in(x, g1, w, cos, sin):
    r, d = x.shape
    tm = min(r, ROW_TILE)
    ntab = cos.shape[0] // tm
    row = lambda: pl.BlockSpec((tm, d), lambda i: (i, 0))
    tab = lambda: pl.BlockSpec((tm, LANES), lambda i: (i % ntab, 0))
    return pl.pallas_call(
        functools.partial(_da_in_kernel, d=d),
        grid=(r // tm,),
        in_specs=[row(), _const_spec((1, d)), _const_spec((d, 3 * d)), tab(), tab()],
        out_specs=[row(), row(), row(), row(), row()],
        out_shape=[jax.ShapeDtypeStruct((r, d), BF16), jax.ShapeDtypeStruct((r, d), F32),
                   jax.ShapeDtypeStruct((r, d), F32), jax.ShapeDtypeStruct((r, d), BF16),
                   jax.ShapeDtypeStruct((r, d), BF16)],
        compiler_params=_cparams("parallel"),
        name="da_in",
    )(x, g1, w, cos, sin)


def _da_lambda(lq1, lk1, lq2, lk2, lam_init):
    return (jnp.exp(jnp.sum(lq1[...] * lk1[...], axis=-1, keepdims=True))
            - jnp.exp(jnp.sum(lq2[...] * lk2[...], axis=-1, keepdims=True)) + lam_init)


def _da_prompt_kernel(lq1, lk1, lq2, lk2, subg_ref, q_ref, k_ref, v_ref, o_ref, *, tq, nq, lam_init):
    i = pl.program_id(2)
    lam = _da_lambda(lq1, lk1, lq2, lk2, lam_init)
    q = q_ref[...]
    lane = _lane_iota()
    zero = jnp.zeros_like(q)
    q1 = jnp.where(lane < HEAD64, q, zero)
    q2 = jnp.where(lane >= HEAD64, q, zero)
    keep = (lax.broadcasted_iota(I32, (tq, tq), 0) >= lax.broadcasted_iota(I32, (tq, tq), 1))

    def body(c):
        off = c * tq

        def softmax_parts(qm):
            sd = jnp.where(keep, _dot_nt(qm, k_ref[off:off + tq, :]), NEG_BIG)
            m = jnp.max(sd, axis=-1, keepdims=True)
            so = None
            if c > 0:
                so = _dot_nt(qm, k_ref[0:off, :])
                m = jnp.maximum(m, jnp.max(so, axis=-1, keepdims=True))
            pd = jnp.exp(sd - m)
            l = jnp.sum(pd, axis=-1, keepdims=True)
            po = None
            if c > 0:
                po = jnp.exp(so - m)
                l = l + jnp.sum(po, axis=-1, keepdims=True)
            return pd, po, l

        pd1, po1, l1 = softmax_parts(q1)
        pd2, po2, l2 = softmax_parts(q2)
        w1 = 1.0 / l1
        w2 = lam / l2
        o = _dot((pd1 * w1 - pd2 * w2).astype(BF16), v_ref[off:off + tq, :])
        if c > 0:
            o = o + _dot((po1 * w1 - po2 * w2).astype(BF16), v_ref[0:off, :])
        o_ref[...] = (_rms(o, subg_ref[...]) * (1.0 - lam_init)).astype(BF16)

    for c in range(nq):
        pl.when(i == c)(functools.partial(body, c))


def _da_prompt(q, kb, vb, lams, subg, bsz, t, lam_init):
    r, d = q.shape
    tq = min(t, ATT_TQ)
    nq = t // tq
    lam_spec = _const_spec((1, HEAD64))
    return pl.pallas_call(
        functools.partial(_da_prompt_kernel, tq=tq, nq=nq, lam_init=lam_init),
        grid=(bsz, d // LANES, nq),
        in_specs=[lam_spec, lam_spec, lam_spec, lam_spec, _const_spec((1, LANES)),
                  pl.BlockSpec((tq, LANES), lambda b, h, i: (b * nq + i, h)),
                  pl.BlockSpec((t, LANES), lambda b, h, i: (b, h)),
                  pl.BlockSpec((t, LANES), lambda b, h, i: (b, h))],
        out_specs=pl.BlockSpec((tq, LANES), lambda b, h, i: (b * nq + i, h)),
        out_shape=jax.ShapeDtypeStruct((r, d), BF16),
        compiler_params=_cparams("parallel", "parallel", "arbitrary"),
        name="da_prompt",
    )(*lams, subg, q, kb, vb)


def _online_update(s, valid, pv_fn, m_sc, l_sc, acc_sc):
    m_old = m_sc[...]
    m_new = jnp.maximum(m_old, jnp.max(s, axis=-1, keepdims=True))
    alpha = jnp.exp(m_old - m_new)
    p = jnp.exp(s - m_new)
    if valid is not None:
        p = jnp.where(valid, p, 0.0)
    l_sc[...] = alpha * l_sc[...] + jnp.sum(p, axis=-1, keepdims=True)
    acc_sc[...] = alpha * acc_sc[...] + pv_fn(p.astype(BF16))
    m_sc[...] = m_new


def _da_sample_kernel(pt_ref, lq1, lk1, lq2, lk2, subg_ref, q_ref, kn_ref, vn_ref, *rest,
                      tn, d, lam_init, nsteps, pp):
    del pt_ref
    kp_refs, vp_refs = rest[:pp], rest[pp:2 * pp]
    o_ref, qh_sc, own_sc, m_sc, l_sc, acc_sc = rest[2 * pp:]
    step = pl.program_id(1)
    heads = d // LANES
    gr = 2 * tn
    rows = heads * gr
    ncol = pp * PAGE * heads

    @pl.when(step == 0)
    def _():
        q = q_ref[0].astype(F32)
        first_map = lax.broadcasted_iota(I32, (gr, LANES), 0) < tn
        lo_half = _lane_iota((gr, LANES)) < HEAD64
        for h in range(heads):
            qc = q[:, h * LANES:(h + 1) * LANES]
            qh_sc[h * gr:(h + 1) * gr, :] = jnp.where(
                first_map == lo_half, jnp.concatenate([qc, qc], axis=0), 0.0).astype(BF16)
        row_head = lax.broadcasted_iota(I32, (rows, ncol), 0) >> int(math.log2(gr))
        col_head = lax.broadcasted_iota(I32, (rows, ncol), 1) & (heads - 1)
        own_sc[...] = jnp.where(row_head == col_head, 0.0, NEG_BIG)
        m_sc[...] = jnp.full((rows, 1), NEG_BIG, F32)
        l_sc[...] = jnp.zeros((rows, 1), F32)
        acc_sc[...] = jnp.zeros((rows, LANES), F32)

    flat = lambda refs: jnp.concatenate([r[...] for r in refs], axis=0).astype(BF16)
    s = _dot_nt(qh_sc[...], flat(kp_refs)) + own_sc[...]
    _online_update(s, None, lambda pb: _dot(pb, flat(vp_refs)), m_sc, l_sc, acc_sc)

    @pl.when(step == nsteps - 1)
    def _():
        fresh = lambda ref, h: _pad_rows(ref[0][:, h * LANES:(h + 1) * LANES], PAGE).astype(BF16)
        qpos = lax.broadcasted_iota(I32, (rows, PAGE), 0) & (tn - 1)
        valid = lax.broadcasted_iota(I32, (rows, PAGE), 1) <= qpos
        s_new = jnp.concatenate(
            [_dot_nt(qh_sc[h * gr:(h + 1) * gr, :], fresh(kn_ref, h)) for h in range(heads)], axis=0)
        pv = lambda pb: jnp.concatenate(
            [_dot(pb[h * gr:(h + 1) * gr, :], fresh(vn_ref, h)) for h in range(heads)], axis=0)
        _online_update(jnp.where(valid, s_new, NEG_BIG), valid, pv, m_sc, l_sc, acc_sc)
        lam = _da_lambda(lq1, lk1, lq2, lk2, lam_init)
        for h in range(heads):
            r1 = slice(h * gr, h * gr + tn)
            r2 = slice(h * gr + tn, (h + 1) * gr)
            o = acc_sc[r1, :] / l_sc[r1, :] - lam * (acc_sc[r2, :] / l_sc[r2, :])
            o_ref[0, :, h * LANES:(h + 1) * LANES] = (_rms(o, subg_ref[...]) * (1.0 - lam_init)).astype(BF16)


def _da_sample(page_table, q3, kn3, vn3, k_pool, v_pool, lams, subg, layer, lam_init):
    bd, tn, d = q3.shape
    n_pages = page_table.shape[1]
    heads = d // LANES
    pp = math.gcd(n_pages, SAMPLE_PP)
    nsteps = n_pages // pp
    rows = heads * 2 * tn
    assert heads & (heads - 1) == 0
    flat_shape = k_pool.shape[:2] + (PAGE * heads, LANES)
    k_pool, v_pool = k_pool.reshape(flat_shape), v_pool.reshape(flat_shape)
    seq = lambda: pl.BlockSpec((1, tn, d), lambda b, s, pt: (b, 0, 0))
    pool = lambda i: pl.BlockSpec((None, None, PAGE * heads, LANES),
                                  lambda b, s, pt: (layer, pt[b, s * pp + i], 0, 0))
    lam_spec = _const_spec((1, HEAD64))
    kern = functools.partial(_da_sample_kernel, tn=tn, d=d, lam_init=lam_init, nsteps=nsteps, pp=pp)
    return pl.pallas_call(
        kern,
        grid_spec=pltpu.PrefetchScalarGridSpec(
            num_scalar_prefetch=1,
            grid=(bd, nsteps),
            in_specs=[lam_spec, lam_spec, lam_spec, lam_spec, _const_spec((1, LANES)), seq(), seq(), seq()]
            + [pool(i) for i in range(pp)] + [pool(i) for i in range(pp)],
            out_specs=seq(),
            scratch_shapes=[pltpu.VMEM((rows, LANES), BF16), pltpu.VMEM((rows, pp * PAGE * heads), F32),
                            pltpu.VMEM((rows, 1), F32), pltpu.VMEM((rows, 1), F32),
                            pltpu.VMEM((rows, LANES), F32)]),
        out_shape=jax.ShapeDtypeStruct((bd, tn, d), BF16),
        compiler_params=_cparams("parallel", "arbitrary"),
        name="da_sample",
    )(page_table, *lams, subg, q3, kn3, vn3, *([k_pool] * pp), *([v_pool] * pp))


def _dsa_in_kernel(x_ref, g_ref, w_ref, cos_ref, sin_ref, q_ref, k_ref, v_ref, kb_ref, vb_ref,
                   qi_ref, misc_ref, *, d, token_minor):
    h = _rms(x_ref[...], g_ref[...]).astype(BF16)
    y = _dot(h, w_ref[...])
    cos, sin = cos_ref[...], sin_ref[...]
    for c in range(d // LANES):
        sl = slice(c * LANES, (c + 1) * LANES)
        q = _rope(y[:, c * LANES:(c + 1) * LANES], cos, sin)
        q_ref[:, sl] = (q * HEAD64 ** -0.5).astype(BF16)
        k = _rope(y[:, d + c * LANES:d + (c + 1) * LANES], cos, sin)
        kb_ref[:, sl] = k.astype(BF16)
        v = y[:, 2 * d + c * LANES:2 * d + (c + 1) * LANES]
        vb_ref[:, sl] = v.astype(BF16)
        if token_minor:
            k_ref[sl, :] = k.T
            v_ref[sl, :] = v.T
        else:
            k_ref[:, sl] = k
            v_ref[:, sl] = v
    for c in range(IDX_HEADS * HEAD64 // LANES):
        qi = _rope(y[:, 3 * d + c * LANES:3 * d + (c + 1) * LANES], cos, sin)
        qi_ref[:, c * LANES:(c + 1) * LANES] = (qi * HEAD64 ** -0.5).astype(BF16)
    is_key = _lane_iota() < HEAD64
    base = 3 * d + IDX_HEADS * HEAD64
    misc_ref[...] = _rope(y[:, base:base + LANES], jnp.where(is_key, cos, 1.0), jnp.where(is_key, sin, 0.0))


def _dsa_in(x, g1, w, cos, sin, seq_len=None):
    r, d = x.shape
    tm = min(r, ROW_TILE)
    ntab = cos.shape[0] // tm
    nqi = IDX_HEADS * HEAD64
    row = lambda w_: pl.BlockSpec((tm, w_), lambda i: (i, 0))
    tab = lambda: pl.BlockSpec((tm, LANES), lambda i: (i % ntab, 0))
    if seq_len is None:
        kv_spec, kv_shape = row(d), jax.ShapeDtypeStruct((r, d), F32)
    else:
        nt = seq_len // tm
        kv_spec = pl.BlockSpec((None, d, tm), lambda i: (i // nt, 0, i % nt))
        kv_shape = jax.ShapeDtypeStruct((r // seq_len, d, seq_len), F32)
    return pl.pallas_call(
        functools.partial(_dsa_in_kernel, d=d, token_minor=seq_len is not None),
        grid=(r // tm,),
        in_specs=[row(d), _const_spec((1, d)), _const_spec(w.shape), tab(), tab()],
        out_specs=[row(d), kv_spec, kv_spec, row(d), row(d), row(nqi), row(LANES)],
        out_shape=[jax.ShapeDtypeStruct((r, d), BF16), kv_shape, kv_shape,
                   jax.ShapeDtypeStruct((r, d), BF16),
                   jax.ShapeDtypeStruct((r, d), BF16), jax.ShapeDtypeStruct((r, nqi), BF16),
                   jax.ShapeDtypeStruct((r, LANES), F32)],
        compiler_params=_cparams("parallel"),
        name="dsa_in",
    )(x, g1, w, cos, sin)


def _sortable_key(score):
    bits = lax.bitcast_convert_type(score + 0.0, I32)
    return bits ^ ((bits >> 31) & 0x7FFFFFFF)


def _select_topk(key, pos, krow, count, pos_bits):
    def ge_step(it, ans):
        cand = ans | lax.shift_left(jnp.int32(1), 30 - it)
        return jnp.where(count(key >= cand) >= krow, cand, ans)

    ans = jnp.where(count(key >= 0) >= krow, jnp.int32(0), jnp.int32(INT_MIN))
    ans = lax.fori_loop(0, 31, ge_step, ans)
    gt = key > ans
    eq = key == ans
    need = krow - count(gt)

    def pos_step(it, cut):
        cand = cut | lax.shift_left(jnp.int32(1), pos_bits - 1 - it)
        return jnp.where(count(eq & (pos < cand)) < need, cand, cut)

    surplus = jnp.max(count(eq) - need) > 0.0
    cut = lax.cond(surplus,
                   lambda: lax.fori_loop(0, pos_bits, pos_step, jnp.zeros(krow.shape, I32)),
                   lambda: jnp.full(krow.shape, 2 ** pos_bits, I32))
    return gt | (eq & (pos <= cut))


def _dsa_prompt_kernel(qi_ref, wq_ref, kim_ref, q_ref, k_ref, v_ref, o_ref, kid_sc, *, tq, nq, d, ksel):
    i = pl.program_id(1)
    lane = _lane_iota()
    lo_half = lane < HEAD64

    @pl.when(i == 0)
    def _():
        lo = jnp.where(lo_half, kim_ref[...], 0.0)
        kid_sc[...] = (lo + pltpu.roll(lo, HEAD64, 1)).astype(BF16)

    def body(kv):
        wt = wq_ref[...] * IDX_HEADS ** -0.5
        kid = kid_sc[0:kv, :]
        score = jnp.zeros((tq, kv), F32)
        for h in range(IDX_HEADS):
            qc = qi_ref[:, (h // 2) * LANES:(h // 2 + 1) * LANES]
            qp = jnp.where(lo_half if h % 2 == 0 else ~lo_half, qc, jnp.zeros_like(qc))
            dots = _dot_nt(qp, kid)
            score = score + wt[:, HEAD64 + h:HEAD64 + h + 1] * jnp.maximum(dots, 0.0)

        qpos = i * tq + lax.broadcasted_iota(I32, (tq, 1), 0)
        kpos = lax.broadcasted_iota(I32, (tq, kv), 1)
        key = jnp.where(kpos <= qpos, _sortable_key(score), INT_MIN)
        krow = jnp.minimum(qpos + 1, ksel).astype(F32)
        count = lambda mask: jnp.sum(jnp.where(mask, 1.0, 0.0), axis=-1, keepdims=True)
        sel = _select_topk(key, kpos, krow, count, max(1, (kv - 1).bit_length()))
        bias = jnp.where(sel, 0.0, NEG_BIG)

        for c in range(d // LANES):
            cols = slice(c * LANES, (c + 1) * LANES)
            qc = q_ref[:, cols]
            kc = k_ref[0:kv, cols]
            vc = v_ref[0:kv, cols]
            halves = []
            for half in range(2):
                qp = jnp.where(lo_half if half == 0 else ~lo_half, qc, jnp.zeros_like(qc))
                s = _dot_nt(qp, kc) + bias
                p = jnp.exp(s - jnp.max(s, axis=-1, keepdims=True))
                l = jnp.sum(p, axis=-1, keepdims=True)
                halves.append(_dot(p.astype(BF16), vc) / l)
            o_ref[:, cols] = jnp.where(lo_half, halves[0], halves[1]).astype(BF16)

    groups = math.gcd(nq, DSA_KV_GROUPS)
    per = nq // groups
    for g in range(groups):
        pl.when((i >= g * per) & (i < (g + 1) * per))(functools.partial(body, (g + 1) * per * tq))


def _dsa_prompt(qi, misc, q, kb, vb, bsz, t, ksel):
    r, d = q.shape
    tq = min(t, DSA_TQ)
    nq = t // tq
    nqi = qi.shape[1]
    qrow = lambda w: pl.BlockSpec((tq, w), lambda b, i: (b * nq + i, 0))
    seq = lambda w: pl.BlockSpec((t, w), lambda b, i: (b, 0))
    return pl.pallas_call(
        functools.partial(_dsa_prompt_kernel, tq=tq, nq=nq, d=d, ksel=ksel),
        grid=(bsz, nq),
        in_specs=[qrow(nqi), qrow(LANES), seq(LANES), qrow(d), seq(d), seq(d)],
        out_specs=qrow(d),
        out_shape=jax.ShapeDtypeStruct((r, d), BF16),
        scratch_shapes=[pltpu.VMEM((t, LANES), BF16)],
        compiler_params=_cparams("parallel", "arbitrary"),
        name="dsa_prompt",
    )(qi, misc, misc, q, kb, vb)


def _dsa_sample_scores_kernel(pt_ref, qi_ref, misc_ref, *rest, tn, nsteps, pp):
    del pt_ref
    pool_refs = rest[:pp]
    sc_ref, scn_ref, qst_sc, w_sc = rest[pp:]
    step = pl.program_id(1)

    @pl.when(step == 0)
    def _():
        qi = qi_ref[0].astype(F32)
        misc = misc_ref[0]
        qst_sc[...] = jnp.concatenate(
            [qi[:, h * HEAD64:(h + 1) * HEAD64] for h in range(IDX_HEADS)], axis=0).astype(BF16)
        w_sc[...] = jnp.concatenate(
            [misc[:, HEAD64 + h:HEAD64 + h + 1] for h in range(IDX_HEADS)], axis=0) * IDX_HEADS ** -0.5

    def head_sum(dots):
        sc = w_sc[...] * jnp.maximum(dots, 0.0)
        out = sc[0:tn, :]
        for h in range(1, IDX_HEADS):
            out = out + sc[h * tn:(h + 1) * tn, :]
        return out

    for i in range(pp):
        sc_ref[0, i] = head_sum(_dot(qst_sc[...], pool_refs[i][...].astype(BF16)))

    @pl.when(step == nsteps - 1)
    def _():
        fresh = _pad_rows(misc_ref[0][:, :HEAD64], PAGE).astype(BF16)
        scn_ref[0] = head_sum(_dot_nt(qst_sc[...], fresh))


def _dsa_sample_scores(page_table, qi3, misc3, idx_pool_t, layer):
    bd, tn, nqi = qi3.shape
    n_pages = page_table.shape[1]
    pp = math.gcd(n_pages, SCORE_PP)
    nsteps = n_pages // pp
    pool = lambda i: pl.BlockSpec((None, None, HEAD64, PAGE),
                                  lambda b, s, pt: (layer, pt[b, s * pp + i], 0, 0))
    kern = functools.partial(_dsa_sample_scores_kernel, tn=tn, nsteps=nsteps, pp=pp)
    return pl.pallas_call(
        kern,
        grid_spec=pltpu.PrefetchScalarGridSpec(
            num_scalar_prefetch=1,
            grid=(bd, nsteps),
            in_specs=[pl.BlockSpec((1, tn, nqi), lambda b, s, pt: (b, 0, 0)),
                      pl.BlockSpec((1, tn, LANES), lambda b, s, pt: (b, 0, 0))]
            + [pool(i) for i in range(pp)],
            out_specs=[pl.BlockSpec((1, pp, tn, PAGE), lambda b, s, pt: (b, s, 0, 0)),
                       pl.BlockSpec((1, tn, PAGE), lambda b, s, pt: (b, 0, 0))],
            scratch_shapes=[pltpu.VMEM((IDX_HEADS * tn, HEAD64), BF16),
                            pltpu.VMEM((IDX_HEADS * tn, 1), F32)]),
        out_shape=[jax.ShapeDtypeStruct((bd, n_pages, tn, PAGE), F32),
                   jax.ShapeDtypeStruct((bd, tn, PAGE), F32)],
        compiler_params=_cparams("parallel", "arbitrary"),
        name="dsa_sample_scores",
    )(page_table, qi3, misc3, *([idx_pool_t] * pp))


def _block_diag_queries(q, groups, tn):
    d = q.shape[1]
    qt = jnp.concatenate([q] * groups, axis=0)
    shift = int(math.log2(tn))
    rg = lax.broadcasted_iota(I32, (groups * tn, d), 0) >> shift
    cg = lax.broadcasted_iota(I32, (groups * tn, d), 1) >> 6
    return jnp.where(rg == cg, qt, 0.0).astype(BF16)


def _dsa_sample_select_kernel(sc_ref, scn_ref, bias_ref, *, nb, tn, n_pages, ksel):
    shape = (nb, n_pages + 1, tn, PAGE)
    page = lax.broadcasted_iota(I32, shape, 1)
    qidx = lax.broadcasted_iota(I32, shape, 2)
    lane = lax.broadcasted_iota(I32, shape, 3)
    score = jnp.concatenate([sc_ref[...], scn_ref[...].reshape(nb, 1, tn, PAGE)], axis=1)
    key = jnp.where((page < n_pages) | (lane <= qidx), _sortable_key(score), INT_MIN)
    krow = jnp.full((nb, 1, tn, 1), float(ksel), F32)
    count = lambda mask: jnp.sum(jnp.sum(jnp.where(mask, 1.0, 0.0), axis=1, keepdims=True),
                                 axis=-1, keepdims=True)
    sel = _select_topk(key, page * PAGE + lane, krow, count, ((n_pages + 1) * PAGE - 1).bit_length())
    bias_ref[...] = jnp.where(sel, 0.0, NEG_BIG)


def _dsa_sample_select(scores, scores_new, ksel):
    bd, n_pages, tn, _ = scores.shape
    nb = math.gcd(bd, SELECT_NB)
    return pl.pallas_call(
        functools.partial(_dsa_sample_select_kernel, nb=nb, tn=tn, n_pages=n_pages, ksel=ksel),
        grid=(bd // nb,),
        in_specs=[pl.BlockSpec((nb, n_pages, tn, PAGE), lambda i: (i, 0, 0, 0)),
                  pl.BlockSpec((nb, tn, PAGE), lambda i: (i, 0, 0))],
        out_specs=pl.BlockSpec((nb, n_pages + 1, tn, PAGE), lambda i: (i, 0, 0, 0)),
        out_shape=jax.ShapeDtypeStruct((bd, n_pages + 1, tn, PAGE), F32),
        compiler_params=_cparams("parallel"),
        name="dsa_sample_select",
    )(scores, scores_new)


def _dsa_sample_kernel(pt_ref, bias_ref, q_ref, kn_ref, vn_ref, *rest, tn, d, n_pages, nsteps, pp):
    del pt_ref
    kp_refs, vp_refs = rest[:pp], rest[pp:2 * pp]
    o_ref, qbd_sc, m_sc, l_sc, acc_sc = rest[2 * pp:]
    step = pl.program_id(1)
    groups = d // HEAD64
    rows = groups * tn

    @pl.when(step == 0)
    def _():
        qbd_sc[...] = _block_diag_queries(q_ref[0].astype(F32), groups, tn)
        m_sc[...] = jnp.full((rows, 1), NEG_BIG, F32)
        l_sc[...] = jnp.zeros((rows, 1), F32)
        acc_sc[...] = jnp.zeros((rows, d), F32)

    def tiled_bias(page):
        return jnp.concatenate([bias_ref[0, page]] * groups, axis=0)

    @pl.when(step < nsteps)
    def _():
        bias = jnp.concatenate([tiled_bias(step * pp + i) for i in range(pp)], axis=1)
        s = jnp.concatenate([_dot(qbd_sc[...], r[...].reshape(d, PAGE).astype(BF16)) for r in kp_refs],
                            axis=1) + bias
        pv = lambda pb: sum(_dot_nt(pb[:, i * PAGE:(i + 1) * PAGE], vp_refs[i][...].reshape(d, PAGE).astype(BF16))
                            for i in range(pp))
        _online_update(s, bias == 0.0, pv, m_sc, l_sc, acc_sc)

    @pl.when(step == nsteps)
    def _():
        bias = tiled_bias(n_pages)
        s = _dot_nt(qbd_sc[...], _pad_rows(kn_ref[0], PAGE).astype(BF16)) + bias
        pv = lambda pb: _dot(pb, _pad_rows(vn_ref[0], PAGE).astype(BF16))
        _online_update(s, bias == 0.0, pv, m_sc, l_sc, acc_sc)
        lo_half = _lane_iota() < HEAD64
        for c in range(d // LANES):
            cols = slice(c * LANES, (c + 1) * LANES)
            r0 = slice(2 * c * tn, (2 * c + 1) * tn)
            r1 = slice((2 * c + 1) * tn, (2 * c + 2) * tn)
            o = jnp.where(lo_half, acc_sc[r0, cols] / l_sc[r0, :], acc_sc[r1, cols] / l_sc[r1, :])
            o_ref[0, :, cols] = o.astype(BF16)


def _dsa_sample(page_table, bias, q3, kn3, vn3, k_pool_t, v_pool_t, layer):
    bd, tn, d = q3.shape
    n_pages = page_table.shape[1]
    groups = d // HEAD64
    rows = groups * tn
    pp = math.gcd(n_pages, SAMPLE_PP)
    nsteps = n_pages // pp
    seq = lambda: pl.BlockSpec((1, tn, d), lambda b, s, pt: (b, 0, 0))
    pool = lambda i: pl.BlockSpec(
        (None, None, groups, HEAD64, PAGE),
        lambda b, s, pt: (layer, pt[b, jnp.minimum(s, nsteps - 1) * pp + i], 0, 0, 0))
    kern = functools.partial(_dsa_sample_kernel, tn=tn, d=d, n_pages=n_pages, nsteps=nsteps, pp=pp)
    return pl.pallas_call(
        kern,
        grid_spec=pltpu.PrefetchScalarGridSpec(
            num_scalar_prefetch=1,
            grid=(bd, nsteps + 1),
            in_specs=[pl.BlockSpec((1, n_pages + 1, tn, PAGE), lambda b, s, pt: (b, 0, 0, 0)),
                      seq(), seq(), seq()] + [pool(i) for i in range(pp)] + [pool(i) for i in range(pp)],
            out_specs=seq(),
            scratch_shapes=[pltpu.VMEM((rows, d), BF16), pltpu.VMEM((rows, 1), F32),
                            pltpu.VMEM((rows, 1), F32), pltpu.VMEM((rows, d), F32)]),
        out_shape=jax.ShapeDtypeStruct((bd, tn, d), BF16),
        compiler_params=_cparams("parallel", "arbitrary"),
        name="dsa_sample",
    )(page_table, bias, q3, kn3, vn3, *([k_pool_t] * pp), *([v_pool_t] * pp))


def _hg_in_kernel(x_ref, g_ref, w_ref, lbw_ref, q_ref, k_ref, lf_ref, v_ref, gs_ref, *, d, layer):
    h = _rms(x_ref[...], g_ref[...]).astype(BF16)
    y = _dot(h, w_ref[...])
    lbw = lbw_ref[...]
    e = jnp.exp(lbw - jnp.max(lbw, axis=0, keepdims=True))
    sm = e / jnp.sum(e, axis=0, keepdims=True)
    lb = jnp.sum(sm[1:layer + 1, :], axis=0, keepdims=True)
    q, fz, v, g = y[:, :d], y[:, d:2 * d], y[:, 2 * d:3 * d], y[:, 3 * d:]
    f = lb + (1.0 - lb) * _sigmoid(fz)
    q_ref[...] = q * _sigmoid(q)
    k_ref[...] = 1.0 - f
    lf_ref[...] = jnp.log(f)
    v_ref[...] = v
    gs_ref[...] = g * _sigmoid(g)


def _hg_in(x, g1, w, lbw, layer):
    r, d = x.shape
    tm = min(r, ROW_TILE)
    row = lambda: pl.BlockSpec((tm, d), lambda i: (i, 0))
    return pl.pallas_call(
        functools.partial(_hg_in_kernel, d=d, layer=layer),
        grid=(r // tm,),
        in_specs=[row(), _const_spec((1, d)), _const_spec((d, 4 * d)), _const_spec(lbw.shape)],
        out_specs=[row()] * 5,
        out_shape=[jax.ShapeDtypeStruct((r, d), F32)] * 5,
        compiler_params=_cparams("parallel"),
        name="hg_in",
    )(x, g1, w, lbw)


def _hg_rec_kernel(q_ref, k_ref, lf_ref, v_ref, gs_ref, s0_ref, gn_ref, o_ref, s_ref, st_sc, *, tb, nt, hp):
    j = pl.program_id(2)

    @pl.when(j == 0)
    def _():
        for hh in range(hp):
            st_sc[hh] = s0_ref[hh].T

    c = HG_C
    r_io = lax.broadcasted_iota(I32, (c, c), 0)
    c_io = lax.broadcasted_iota(I32, (c, c), 1)
    causal = r_io >= c_io
    tri = jnp.where(causal, 1.0, 0.0)
    row = lax.broadcasted_iota(I32, (c, 1), 0)
    nch = max(1, tb // c)
    for ci, hh in [(ci, hh) for ci in range(nch) for hh in range(hp)]:
        cols = slice(hh * LANES, (hh + 1) * LANES)
        if tb >= c:
            rows = slice(ci * c, (ci + 1) * c)
            load = lambda ref: ref[rows, cols]
        else:
            load = lambda ref: _pad_rows(ref[:, cols], c)
        q, k, lf, v = load(q_ref), load(k_ref), load(lf_ref), load(v_ref)
        b = jnp.dot(tri, lf, preferred_element_type=F32, precision=lax.Precision.HIGHEST)
        st = st_sc[hh]
        o = _dot_nt((q * jnp.exp(b)).astype(BF16), st.astype(BF16))
        slabs = []
        for blk in range(c // HG_SUB):
            lo, hi = blk * HG_SUB, (blk + 1) * HG_SUB
            anchor = b[lo - 1:lo, :] if blk > 0 else jnp.zeros((1, LANES), F32)
            qb = q[lo:hi, :] * jnp.exp(b[lo:hi, :] - anchor)
            kb = k * jnp.exp(jnp.where(row < hi, anchor - b, -jnp.inf))
            slabs.append(_dot_nt(qb.astype(BF16), kb.astype(BF16)))
        a = jnp.where(causal, jnp.concatenate(slabs, axis=0), 0.0)
        o = o + _dot(a.astype(BF16), v.astype(BF16))
        b_last = b[c - 1:c, :]
        kd = k * jnp.exp(b_last - b)
        st_sc[hh] = st * jnp.exp(b_last) + _dot(v.T.astype(BF16), kd.astype(BF16))
        og = _rms(o, gn_ref[...])
        if tb >= c:
            o_ref[rows, cols] = (og * gs_ref[rows, cols]).astype(BF16)
        else:
            o_ref[:, cols] = (og[:tb, :] * gs_ref[:, cols]).astype(BF16)

    @pl.when(j == nt - 1)
    def _():
        for hh in range(hp):
            s_ref[hh] = st_sc[hh].T


def _hg_rec(q, k, lf, v, gs, s0, gn, bsz, t):
    r, d = q.shape
    heads = d // LANES
    hp = math.gcd(heads, HG_HP)
    tb = min(t, HG_TB)
    nt = t // tb
    blk = lambda: pl.BlockSpec((tb, hp * LANES), lambda b, h, j: (b * nt + j, h))
    state = lambda: pl.BlockSpec((None, hp, LANES, LANES), lambda b, h, j: (b, h, 0, 0))
    return pl.pallas_call(
        functools.partial(_hg_rec_kernel, tb=tb, nt=nt, hp=hp),
        grid=(bsz, heads // hp, nt),
        in_specs=[blk(), blk(), blk(), blk(), blk(), state(), _const_spec((1, LANES))],
        out_specs=[blk(), state()],
        out_shape=[jax.ShapeDtypeStruct((r, d), BF16),
                   jax.ShapeDtypeStruct((bsz, heads, LANES, LANES), F32)],
        scratch_shapes=[pltpu.VMEM((hp, LANES, LANES), F32)],
        compiler_params=_cparams("parallel", "parallel", "arbitrary"),
        name="hg_rec",
    )(q, k, lf, v, gs, s0, gn)


def kernel(x_prompt, x_sample, state_conv, cache_da_k, cache_da_v, cache_dsa_k, cache_dsa_v, cache_dsa_idx_k, state_hgrn, page_table, norm1_g, norm2_g, final_g, mlp_w1, mlp_w2, cv_w1, cv_b1, cv_dw, cv_dwb, cv_ln_g, cv_ln_b, cv_w2, cv_b2, da_w_in, da_lq1, da_lk1, da_lq2, da_lk2, da_subln_g, da_wo, dsa_w_in, dsa_wo, hg_w_in, hg_lb, hg_norm_g, hg_wo):
    bp, t, d = x_prompt.shape
    bd, tn, _ = x_sample.shape
    n_pages = page_table.shape[1]
    past_len = n_pages * PAGE
    depth = norm1_g.shape[0]
    assert d % LANES == 0 and t % ROW_TILE == 0 and tn & (tn - 1) == 0 and tn <= SUBLANES

    xp = x_prompt.reshape(bp * t, d)
    xs = x_sample.reshape(bd * tn, d)
    cos_p, sin_p = _rope_tables(jnp.arange(t))
    cos_s, sin_s = _rope_tables(past_len + jnp.arange(tn))
    cos_s, sin_s = jnp.tile(cos_s, (bd, 1)), jnp.tile(sin_s, (bd, 1))
    ksel_p = min(DSA_TOPK, t // 4)
    ksel_s = min(DSA_TOPK, (past_len + tn) // 4)
    zero_bias = jnp.zeros((1, d), F32)
    row1 = lambda a: a.reshape(1, -1)
    new = {n: [] for n in ('conv_p', 'conv_s', 'dak_p', 'dav_p', 'dak_s', 'dav_s', 'dsak_p', 'dsav_p', 'dsai_p',
                           'dsak_s', 'dsav_s', 'dsai_s', 'hg_p', 'hg_s')}

    for layer in range(depth):
        kind, j = layer % 4, layer // 4
        g1 = row1(norm1_g[layer])
        bo = zero_bias
        if kind == 0:
            pad = lambda a, n: jnp.concatenate([jnp.zeros(a.shape[:1] + (n,) + a.shape[2:], a.dtype), a], axis=1)
            dw = jnp.concatenate([cv_dw[j], jnp.zeros((CONV_HALO - CONV_W, d), F32)], axis=0)[:, None, :]
            cw = (g1, cv_w1[j].astype(BF16), row1(cv_b1[j]), dw, row1(cv_dwb[j]), row1(cv_ln_g[j]),
                  row1(cv_ln_b[j]))
            halo_pad = CONV_HALO - (CONV_W - 1)
            op, tail_p = _conv_mixer(xp.reshape(bp, t, d), jnp.zeros((bp, CONV_HALO, d), F32), *cw,
                                     nb=1, tt=ROW_TILE)
            os_, tail_s = _conv_mixer(xs.reshape(bd, tn, d), pad(state_conv[j], halo_pad), *cw, nb=bd, tt=tn)
            op, os_ = op.reshape(bp * t, d), os_.reshape(bd * tn, d)
            new['conv_p'].append(tail_p[:, halo_pad:])
            new['conv_s'].append(tail_s[:, halo_pad:])
            wo, bo = cv_w2[j].astype(BF16), row1(cv_b2[j])
        elif kind == 1:
            lam_init = 0.8 - 0.6 * math.exp(-0.3 * layer)
            lams = (row1(da_lq1[j]), row1(da_lk1[j]), row1(da_lq2[j]), row1(da_lk2[j]))
            subg = row1(da_subln_g[j])
            w_in = da_w_in[j].astype(BF16)
            qp, kp, vp, kbp, vbp = _da_in(xp, g1, w_in, cos_p, sin_p)
            qs, ks, vs, _, _ = _da_in(xs, g1, w_in, cos_s, sin_s)
            op = _da_prompt(qp, kbp, vbp, lams, subg, bp, t, lam_init)
            os_ = _da_sample(page_table, qs.reshape(bd, tn, d), ks.reshape(bd, tn, d), vs.reshape(bd, tn, d),
                             cache_da_k, cache_da_v, lams, subg, j, lam_init).reshape(bd * tn, d)
            hd = (d // LANES, LANES)
            new['dak_p'].append(kp.reshape((bp, t) + hd))
            new['dav_p'].append(vp.reshape((bp, t) + hd))
            new['dak_s'].append(ks.reshape((bd, tn) + hd))
            new['dav_s'].append(vs.reshape((bd, tn) + hd))
            wo = da_wo[j].astype(BF16)
        elif kind == 2:
            w_in = dsa_w_in[j]
            w_in = jnp.concatenate([w_in, jnp.zeros((d, -w_in.shape[1] % LANES), F32)], axis=1).astype(BF16)
            qp, kp, vp, kbp, vbp, qip, mp_ = _dsa_in(xp, g1, w_in, cos_p, sin_p, seq_len=t)
            qs, ks, vs, _, _, qis, ms_ = _dsa_in(xs, g1, w_in, cos_s, sin_s)
            op = _dsa_prompt(qip, mp_, qp, kbp, vbp, bp, t, ksel_p)
            scores, scores_new = _dsa_sample_scores(
                page_table, qis.reshape(bd, tn, -1), ms_.reshape(bd, tn, LANES),
                jnp.transpose(cache_dsa_idx_k, (0, 1, 3, 2)), j)
            bias = _dsa_sample_select(scores, scores_new, ksel_s)
            os_ = _dsa_sample(page_table, bias, qs.reshape(bd, tn, d), ks.reshape(bd, tn, d),
                              vs.reshape(bd, tn, d), jnp.transpose(cache_dsa_k, (0, 1, 3, 4, 2)),
                              jnp.transpose(cache_dsa_v, (0, 1, 3, 4, 2)), j).reshape(bd * tn, d)
            hd = (d // HEAD64, HEAD64)
            new['dsak_p'].append(jnp.transpose(kp.reshape((bp,) + hd + (t,)), (0, 3, 1, 2)))
            new['dsav_p'].append(jnp.transpose(vp.reshape((bp,) + hd + (t,)), (0, 3, 1, 2)))
            new['dsai_p'].append(mp_[:, :HEAD64].reshape(bp, t, HEAD64))
            new['dsak_s'].append(ks.reshape((bd, tn) + hd))
            new['dsav_s'].append(vs.reshape((bd, tn) + hd))
            new['dsai_s'].append(ms_[:, :HEAD64].reshape(bd, tn, HEAD64))
            wo = dsa_wo[j].astype(BF16)
        else:
            w_in = hg_w_in[j].astype(BF16)
            gn = row1(hg_norm_g[j])
            heads = d // LANES
            hp = _hg_in(xp, g1, w_in, hg_lb, layer)
            hs = _hg_in(xs, g1, w_in, hg_lb, layer)
            op, sp = _hg_rec(*hp, jnp.zeros((bp, heads, LANES, LANES), F32), gn, bp, t)
            os_, ss = _hg_rec(*hs, state_hgrn[j], gn, bd, tn)
            new['hg_p'].append(sp.astype(state_hgrn.dtype))
            new['hg_s'].append(ss.astype(state_hgrn.dtype))
            wo = hg_wo[j].astype(BF16)

        final = layer == depth - 1
        post = (wo, bo, row1(norm2_g[layer]), mlp_w1[layer].astype(BF16), mlp_w2[layer].astype(BF16),
                row1(final_g))
        xp = _post(xp, op, *post, final=final)
        xs = _post(xs, os_, *post, final=final)

    return (xp.reshape(bp, t, d), xs.reshape(bd, tn, d),
            jnp.stack(new['conv_p']), jnp.stack(new['conv_s']),
            jnp.stack(new['dak_p']), jnp.stack(new['dav_p']), jnp.stack(new['dak_s']), jnp.stack(new['dav_s']),
            jnp.stack(new['dsak_p']), jnp.stack(new['dsav_p']), jnp.stack(new['dsai_p']),
            jnp.stack(new['dsak_s']), jnp.stack(new['dsav_s']), jnp.stack(new['dsai_s']),
            jnp.stack(new['hg_p']), jnp.stack(new['hg_s']))
```

```python
import functools
import math

import jax
import jax.numpy as jnp
from jax import lax
from jax.experimental import pallas as pl
from jax.experimental.pallas import tpu as pltpu

F32 = jnp.float32
BF16 = jnp.bfloat16
I32 = jnp.int32

NORM_EPS = 1e-6
LN_EPS = 1e-5
ROPE_THETA = 10000.0
ROPE_DIM = 64
CONV_W = 31
PAGE = 128
HEAD64 = 64
IDX_HEADS = 8
DSA_TOPK = 256

LANES = 128
SUBLANES = 8
CONV_HALO = 32
CONV_ROWS = 32
ROW_TILE = 256
ATT_TQ = 256
DSA_TQ = 128
DSA_KV_GROUPS = 4
SAMPLE_PP = 8
SCORE_PP = 8
HG_C = 128
HG_SUB = 16
HG_TB = 512
HG_HP = 4
SELECT_NB = 8
MLP_FCHUNK = 1024
VMEM_LIMIT = 56 * 1024 * 1024
NEG_BIG = -1e30
INT_MIN = -2 ** 31


def _cparams(*sem):
    return pltpu.CompilerParams(dimension_semantics=sem, vmem_limit_bytes=VMEM_LIMIT)


def _const_spec(shape):
    nd = len(shape)
    return pl.BlockSpec(shape, lambda *_: (0,) * nd, pipeline_mode=pl.Buffered(1))


def _dot(a, b):
    return jnp.dot(a, b, preferred_element_type=F32)


def _dot_nt(a, b):
    return lax.dot_general(a, b, (((1,), (1,)), ((), ())), preferred_element_type=F32)


def _rms(x, g):
    return x * lax.rsqrt(jnp.mean(x * x, axis=-1, keepdims=True) + NORM_EPS) * g


def _sigmoid(x):
    return 1.0 / (1.0 + jnp.exp(-x))


def _lane_iota(shape=(1, LANES)):
    return lax.broadcasted_iota(I32, shape, len(shape) - 1)


def _rope(xc, cos, sin_signed):
    first_half = (_lane_iota() & 32) == 0
    partner = jnp.where(first_half, pltpu.roll(xc, 96, 1), pltpu.roll(xc, 32, 1))
    return xc * cos + partner * sin_signed


def _rope_tables(pos):
    inv = ROPE_THETA ** (-jnp.arange(0, ROPE_DIM, 2, dtype=F32) / ROPE_DIM)
    ang = pos.astype(F32)[:, None] * inv[None, :]
    cos, sin = jnp.cos(ang), jnp.sin(ang)
    return jnp.tile(cos, (1, 4)), jnp.tile(jnp.concatenate([-sin, sin], axis=1), (1, 2))


def _pad_rows(x, rows):
    return jnp.concatenate([x, jnp.zeros((rows - x.shape[0], x.shape[1]), x.dtype)], axis=0)


def _post_kernel(x_ref, o_ref, wo_ref, bo_ref, g_ref, w1_ref, w2_ref, fg_ref, y_ref, h_ref, *, final):
    x1 = x_ref[...] + _dot(o_ref[...], wo_ref[...]) + bo_ref[...]
    h = _rms(x1, g_ref[...]).astype(BF16)
    f = w1_ref.shape[1]
    fc = min(f, MLP_FCHUNK)
    for c in range(f // fc):
        a = jnp.maximum(_dot(h, w1_ref[:, c * fc:(c + 1) * fc]), 0.0)
        h_ref[:, c * fc:(c + 1) * fc] = (a * a).astype(BF16)
    y = x1 + _dot(h_ref[...], w2_ref[...])
    if final:
        y = _rms(y, fg_ref[...])
    y_ref[...] = y


def _post(x, o, wo, bo, g2, w1, w2, fg, final):
    r, d = x.shape
    f = w1.shape[1]
    tm = min(r, ROW_TILE)
    row = lambda w: pl.BlockSpec((tm, w), lambda i: (i, 0))
    return pl.pallas_call(
        functools.partial(_post_kernel, final=final),
        grid=(r // tm,),
        in_specs=[row(d), row(d), _const_spec((d, d)), _const_spec((1, d)), _const_spec((1, d)),
                  _const_spec((d, f)), _const_spec((f, d)), _const_spec((1, d))],
        out_specs=row(d),
        out_shape=jax.ShapeDtypeStruct((r, d), F32),
        scratch_shapes=[pltpu.VMEM((tm, f), BF16)],
        compiler_params=_cparams("parallel"),
        name="post_mlp",
    )(x, o, wo, bo, g2, w1, w2, fg)


def _conv_kernel(x_ref, buf_ref, g_ref, w1_ref, b1_ref, dw_ref, dwb_ref, lng_ref, lnb_ref,
                 o_ref, tail_ref, ext_ref, y_ref, sh_ref, *, nb, tt, d, carry, preshift):
    @pl.when(pl.program_id(1) == 0)
    def _():
        ext_ref[:, 0:CONV_HALO, :] = buf_ref[...]

    x = x_ref[...].reshape(nb * tt, d)
    h = _rms(x, g_ref[...]).astype(BF16)
    ag = _dot(h, w1_ref[...]) + b1_ref[...]
    u = ag[:, :d] * _sigmoid(ag[:, d:])
    ext_ref[:, CONV_HALO:CONV_HALO + tt, :] = u.reshape(nb, tt, d)

    if preshift:
        for a in range(1, SUBLANES):
            sh_ref[a - 1] = ext_ref[0, a:a + sh_ref.shape[1], :]

    rs = min(tt, CONV_ROWS)
    for r in range(tt // rs):
        acc = jnp.zeros((nb, rs, d), F32)
        for k in range(CONV_W):
            off = k + CONV_HALO - (CONV_W - 1)
            a = off % SUBLANES
            if preshift and a:
                start = r * rs + off - a
                window = sh_ref[a - 1, start:start + rs, :][None]
            else:
                window = ext_ref[:, r * rs + off:r * rs + off + rs, :]
            acc = acc + dw_ref[k] * window
        y_ref[:, r * rs:(r + 1) * rs, :] = acc + dwb_ref[...]

    y = y_ref[...]
    mu = jnp.mean(y, axis=-1, keepdims=True)
    yc = y - mu
    var = jnp.mean(yc * yc, axis=-1, keepdims=True)
    z = yc * lax.rsqrt(var + LN_EPS) * lng_ref[...] + lnb_ref[...]
    o_ref[...] = (z * _sigmoid(z)).astype(BF16)
    tail = ext_ref[:, tt:tt + CONV_HALO, :]
    tail_ref[...] = tail
    if carry:
        ext_ref[:, 0:CONV_HALO, :] = tail


def _conv_mixer(x3, buf, g1, w1, b1, dw, dwb, lng, lnb, nb, tt):
    b, t, d = x3.shape
    nt = t // tt
    preshift = nb == 1 and tt > CONV_ROWS
    sh_shape = (SUBLANES - 1, tt + CONV_HALO - SUBLANES, d) if preshift else (1, SUBLANES, LANES)
    kern = functools.partial(_conv_kernel, nb=nb, tt=tt, d=d, carry=nt > 1, preshift=preshift)
    return pl.pallas_call(
        kern,
        grid=(b // nb, nt),
        in_specs=[pl.BlockSpec((nb, tt, d), lambda i, j: (i, j, 0)),
                  pl.BlockSpec((nb, CONV_HALO, d), lambda i, j: (i, 0, 0)),
                  _const_spec((1, d)), _const_spec((d, 2 * d)), _const_spec((1, 2 * d)),
                  _const_spec((CONV_HALO, 1, d)), _const_spec((1, d)), _const_spec((1, d)),
                  _const_spec((1, d))],
        out_specs=[pl.BlockSpec((nb, tt, d), lambda i, j: (i, j, 0)),
                   pl.BlockSpec((nb, CONV_HALO, d), lambda i, j: (i, 0, 0))],
        out_shape=[jax.ShapeDtypeStruct((b, t, d), BF16),
                   jax.ShapeDtypeStruct((b, CONV_HALO, d), F32)],
        scratch_shapes=[pltpu.VMEM((nb, CONV_HALO + tt, d), F32), pltpu.VMEM((nb, tt, d), F32),
                        pltpu.VMEM(sh_shape, F32)],
        compiler_params=_cparams("parallel", "arbitrary"),
        name="conv_mixer",
    )(x3, buf, g1, w1, b1, dw, dwb, lng, lnb)


def _da_in_kernel(x_ref, g_ref, w_ref, cos_ref, sin_ref, q_ref, k_ref, v_ref, kb_ref, vb_ref, *, d):
    h = _rms(x_ref[...], g_ref[...]).astype(BF16)
    y = _dot(h, w_ref[...])
    cos, sin = cos_ref[...], sin_ref[...]
    for c in range(d // LANES):
        sl = slice(c * LANES, (c + 1) * LANES)
        q = _rope(y[:, c * LANES:(c + 1) * LANES], cos, sin)
        q_ref[:, sl] = (q * HEAD64 ** -0.5).astype(BF16)
        k = _rope(y[:, d + c * LANES:d + (c + 1) * LANES], cos, sin)
        k_ref[:, sl] = k
        kb_ref[:, sl] = k.astype(BF16)
        v = y[:, 2 * d + c * LANES:2 * d + (c + 1) * LANES]
        v_ref[:, sl] = v
        vb_ref[:, sl] = v.astype(BF16)


def _da_in(x, g1, w, cos, sin):
    r, d = x.shape
    tm = min(r, ROW_TILE)
    ntab = cos.shape[0] // tm
    row = lambda: pl.BlockSpec((tm, d), lambda i: (i, 0))
    tab = lambda: pl.BlockSpec((tm, LANES), lambda i: (i % ntab, 0))
    return pl.pallas_call(
        functools.partial(_da_in_kernel, d=d),
        grid=(r // tm,),
        in_specs=[row(), _const_spec((1, d)), _const_spec((d, 3 * d)), tab(), tab()],
        out_specs=[row(), row(), row(), row(), row()],
        out_shape=[jax.ShapeDtypeStruct((r, d), BF16), jax.ShapeDtypeStruct((r, d), F32),
                   jax.ShapeDtypeStruct((r, d), F32), jax.ShapeDtypeStruct((r, d), BF16),
                   jax.ShapeDtypeStruct((r, d), BF16)],
        compiler_params=_cparams("parallel"),
        name="da_in",
    )(x, g1, w, cos, sin)


def _da_lambda(lq1, lk1, lq2, lk2, lam_init):
    return (jnp.exp(jnp.sum(lq1[...] * lk1[...], axis=-1, keepdims=True))
            - jnp.exp(jnp.sum(lq2[...] * lk2[...], axis=-1, keepdims=True)) + lam_init)


def _da_prompt_kernel(lq1, lk1, lq2, lk2, subg_ref, q_ref, k_ref, v_ref, o_ref, *, tq, nq, lam_init):
    i = pl.program_id(2)
    lam = _da_lambda(lq1, lk1, lq2, lk2, lam_init)
    q = q_ref[...]
    lane = _lane_iota()
    zero = jnp.zeros_like(q)
    q1 = jnp.where(lane < HEAD64, q, zero)
    q2 = jnp.where(lane >= HEAD64, q, zero)
    keep = (lax.broadcasted_iota(I32, (tq, tq), 0) >= lax.broadcasted_iota(I32, (tq, tq), 1))

    def body(c):
        off = c * tq

        def softmax_parts(qm):
            sd = jnp.where(keep, _dot_nt(qm, k_ref[off:off + tq, :]), NEG_BIG)
            m = jnp.max(sd, axis=-1, keepdims=True)
            so = None
            if c > 0:
                so = _dot_nt(qm, k_ref[0:off, :])
                m = jnp.maximum(m, jnp.max(so, axis=-1, keepdims=True))
            pd = jnp.exp(sd - m)
            l = jnp.sum(pd, axis=-1, keepdims=True)
            po = None
            if c > 0:
                po = jnp.exp(so - m)
                l = l + jnp.sum(po, axis=-1, keepdims=True)
            return pd, po, l

        pd1, po1, l1 = softmax_parts(q1)
        pd2, po2, l2 = softmax_parts(q2)
        w1 = 1.0 / l1
        w2 = lam / l2
        o = _dot((pd1 * w1 - pd2 * w2).astype(BF16), v_ref[off:off + tq, :])
        if c > 0:
            o = o + _dot((po1 * w1 - po2 * w2).astype(BF16), v_ref[0:off, :])
        o_ref[...] = (_rms(o, subg_ref[...]) * (1.0 - lam_init)).astype(BF16)

    for c in range(nq):
        pl.when(i == c)(functools.partial(body, c))


def _da_prompt(q, kb, vb, lams, subg, bsz, t, lam_init):
    r, d = q.shape
    tq = min(t, ATT_TQ)
    nq = t // tq
    lam_spec = _const_spec((1, HEAD64))
    return pl.pallas_call(
        functools.partial(_da_prompt_kernel, tq=tq, nq=nq, lam_init=lam_init),
        grid=(bsz, d // LANES, nq),
        in_specs=[lam_spec, lam_spec, lam_spec, lam_spec, _const_spec((1, LANES)),
                  pl.BlockSpec((tq, LANES), lambda b, h, i: (b * nq + i, h)),
                  pl.BlockSpec((t, LANES), lambda b, h, i: (b, h)),
                  pl.BlockSpec((t, LANES), lambda b, h, i: (b, h))],
        out_specs=pl.BlockSpec((tq, LANES), lambda b, h, i: (b * nq + i, h)),
        out_shape=jax.ShapeDtypeStruct((r, d), BF16),
        compiler_params=_cparams("parallel", "parallel", "arbitrary"),
        name="da_prompt",
    )(*lams, subg, q, kb, vb)


def _online_update(s, valid, pv_fn, m_sc, l_sc, acc_sc):
    m_old = m_sc[...]
    m_new = jnp.maximum(m_old, jnp.max(s, axis=-1, keepdims=True))
    alpha = jnp.exp(m_old - m_new)
    p = jnp.exp(s - m_new)
    if valid is not None:
        p = jnp.where(valid, p, 0.0)
    l_sc[...] = alpha * l_sc[...] + jnp.sum(p, axis=-1, keepdims=True)
    acc_sc[...] = alpha * acc_sc[...] + pv_fn(p.astype(BF16))
    m_sc[...] = m_new


def _da_sample_kernel(pt_ref, lq1, lk1, lq2, lk2, subg_ref, q_ref, kn_ref, vn_ref, *rest,
                      tn, d, lam_init, nsteps, pp):
    del pt_ref
    kp_refs, vp_refs = rest[:pp], rest[pp:2 * pp]
    o_ref, qh_sc, own_sc, m_sc, l_sc, acc_sc = rest[2 * pp:]
    step = pl.program_id(1)
    heads = d // LANES
    gr = 2 * tn
    rows = heads * gr
    ncol = pp * PAGE * heads

    @pl.when(step == 0)
    def _():
        q = q_ref[0].astype(F32)
        first_map = lax.broadcasted_iota(I32, (gr, LANES), 0) < tn
        lo_half = _lane_iota((gr, LANES)) < HEAD64
        for h in range(heads):
            qc = q[:, h * LANES:(h + 1) * LANES]
            qh_sc[h * gr:(h + 1) * gr, :] = jnp.where(
                first_map == lo_half, jnp.concatenate([qc, qc], axis=0), 0.0).astype(BF16)
        row_head = lax.broadcasted_iota(I32, (rows, ncol), 0) >> int(math.log2(gr))
        col_head = lax.broadcasted_iota(I32, (rows, ncol), 1) & (heads - 1)
        own_sc[...] = jnp.where(row_head == col_head, 0.0, NEG_BIG)
        m_sc[...] = jnp.full((rows, 1), NEG_BIG, F32)
        l_sc[...] = jnp.zeros((rows, 1), F32)
        acc_sc[...] = jnp.zeros((rows, LANES), F32)

    flat = lambda refs: jnp.concatenate([r[...] for r in refs], axis=0).astype(BF16)
    s = _dot_nt(qh_sc[...], flat(kp_refs)) + own_sc[...]
    _online_update(s, None, lambda pb: _dot(pb, flat(vp_refs)), m_sc, l_sc, acc_sc)

    @pl.when(step == nsteps - 1)
    def _():
        fresh = lambda ref, h: _pad_rows(ref[0][:, h * LANES:(h + 1) * LANES], PAGE).astype(BF16)
        qpos = lax.broadcasted_iota(I32, (rows, PAGE), 0) & (tn - 1)
        valid = lax.broadcasted_iota(I32, (rows, PAGE), 1) <= qpos
        s_new = jnp.concatenate(
            [_dot_nt(qh_sc[h * gr:(h + 1) * gr, :], fresh(kn_ref, h)) for h in range(heads)], axis=0)
        pv = lambda pb: jnp.concatenate(
            [_dot(pb[h * gr:(h + 1) * gr, :], fresh(vn_ref, h)) for h in range(heads)], axis=0)
        _online_update(jnp.where(valid, s_new, NEG_BIG), valid, pv, m_sc, l_sc, acc_sc)
        lam = _da_lambda(lq1, lk1, lq2, lk2, lam_init)
        for h in range(heads):
            r1 = slice(h * gr, h * gr + tn)
            r2 = slice(h * gr + tn, (h + 1) * gr)
            o = acc_sc[r1, :] / l_sc[r1, :] - lam * (acc_sc[r2, :] / l_sc[r2, :])
            o_ref[0, :, h * LANES:(h + 1) * LANES] = (_rms(o, subg_ref[...]) * (1.0 - lam_init)).astype(BF16)


def _da_sample(page_table, q3, kn3, vn3, k_pool, v_pool, lams, subg, layer, lam_init):
    bd, tn, d = q3.shape
    n_pages = page_table.shape[1]
    heads = d // LANES
    pp = math.gcd(n_pages, SAMPLE_PP)
    nsteps = n_pages // pp
    rows = heads * 2 * tn
    assert heads & (heads - 1) == 0
    flat_shape = k_pool.shape[:2] + (PAGE * heads, LANES)
    k_pool, v_pool = k_pool.reshape(flat_shape), v_pool.reshape(flat_shape)
    seq = lambda: pl.BlockSpec((1, tn, d), lambda b, s, pt: (b, 0, 0))
    pool = lambda i: pl.BlockSpec((None, None, PAGE * heads, LANES),
                                  lambda b, s, pt: (layer, pt[b, s * pp + i], 0, 0))
    lam_spec = _const_spec((1, HEAD64))
    kern = functools.partial(_da_sample_kernel, tn=tn, d=d, lam_init=lam_init, nsteps=nsteps, pp=pp)
    return pl.pallas_call(
        kern,
        grid_spec=pltpu.PrefetchScalarGridSpec(
            num_scalar_prefetch=1,
            grid=(bd, nsteps),
            in_specs=[lam_spec, lam_spec, lam_spec, lam_spec, _const_spec((1, LANES)), seq(), seq(), seq()]
            + [pool(i) for i in range(pp)] + [pool(i) for i in range(pp)],
            out_specs=seq(),
            scratch_shapes=[pltpu.VMEM((rows, LANES), BF16), pltpu.VMEM((rows, pp * PAGE * heads), F32),
                            pltpu.VMEM((rows, 1), F32), pltpu.VMEM((rows, 1), F32),
                            pltpu.VMEM((rows, LANES), F32)]),
        out_shape=jax.ShapeDtypeStruct((bd, tn, d), BF16),
        compiler_params=_cparams("parallel", "arbitrary"),
        name="da_sample",
    )(page_table, *lams, subg, q3, kn3, vn3, *([k_pool] * pp), *([v_pool] * pp))


def _dsa_in_kernel(x_ref, g_ref, w_ref, cos_ref, sin_ref, q_ref, k_ref, v_ref, kb_ref, vb_ref,
                   qi_ref, misc_ref, *, d, token_minor):
    h = _rms(x_ref[...], g_ref[...]).astype(BF16)
    y = _dot(h, w_ref[...])
    cos, sin = cos_ref[...], sin_ref[...]
    for c in range(d // LANES):
        sl = slice(c * LANES, (c + 1) * LANES)
        q = _rope(y[:, c * LANES:(c + 1) * LANES], cos, sin)
        q_ref[:, sl] = (q * HEAD64 ** -0.5).astype(BF16)
        k = _rope(y[:, d + c * LANES:d + (c + 1) * LANES], cos, sin)
        kb_ref[:, sl] = k.astype(BF16)
        v = y[:, 2 * d + c * LANES:2 * d + (c + 1) * LANES]
        vb_ref[:, sl] = v.astype(BF16)
        if token_minor:
            k_ref[sl, :] = k.T
            v_ref[sl, :] = v.T
        else:
            k_ref[:, sl] = k
            v_ref[:, sl] = v
    for c in range(IDX_HEADS * HEAD64 // LANES):
        qi = _rope(y[:, 3 * d + c * LANES:3 * d + (c + 1) * LANES], cos, sin)
        qi_ref[:, c * LANES:(c + 1) * LANES] = (qi * HEAD64 ** -0.5).astype(BF16)
    is_key = _lane_iota() < HEAD64
    base = 3 * d + IDX_HEADS * HEAD64
    misc_ref[...] = _rope(y[:, base:base + LANES], jnp.where(is_key, cos, 1.0), jnp.where(is_key, sin, 0.0))


def _dsa_in(x, g1, w, cos, sin, seq_len=None):
    r, d = x.shape
    tm = min(r, ROW_TILE)
    ntab = cos.shape[0] // tm
    nqi = IDX_HEADS * HEAD64
    row = lambda w_: pl.BlockSpec((tm, w_), lambda i: (i, 0))
    tab = lambda: pl.BlockSpec((tm, LANES), lambda i: (i % ntab, 0))
    if seq_len is None:
        kv_spec, kv_shape = row(d), jax.ShapeDtypeStruct((r, d), F32)
    else:
        nt = seq_len // tm
        kv_spec = pl.BlockSpec((None, d, tm), lambda i: (i // nt, 0, i % nt))
        kv_shape = jax.ShapeDtypeStruct((r // seq_len, d, seq_len), F32)
    return pl.pallas_call(
        functools.partial(_dsa_in_kernel, d=d, token_minor=seq_len is not None),
        grid=(r // tm,),
        in_specs=[row(d), _const_spec((1, d)), _const_spec(w.shape), tab(), tab()],
        out_specs=[row(d), kv_spec, kv_spec, row(d), row(d), row(nqi), row(LANES)],
        out_shape=[jax.ShapeDtypeStruct((r, d), BF16), kv_shape, kv_shape,
                   jax.ShapeDtypeStruct((r, d), BF16),
                   jax.ShapeDtypeStruct((r, d), BF16), jax.ShapeDtypeStruct((r, nqi), BF16),
                   jax.ShapeDtypeStruct((r, LANES), F32)],
        compiler_params=_cparams("parallel"),
        name="dsa_in",
    )(x, g1, w, cos, sin)


def _sortable_key(score):
    bits = lax.bitcast_convert_type(score + 0.0, I32)
    return bits ^ ((bits >> 31) & 0x7FFFFFFF)


def _select_topk(key, pos, krow, count, pos_bits):
    def ge_step(it, ans):
        cand = ans | lax.shift_left(jnp.int32(1), 30 - it)
        return jnp.where(count(key >= cand) >= krow, cand, ans)

    ans = jnp.where(count(key >= 0) >= krow, jnp.int32(0), jnp.int32(INT_MIN))
    ans = lax.fori_loop(0, 31, ge_step, ans)
    gt = key > ans
    eq = key == ans
    need = krow - count(gt)

    def pos_step(it, cut):
        cand = cut | lax.shift_left(jnp.int32(1), pos_bits - 1 - it)
        return jnp.where(count(eq & (pos < cand)) < need, cand, cut)

    surplus = jnp.max(count(eq) - need) > 0.0
    cut = lax.cond(surplus,
                   lambda: lax.fori_loop(0, pos_bits, pos_step, jnp.zeros(krow.shape, I32)),
                   lambda: jnp.full(krow.shape, 2 ** pos_bits, I32))
    return gt | (eq & (pos <= cut))


def _dsa_prompt_kernel(qi_ref, wq_ref, kim_ref, q_ref, k_ref, v_ref, o_ref, kid_sc, *, tq, nq, d, ksel):
    i = pl.program_id(1)
    lane = _lane_iota()
    lo_half = lane < HEAD64

    @pl.when(i == 0)
    def _():
        lo = jnp.where(lo_half, kim_ref[...], 0.0)
        kid_sc[...] = (lo + pltpu.roll(lo, HEAD64, 1)).astype(BF16)

    def body(kv):
        wt = wq_ref[...] * IDX_HEADS ** -0.5
        kid = kid_sc[0:kv, :]
        score = jnp.zeros((tq, kv), F32)
        for h in range(IDX_HEADS):
            qc = qi_ref[:, (h // 2) * LANES:(h // 2 + 1) * LANES]
            qp = jnp.where(lo_half if h % 2 == 0 else ~lo_half, qc, jnp.zeros_like(qc))
            dots = _dot_nt(qp, kid)
            score = score + wt[:, HEAD64 + h:HEAD64 + h + 1] * jnp.maximum(dots, 0.0)

        qpos = i * tq + lax.broadcasted_iota(I32, (tq, 1), 0)
        kpos = lax.broadcasted_iota(I32, (tq, kv), 1)
        key = jnp.where(kpos <= qpos, _sortable_key(score), INT_MIN)
        krow = jnp.minimum(qpos + 1, ksel).astype(F32)
        count = lambda mask: jnp.sum(jnp.where(mask, 1.0, 0.0), axis=-1, keepdims=True)
        sel = _select_topk(key, kpos, krow, count, max(1, (kv - 1).bit_length()))
        bias = jnp.where(sel, 0.0, NEG_BIG)

        for c in range(d // LANES):
            cols = slice(c * LANES, (c + 1) * LANES)
            qc = q_ref[:, cols]
            kc = k_ref[0:kv, cols]
            vc = v_ref[0:kv, cols]
            halves = []
            for half in range(2):
                qp = jnp.where(lo_half if half == 0 else ~lo_half, qc, jnp.zeros_like(qc))
                s = _dot_nt(qp, kc) + bias
                p = jnp.exp(s - jnp.max(s, axis=-1, keepdims=True))
                l = jnp.sum(p, axis=-1, keepdims=True)
                halves.append(_dot(p.astype(BF16), vc) / l)
            o_ref[:, cols] = jnp.where(lo_half, halves[0], halves[1]).astype(BF16)

    groups = math.gcd(nq, DSA_KV_GROUPS)
    per = nq // groups
    for g in range(groups):
        pl.when((i >= g * per) & (i < (g + 1) * per))(functools.partial(body, (g + 1) * per * tq))


def _dsa_prompt(qi, misc, q, kb, vb, bsz, t, ksel):
    r, d = q.shape
    tq = min(t, DSA_TQ)
    nq = t // tq
    nqi = qi.shape[1]
    qrow = lambda w: pl.BlockSpec((tq, w), lambda b, i: (b * nq + i, 0))
    seq = lambda w: pl.BlockSpec((t, w), lambda b, i: (b, 0))
    return pl.pallas_call(
        functools.partial(_dsa_prompt_kernel, tq=tq, nq=nq, d=d, ksel=ksel),
        grid=(bsz, nq),
        in_specs=[qrow(nqi), qrow(LANES), seq(LANES), qrow(d), seq(d), seq(d)],
        out_specs=qrow(d),
        out_shape=jax.ShapeDtypeStruct((r, d), BF16),
        scratch_shapes=[pltpu.VMEM((t, LANES), BF16)],
        compiler_params=_cparams("parallel", "arbitrary"),
        name="dsa_prompt",
    )(qi, misc, misc, q, kb, vb)


def _dsa_sample_scores_kernel(pt_ref, qi_ref, misc_ref, *rest, tn, nsteps, pp):
    del pt_ref
    pool_refs = rest[:pp]
    sc_ref, scn_ref, qst_sc, w_sc = rest[pp:]
    step = pl.program_id(1)

    @pl.when(step == 0)
    def _():
        qi = qi_ref[0].astype(F32)
        misc = misc_ref[0]
        qst_sc[...] = jnp.concatenate(
            [qi[:, h * HEAD64:(h + 1) * HEAD64] for h in range(IDX_HEADS)], axis=0).astype(BF16)
        w_sc[...] = jnp.concatenate(
            [misc[:, HEAD64 + h:HEAD64 + h + 1] for h in range(IDX_HEADS)], axis=0) * IDX_HEADS ** -0.5

    def head_sum(dots):
        sc = w_sc[...] * jnp.maximum(dots, 0.0)
        out = sc[0:tn, :]
        for h in range(1, IDX_HEADS):
            out = out + sc[h * tn:(h + 1) * tn, :]
        return out

    for i in range(pp):
        sc_ref[0, i] = head_sum(_dot(qst_sc[...], pool_refs[i][...].astype(BF16)))

    @pl.when(step == nsteps - 1)
    def _():
        fresh = _pad_rows(misc_ref[0][:, :HEAD64], PAGE).astype(BF16)
        scn_ref[0] = head_sum(_dot_nt(qst_sc[...], fresh))


def _dsa_sample_scores(page_table, qi3, misc3, idx_pool_t, layer):
    bd, tn, nqi = qi3.shape
    n_pages = page_table.shape[1]
    pp = math.gcd(n_pages, SCORE_PP)
    nsteps = n_pages // pp
    pool = lambda i: pl.BlockSpec((None, None, HEAD64, PAGE),
                                  lambda b, s, pt: (layer, pt[b, s * pp + i], 0, 0))
    kern = functools.partial(_dsa_sample_scores_kernel, tn=tn, nsteps=nsteps, pp=pp)
    return pl.pallas_call(
        kern,
        grid_spec=pltpu.PrefetchScalarGridSpec(
            num_scalar_prefetch=1,
            grid=(bd, nsteps),
            in_specs=[pl.BlockSpec((1, tn, nqi), lambda b, s, pt: (b, 0, 0)),
                      pl.BlockSpec((1, tn, LANES), lambda b, s, pt: (b, 0, 0))]
            + [pool(i) for i in range(pp)],
            out_specs=[pl.BlockSpec((1, pp, tn, PAGE), lambda b, s, pt: (b, s, 0, 0)),
                       pl.BlockSpec((1, tn, PAGE), lambda b, s, pt: (b, 0, 0))],
            scratch_shapes=[pltpu.VMEM((IDX_HEADS * tn, HEAD64), BF16),
                            pltpu.VMEM((IDX_HEADS * tn, 1), F32)]),
        out_shape=[jax.ShapeDtypeStruct((bd, n_pages, tn, PAGE), F32),
                   jax.ShapeDtypeStruct((bd, tn, PAGE), F32)],
        compiler_params=_cparams("parallel", "arbitrary"),
        name="dsa_sample_scores",
    )(page_table, qi3, misc3, *([idx_pool_t] * pp))


def _block_diag_queries(q, groups, tn):
    d = q.shape[1]
    qt = jnp.concatenate([q] * groups, axis=0)
    shift = int(math.log2(tn))
    rg = lax.broadcasted_iota(I32, (groups * tn, d), 0) >> shift
    cg = lax.broadcasted_iota(I32, (groups * tn, d), 1) >> 6
    return jnp.where(rg == cg, qt, 0.0).astype(BF16)


def _dsa_sample_select_kernel(sc_ref, scn_ref, bias_ref, *, nb, tn, n_pages, ksel):
    shape = (nb, n_pages + 1, tn, PAGE)
    page = lax.broadcasted_iota(I32, shape, 1)
    qidx = lax.broadcasted_iota(I32, shape, 2)
    lane = lax.broadcasted_iota(I32, shape, 3)
    score = jnp.concatenate([sc_ref[...], scn_ref[...].reshape(nb, 1, tn, PAGE)], axis=1)
    key = jnp.where((page < n_pages) | (lane <= qidx), _sortable_key(score), INT_MIN)
    krow = jnp.full((nb, 1, tn, 1), float(ksel), F32)
    count = lambda mask: jnp.sum(jnp.sum(jnp.where(mask, 1.0, 0.0), axis=1, keepdims=True),
                                 axis=-1, keepdims=True)
    sel = _select_topk(key, page * PAGE + lane, krow, count, ((n_pages + 1) * PAGE - 1).bit_length())
    bias_ref[...] = jnp.where(sel, 0.0, NEG_BIG)


def _dsa_sample_select(scores, scores_new, ksel):
    bd, n_pages, tn, _ = scores.shape
    nb = math.gcd(bd, SELECT_NB)
    return pl.pallas_call(
        functools.partial(_dsa_sample_select_kernel, nb=nb, tn=tn, n_pages=n_pages, ksel=ksel),
        grid=(bd // nb,),
        in_specs=[pl.BlockSpec((nb, n_pages, tn, PAGE), lambda i: (i, 0, 0, 0)),
                  pl.BlockSpec((nb, tn, PAGE), lambda i: (i, 0, 0))],
        out_specs=pl.BlockSpec((nb, n_pages + 1, tn, PAGE), lambda i: (i, 0, 0, 0)),
        out_shape=jax.ShapeDtypeStruct((bd, n_pages + 1, tn, PAGE), F32),
        compiler_params=_cparams("parallel"),
        name="dsa_sample_select",
    )(scores, scores_new)


def _dsa_sample_kernel(pt_ref, bias_ref, q_ref, kn_ref, vn_ref, *rest, tn, d, n_pages, nsteps, pp):
    del pt_ref
    kp_refs, vp_refs = rest[:pp], rest[pp:2 * pp]
    o_ref, qbd_sc, m_sc, l_sc, acc_sc = rest[2 * pp:]
    step = pl.program_id(1)
    groups = d // HEAD64
    rows = groups * tn

    @pl.when(step == 0)
    def _():
        qbd_sc[...] = _block_diag_queries(q_ref[0].astype(F32), groups, tn)
        m_sc[...] = jnp.full((rows, 1), NEG_BIG, F32)
        l_sc[...] = jnp.zeros((rows, 1), F32)
        acc_sc[...] = jnp.zeros((rows, d), F32)

    def tiled_bias(page):
        return jnp.concatenate([bias_ref[0, page]] * groups, axis=0)

    @pl.when(step < nsteps)
    def _():
        bias = jnp.concatenate([tiled_bias(step * pp + i) for i in range(pp)], axis=1)
        s = jnp.concatenate([_dot(qbd_sc[...], r[...].reshape(d, PAGE).astype(BF16)) for r in kp_refs],
                            axis=1) + bias
        pv = lambda pb: sum(_dot_nt(pb[:, i * PAGE:(i + 1) * PAGE], vp_refs[i][...].reshape(d, PAGE).astype(BF16))
                            for i in range(pp))
        _online_update(s, bias == 0.0, pv, m_sc, l_sc, acc_sc)

    @pl.when(step == nsteps)
    def _():
        bias = tiled_bias(n_pages)
        s = _dot_nt(qbd_sc[...], _pad_rows(kn_ref[0], PAGE).astype(BF16)) + bias
        pv = lambda pb: _dot(pb, _pad_rows(vn_ref[0], PAGE).astype(BF16))
        _online_update(s, bias == 0.0, pv, m_sc, l_sc, acc_sc)
        lo_half = _lane_iota() < HEAD64
        for c in range(d // LANES):
            cols = slice(c * LANES, (c + 1) * LANES)
            r0 = slice(2 * c * tn, (2 * c + 1) * tn)
            r1 = slice((2 * c + 1) * tn, (2 * c + 2) * tn)
            o = jnp.where(lo_half, acc_sc[r0, cols] / l_sc[r0, :], acc_sc[r1, cols] / l_sc[r1, :])
            o_ref[0, :, cols] = o.astype(BF16)


def _dsa_sample(page_table, bias, q3, kn3, vn3, k_pool_t, v_pool_t, layer):
    bd, tn, d = q3.shape
    n_pages = page_table.shape[1]
    groups = d // HEAD64
    rows = groups * tn
    pp = math.gcd(n_pages, SAMPLE_PP)
    nsteps = n_pages // pp
    seq = lambda: pl.BlockSpec((1, tn, d), lambda b, s, pt: (b, 0, 0))
    pool = lambda i: pl.BlockSpec(
        (None, None, groups, HEAD64, PAGE),
        lambda b, s, pt: (layer, pt[b, jnp.minimum(s, nsteps - 1) * pp + i], 0, 0, 0))
    kern = functools.partial(_dsa_sample_kernel, tn=tn, d=d, n_pages=n_pages, nsteps=nsteps, pp=pp)
    return pl.pallas_call(
        kern,
        grid_spec=pltpu.PrefetchScalarGridSpec(
            num_scalar_prefetch=1,
            grid=(bd, nsteps + 1),
            in_specs=[pl.BlockSpec((1, n_pages + 1, tn, PAGE), lambda b, s, pt: (b, 0, 0, 0)),
                      seq(), seq(), seq()] + [pool(i) for i in range(pp)] + [pool(i) for i in range(pp)],
            out_specs=seq(),
            scratch_shapes=[pltpu.VMEM((rows, d), BF16), pltpu.VMEM((rows, 1), F32),
                            pltpu.VMEM((rows, 1), F32), pltpu.VMEM((rows, d), F32)]),
        out_shape=jax.ShapeDtypeStruct((bd, tn, d), BF16),
        compiler_params=_cparams("parallel", "arbitrary"),
        name="dsa_sample",
    )(page_table, bias, q3, kn3, vn3, *([k_pool_t] * pp), *([v_pool_t] * pp))


def _hg_in_kernel(x_ref, g_ref, w_ref, lbw_ref, q_ref, k_ref, lf_ref, v_ref, gs_ref, *, d, layer):
    h = _rms(x_ref[...], g_ref[...]).astype(BF16)
    y = _dot(h, w_ref[...])
    lbw = lbw_ref[...]
    e = jnp.exp(lbw - jnp.max(lbw, axis=0, keepdims=True))
    sm = e / jnp.sum(e, axis=0, keepdims=True)
    lb = jnp.sum(sm[1:layer + 1, :], axis=0, keepdims=True)
    q, fz, v, g = y[:, :d], y[:, d:2 * d], y[:, 2 * d:3 * d], y[:, 3 * d:]
    f = lb + (1.0 - lb) * _sigmoid(fz)
    q_ref[...] = q * _sigmoid(q)
    k_ref[...] = 1.0 - f
    lf_ref[...] = jnp.log(f)
    v_ref[...] = v
    gs_ref[...] = g * _sigmoid(g)


def _hg_in(x, g1, w, lbw, layer):
    r, d = x.shape
    tm = min(r, ROW_TILE)
    row = lambda: pl.BlockSpec((tm, d), lambda i: (i, 0))
    return pl.pallas_call(
        functools.partial(_hg_in_kernel, d=d, layer=layer),
        grid=(r // tm,),
        in_specs=[row(), _const_spec((1, d)), _const_spec((d, 4 * d)), _const_spec(lbw.shape)],
        out_specs=[row()] * 5,
        out_shape=[jax.ShapeDtypeStruct((r, d), F32)] * 5,
        compiler_params=_cparams("parallel"),
        name="hg_in",
    )(x, g1, w, lbw)


def _hg_rec_kernel(q_ref, k_ref, lf_ref, v_ref, gs_ref, s0_ref, gn_ref, o_ref, s_ref, st_sc, *, tb, nt, hp):
    j = pl.program_id(2)

    @pl.when(j == 0)
    def _():
        for hh in range(hp):
            st_sc[hh] = s0_ref[hh].T

    c = HG_C
    r_io = lax.broadcasted_iota(I32, (c, c), 0)
    c_io = lax.broadcasted_iota(I32, (c, c), 1)
    causal = r_io >= c_io
    tri = jnp.where(causal, 1.0, 0.0)
    row = lax.broadcasted_iota(I32, (c, 1), 0)
    nch = max(1, tb // c)
    for ci, hh in [(ci, hh) for ci in range(nch) for hh in range(hp)]:
        cols = slice(hh * LANES, (hh + 1) * LANES)
        if tb >= c:
            rows = slice(ci * c, (ci + 1) * c)
            load = lambda ref: ref[rows, cols]
        else:
            load = lambda ref: _pad_rows(ref[:, cols], c)
        q, k, lf, v = load(q_ref), load(k_ref), load(lf_ref), load(v_ref)
        b = jnp.dot(tri, lf, preferred_element_type=F32, precision=lax.Precision.HIGHEST)
        st = st_sc[hh]
        o = _dot_nt((q * jnp.exp(b)).astype(BF16), st.astype(BF16))
        slabs = []
        for blk in range(c // HG_SUB):
            lo, hi = blk * HG_SUB, (blk + 1) * HG_SUB
            anchor = b[lo - 1:lo, :] if blk > 0 else jnp.zeros((1, LANES), F32)
            qb = q[lo:hi, :] * jnp.exp(b[lo:hi, :] - anchor)
            kb = k * jnp.exp(jnp.where(row < hi, anchor - b, -jnp.inf))
            slabs.append(_dot_nt(qb.astype(BF16), kb.astype(BF16)))
        a = jnp.where(causal, jnp.concatenate(slabs, axis=0), 0.0)
        o = o + _dot(a.astype(BF16), v.astype(BF16))
        b_last = b[c - 1:c, :]
        kd = k * jnp.exp(b_last - b)
        st_sc[hh] = st * jnp.exp(b_last) + _dot(v.T.astype(BF16), kd.astype(BF16))
        og = _rms(o, gn_ref[...])
        if tb >= c:
            o_ref[rows, cols] = (og * gs_ref[rows, cols]).astype(BF16)
        else:
            o_ref[:, cols] = (og[:tb, :] * gs_ref[:, cols]).astype(BF16)

    @pl.when(j == nt - 1)
    def _():
        for hh in range(hp):
            s_ref[hh] = st_sc[hh].T


def _hg_rec(q, k, lf, v, gs, s0, gn, bsz, t):
    r, d = q.shape
    heads = d // LANES
    hp = math.gcd(heads, HG_HP)
    tb = min(t, HG_TB)
    nt = t // tb
    blk = lambda: pl.BlockSpec((tb, hp * LANES), lambda b, h, j: (b * nt + j, h))
    state = lambda: pl.BlockSpec((None, hp, LANES, LANES), lambda b, h, j: (b, h, 0, 0))
    return pl.pallas_call(
        functools.partial(_hg_rec_kernel, tb=tb, nt=nt, hp=hp),
        grid=(bsz, heads // hp, nt),
        in_specs=[blk(), blk(), blk(), blk(), blk(), state(), _const_spec((1, LANES))],
        out_specs=[blk(), state()],
        out_shape=[jax.ShapeDtypeStruct((r, d), BF16),
                   jax.ShapeDtypeStruct((bsz, heads, LANES, LANES), F32)],
        scratch_shapes=[pltpu.VMEM((hp, LANES, LANES), F32)],
        compiler_params=_cparams("parallel", "parallel", "arbitrary"),
        name="hg_rec",
    )(q, k, lf, v, gs, s0, gn)


def kernel(x_prompt, x_sample, state_conv, cache_da_k, cache_da_v, cache_dsa_k, cache_dsa_v, cache_dsa_idx_k, state_hgrn, page_table, norm1_g, norm2_g, final_g, mlp_w1, mlp_w2, cv_w1, cv_b1, cv_dw, cv_dwb, cv_ln_g, cv_ln_b, cv_w2, cv_b2, da_w_in, da_lq1, da_lk1, da_lq2, da_lk2, da_subln_g, da_wo, dsa_w_in, dsa_wo, hg_w_in, hg_lb, hg_norm_g, hg_wo):
    bp, t, d = x_prompt.shape
    bd, tn, _ = x_sample.shape
    n_pages = page_table.shape[1]
    past_len = n_pages * PAGE
    depth = norm1_g.shape[0]
    assert d % LANES == 0 and t % ROW_TILE == 0 and tn & (tn - 1) == 0 and tn <= SUBLANES

    xp = x_prompt.reshape(bp * t, d)
    xs = x_sample.reshape(bd * tn, d)
    cos_p, sin_p = _rope_tables(jnp.arange(t))
    cos_s, sin_s = _rope_tables(past_len + jnp.arange(tn))
    cos_s, sin_s = jnp.tile(cos_s, (bd, 1)), jnp.tile(sin_s, (bd, 1))
    ksel_p = min(DSA_TOPK, t // 4)
    ksel_s = min(DSA_TOPK, (past_len + tn) // 4)
    zero_bias = jnp.zeros((1, d), F32)
    row1 = lambda a: a.reshape(1, -1)
    new = {n: [] for n in ('conv_p', 'conv_s', 'dak_p', 'dav_p', 'dak_s', 'dav_s', 'dsak_p', 'dsav_p', 'dsai_p',
                           'dsak_s', 'dsav_s', 'dsai_s', 'hg_p', 'hg_s')}

    for layer in range(depth):
        kind, j = layer % 4, layer // 4
        g1 = row1(norm1_g[layer])
        bo = zero_bias
        if kind == 0:
            pad = lambda a, n: jnp.concatenate([jnp.zeros(a.shape[:1] + (n,) + a.shape[2:], a.dtype), a], axis=1)
            dw = jnp.concatenate([cv_dw[j], jnp.zeros((CONV_HALO - CONV_W, d), F32)], axis=0)[:, None, :]
            cw = (g1, cv_w1[j].astype(BF16), row1(cv_b1[j]), dw, row1(cv_dwb[j]), row1(cv_ln_g[j]),
                  row1(cv_ln_b[j]))
            halo_pad = CONV_HALO - (CONV_W - 1)
            op, tail_p = _conv_mixer(xp.reshape(bp, t, d), jnp.zeros((bp, CONV_HALO, d), F32), *cw,
                                     nb=1, tt=ROW_TILE)
            os_, tail_s = _conv_mixer(xs.reshape(bd, tn, d), pad(state_conv[j], halo_pad), *cw, nb=bd, tt=tn)
            op, os_ = op.reshape(bp * t, d), os_.reshape(bd * tn, d)
            new['conv_p'].append(tail_p[:, halo_pad:])
            new['conv_s'].append(tail_s[:, halo_pad:])
            wo, bo = cv_w2[j].astype(BF16), row1(cv_b2[j])
        elif kind == 1:
            lam_init = 0.8 - 0.6 * math.exp(-0.3 * layer)
            lams = (row1(da_lq1[j]), row1(da_lk1[j]), row1(da_lq2[j]), row1(da_lk2[j]))
            subg = row1(da_subln_g[j])
            w_in = da_w_in[j].astype(BF16)
            qp, kp, vp, kbp, vbp = _da_in(xp, g1, w_in, cos_p, sin_p)
            qs, ks, vs, _, _ = _da_in(xs, g1, w_in, cos_s, sin_s)
            op = _da_prompt(qp, kbp, vbp, lams, subg, bp, t, lam_init)
            os_ = _da_sample(page_table, qs.reshape(bd, tn, d), ks.reshape(bd, tn, d), vs.reshape(bd, tn, d),
                             cache_da_k, cache_da_v, lams, subg, j, lam_init).reshape(bd * tn, d)
            hd = (d // LANES, LANES)
            new['dak_p'].append(kp.reshape((bp, t) + hd))
            new['dav_p'].append(vp.reshape((bp, t) + hd))
            new['dak_s'].append(ks.reshape((bd, tn) + hd))
            new['dav_s'].append(vs.reshape((bd, tn) + hd))
            wo = da_wo[j].astype(BF16)
        elif kind == 2:
            w_in = dsa_w_in[j]
            w_in = jnp.concatenate([w_in, jnp.zeros((d, -w_in.shape[1] % LANES), F32)], axis=1).astype(BF16)
            qp, kp, vp, kbp, vbp, qip, mp_ = _dsa_in(xp, g1, w_in, cos_p, sin_p, seq_len=t)
            qs, ks, vs, _, _, qis, ms_ = _dsa_in(xs, g1, w_in, cos_s, sin_s)
            op = _dsa_prompt(qip, mp_, qp, kbp, vbp, bp, t, ksel_p)
            scores, scores_new = _dsa_sample_scores(
                page_table, qis.reshape(bd, tn, -1), ms_.reshape(bd, tn, LANES),
                jnp.transpose(cache_dsa_idx_k, (0, 1, 3, 2)), j)
            bias = _dsa_sample_select(scores, scores_new, ksel_s)
            os_ = _dsa_sample(page_table, bias, qs.reshape(bd, tn, d), ks.reshape(bd, tn, d),
                              vs.reshape(bd, tn, d), jnp.transpose(cache_dsa_k, (0, 1, 3, 4, 2)),
                              jnp.transpose(cache_dsa_v, (0, 1, 3, 4, 2)), j).reshape(bd * tn, d)
            hd = (d // HEAD64, HEAD64)
            new['dsak_p'].append(jnp.transpose(kp.reshape((bp,) + hd + (t,)), (0, 3, 1, 2)))
            new['dsav_p'].append(jnp.transpose(vp.reshape((bp,) + hd + (t,)), (0, 3, 1, 2)))
            new['dsai_p'].append(mp_[:, :HEAD64].reshape(bp, t, HEAD64))
            new['dsak_s'].append(ks.reshape((bd, tn) + hd))
            new['dsav_s'].append(vs.reshape((bd, tn) + hd))
            new['dsai_s'].append(ms_[:, :HEAD64].reshape(bd, tn, HEAD64))
            wo = dsa_wo[j].astype(BF16)
        else:
            w_in = hg_w_in[j].astype(BF16)
            gn = row1(hg_norm_g[j])
            heads = d // LANES
            hp = _hg_in(xp, g1, w_in, hg_lb, layer)
            hs = _hg_in(xs, g1, w_in, hg_lb, layer)
            op, sp = _hg_rec(*hp, jnp.zeros((bp, heads, LANES, LANES), F32), gn, bp, t)
            os_, ss = _hg_rec(*hs, state_hgrn[j], gn, bd, tn)
            new['hg_p'].append(sp.astype(state_hgrn.dtype))
            new['hg_s'].append(ss.astype(state_hgrn.dtype))
            wo = hg_wo[j].astype(BF16)

        final = layer == depth - 1
        post = (wo, bo, row1(norm2_g[layer]), mlp_w1[layer].astype(BF16), mlp_w2[layer].astype(BF16),
                row1(final_g))
        xp = _post(xp, op, *post, final=final)
        xs = _post(xs, os_, *post, final=final)

    return (xp.reshape(bp, t, d), xs.reshape(bd, tn, d),
            jnp.stack(new['conv_p']), jnp.stack(new['conv_s']),
            jnp.stack(new['dak_p']), jnp.stack(new['dav_p']), jnp.stack(new['dak_s']), jnp.stack(new['dav_s']),
            jnp.stack(new['dsak_p']), jnp.stack(new['dsav_p']), jnp.stack(new['dsai_p']),
            jnp.stack(new['dsak_s']), jnp.stack(new['dsav_s']), jnp.stack(new['dsai_s']),
            jnp.stack(new['hg_p']), jnp.stack(new['hg_s']))
```

```python
import functools
import math

import jax
import jax.numpy as jnp
from jax import lax
from jax.experimental import pallas as pl
from jax.experimental.pallas import tpu as pltpu

F32 = jnp.float32
BF16 = jnp.bfloat16
I32 = jnp.int32

NORM_EPS = 1e-6
LN_EPS = 1e-5
ROPE_THETA = 10000.0
ROPE_DIM = 64
CONV_W = 31
PAGE = 128
HEAD64 = 64
IDX_HEADS = 8
DSA_TOPK = 256

LANES = 128
SUBLANES = 8
CONV_HALO = 32
CONV_ROWS = 32
ROW_TILE = 512
CONV_TT = 256
ATT_TQ = 256
DSA_TQ = 128
DSA_KV_GROUPS = 4
SAMPLE_PP = 8
DSA_SAMPLE_PP = 16
SCORE_PP = 8
HG_C = 128
HG_SUB = 16
HG_TB = 512
HG_HP = 4
SELECT_NB = 8
MLP_FCHUNK = 1024
VMEM_LIMIT = 56 * 1024 * 1024
NEG_BIG = -1e30
INT_MIN = -2 ** 31


def _cparams(*sem):
    return pltpu.CompilerParams(dimension_semantics=sem, vmem_limit_bytes=VMEM_LIMIT)


def _const_spec(shape):
    nd = len(shape)
    return pl.BlockSpec(shape, lambda *_: (0,) * nd, pipeline_mode=pl.Buffered(1))


def _dot(a, b):
    return jnp.dot(a, b, preferred_element_type=F32)


def _dot_nt(a, b):
    return lax.dot_general(a, b, (((1,), (1,)), ((), ())), preferred_element_type=F32)


def _rms(x, g):
    return x * lax.rsqrt(jnp.mean(x * x, axis=-1, keepdims=True) + NORM_EPS) * g


def _sigmoid(x):
    return 1.0 / (1.0 + jnp.exp(-x))


def _lane_iota(shape=(1, LANES)):
    return lax.broadcasted_iota(I32, shape, len(shape) - 1)


def _rope(xc, cos, sin_signed):
    first_half = (_lane_iota() & 32) == 0
    partner = jnp.where(first_half, pltpu.roll(xc, 96, 1), pltpu.roll(xc, 32, 1))
    return xc * cos + partner * sin_signed


def _rope_tables(pos):
    inv = ROPE_THETA ** (-jnp.arange(0, ROPE_DIM, 2, dtype=F32) / ROPE_DIM)
    ang = pos.astype(F32)[:, None] * inv[None, :]
    cos, sin = jnp.cos(ang), jnp.sin(ang)
    return jnp.tile(cos, (1, 4)), jnp.tile(jnp.concatenate([-sin, sin], axis=1), (1, 2))


def _pad_rows(x, rows):
    return jnp.concatenate([x, jnp.zeros((rows - x.shape[0], x.shape[1]), x.dtype)], axis=0)


def _post_kernel(x_ref, o_ref, wo_ref, bo_ref, g_ref, w1_ref, w2_ref, fg_ref, y_ref, h_ref, *, final):
    x1 = x_ref[...] + _dot(o_ref[...], wo_ref[...]) + bo_ref[...]
    h = _rms(x1, g_ref[...]).astype(BF16)
    f = w1_ref.shape[1]
    fc = min(f, MLP_FCHUNK)
    for c in range(f // fc):
        a = jnp.maximum(_dot(h, w1_ref[:, c * fc:(c + 1) * fc]), 0.0)
        h_ref[:, c * fc:(c + 1) * fc] = (a * a).astype(BF16)
    y = x1 + _dot(h_ref[...], w2_ref[...])
    if final:
        y = _rms(y, fg_ref[...])
    y_ref[...] = y


def _post(x, o, wo, bo, g2, w1, w2, fg, final):
    r, d = x.shape
    f = w1.shape[1]
    tm = min(r, ROW_TILE)
    row = lambda w: pl.BlockSpec((tm, w), lambda i: (i, 0))
    return pl.pallas_call(
        functools.partial(_post_kernel, final=final),
        grid=(r // tm,),
        in_specs=[row(d), row(d), _const_spec((d, d)), _const_spec((1, d)), _const_spec((1, d)),
                  _const_spec((d, f)), _const_spec((f, d)), _const_spec((1, d))],
        out_specs=row(d),
        out_shape=jax.ShapeDtypeStruct((r, d), F32),
        scratch_shapes=[pltpu.VMEM((tm, f), BF16)],
        compiler_params=_cparams("parallel"),
        name="post_mlp",
    )(x, o, wo, bo, g2, w1, w2, fg)


def _conv_kernel(x_ref, buf_ref, g_ref, w1_ref, b1_ref, dw_ref, dwb_ref, lng_ref, lnb_ref,
                 o_ref, tail_ref, ext_ref, y_ref, sh_ref, *, nb, tt, d, carry, preshift):
    @pl.when(pl.program_id(1) == 0)
    def _():
        ext_ref[:, 0:CONV_HALO, :] = buf_ref[...]

    x = x_ref[...].reshape(nb * tt, d)
    h = _rms(x, g_ref[...]).astype(BF16)
    ag = _dot(h, w1_ref[...]) + b1_ref[...]
    u = ag[:, :d] * _sigmoid(ag[:, d:])
    ext_ref[:, CONV_HALO:CONV_HALO + tt, :] = u.reshape(nb, tt, d)

    if preshift:
        for a in range(1, SUBLANES):
            sh_ref[a - 1] = ext_ref[0, a:a + sh_ref.shape[1], :]

    rs = min(tt, CONV_ROWS)
    for r in range(tt // rs):
        acc = jnp.zeros((nb, rs, d), F32)
        for k in range(CONV_W):
            off = k + CONV_HALO - (CONV_W - 1)
            a = off % SUBLANES
            if preshift and a:
                start = r * rs + off - a
                window = sh_ref[a - 1, start:start + rs, :][None]
            else:
                window = ext_ref[:, r * rs + off:r * rs + off + rs, :]
            acc = acc + dw_ref[k] * window
        y_ref[:, r * rs:(r + 1) * rs, :] = acc + dwb_ref[...]

    y = y_ref[...]
    mu = jnp.mean(y, axis=-1, keepdims=True)
    yc = y - mu
    var = jnp.mean(yc * yc, axis=-1, keepdims=True)
    z = yc * lax.rsqrt(var + LN_EPS) * lng_ref[...] + lnb_ref[...]
    o_ref[...] = (z * _sigmoid(z)).astype(BF16)
    tail = ext_ref[:, tt:tt + CONV_HALO, :]
    tail_ref[...] = tail
    if carry:
        ext_ref[:, 0:CONV_HALO, :] = tail


def _conv_mixer(x3, buf, g1, w1, b1, dw, dwb, lng, lnb, nb, tt):
    b, t, d = x3.shape
    nt = t // tt
    preshift = nb == 1 and tt > CONV_ROWS
    sh_shape = (SUBLANES - 1, tt + CONV_HALO - SUBLANES, d) if preshift else (1, SUBLANES, LANES)
    kern = functools.partial(_conv_kernel, nb=nb, tt=tt, d=d, carry=nt > 1, preshift=preshift)
    return pl.pallas_call(
        kern,
        grid=(b // nb, nt),
        in_specs=[pl.BlockSpec((nb, tt, d), lambda i, j: (i, j, 0)),
                  pl.BlockSpec((nb, CONV_HALO, d), lambda i, j: (i, 0, 0)),
                  _const_spec((1, d)), _const_spec((d, 2 * d)), _const_spec((1, 2 * d)),
                  _const_spec((CONV_HALO, 1, d)), _const_spec((1, d)), _const_spec((1, d)),
                  _const_spec((1, d))],
        out_specs=[pl.BlockSpec((nb, tt, d), lambda i, j: (i, j, 0)),
                   pl.BlockSpec((nb, CONV_HALO, d), lambda i, j: (i, 0, 0))],
        out_shape=[jax.ShapeDtypeStruct((b, t, d), BF16),
                   jax.ShapeDtypeStruct((b, CONV_HALO, d), F32)],
        scratch_shapes=[pltpu.VMEM((nb, CONV_HALO + tt, d), F32), pltpu.VMEM((nb, tt, d), F32),
                        pltpu.VMEM(sh_shape, F32)],
        compiler_params=_cparams("parallel", "arbitrary"),
        name="conv_mixer",
    )(x3, buf, g1, w1, b1, dw, dwb, lng, lnb)


def _da_in_kernel(x_ref, g_ref, w_ref, cos_ref, sin_ref, q_ref, k_ref, v_ref, kb_ref, vb_ref, *, d):
    h = _rms(x_ref[...], g_ref[...]).astype(BF16)
    y = _dot(h, w_ref[...])
    cos, sin = cos_ref[...], sin_ref[...]
    for c in range(d // LANES):
        sl = slice(c * LANES, (c + 1) * LANES)
        q = _rope(y[:, c * LANES:(c + 1) * LANES], cos, sin)
        q_ref[:, sl] = (q * HEAD64 ** -0.5).astype(BF16)
        k = _rope(y[:, d + c * LANES:d + (c + 1) * LANES], cos, sin)
        k_ref[:, sl] = k
        kb_ref[:, sl] = k.astype(BF16)
        v = y[:, 2 * d + c * LANES:2 * d + (c + 1) * LANES]
        v_ref[:, sl] = v
        vb_ref[:, sl] = v.astype(BF16)


def _da_in(x, g1, w, cos, sin):
    r, d = x.shape
    tm = min(r, ROW_TILE)
    ntab = cos.shape[0] // tm
    row = lambda: pl.BlockSpec((tm, d), lambda i: (i, 0))
    tab = lambda: pl.BlockSpec((tm, LANES), lambda i: (i % ntab, 0))
    return pl.pallas_call(
        functools.partial(_da_in_kernel, d=d),
        grid=(r // tm,),
        in_specs=[row(), _const_spec((1, d)), _const_spec((d, 3 * d)), tab(), tab()],
        out_specs=[row(), row(), row(), row(), row()],
        out_shape=[jax.ShapeDtypeStruct((r, d), BF16), jax.ShapeDtypeStruct((r, d), F32),
                   jax.ShapeDtypeStruct((r, d), F32), jax.ShapeDtypeStruct((r, d), BF16),
                   jax.ShapeDtypeStruct((r, d), BF16)],
        compiler_params=_cparams("parallel"),
        name="da_in",
    )(x, g1, w, cos, sin)


def _da_lambda(lq1, lk1, lq2, lk2, lam_init):
    return (jnp.exp(jnp.sum(lq1[...] * lk1[...], axis=-1, keepdims=True))
            - jnp.exp(jnp.sum(lq2[...] * lk2[...], axis=-1, keepdims=True)) + lam_init)


def _da_prompt_kernel(lq1, lk1, lq2, lk2, subg_ref, q_ref, k_ref, v_ref, o_ref, *, tq, nq, lam_init):
    i = pl.program_id(2)
    lam = _da_lambda(lq1, lk1, lq2, lk2, lam_init)
    q = q_ref[...]
    lane = _lane_iota()
    zero = jnp.zeros_like(q)
    q1 = jnp.where(lane < HEAD64, q, zero)
    q2 = jnp.where(lane >= HEAD64, q, zero)
    keep = (lax.broadcasted_iota(I32, (tq, tq), 0) >= lax.broadcasted_iota(I32, (tq, tq), 1))

    def body(c):
        off = c * tq

        def softmax_parts(qm):
            sd = jnp.where(keep, _dot_nt(qm, k_ref[off:off + tq, :]), NEG_BIG)
            m = jnp.max(sd, axis=-1, keepdims=True)
            so = None
            if c > 0:
                so = _dot_nt(qm, k_ref[0:off, :])
                m = jnp.maximum(m, jnp.max(so, axis=-1, keepdims=True))
            pd = jnp.exp(sd - m)
            l = jnp.sum(pd, axis=-1, keepdims=True)
            po = None
            if c > 0:
                po = jnp.exp(so - m)
                l = l + jnp.sum(po, axis=-1, keepdims=True)
            return pd, po, l

        pd1, po1, l1 = softmax_parts(q1)
        pd2, po2, l2 = softmax_parts(q2)
        w1 = 1.0 / l1
        w2 = lam / l2
        o = _dot((pd1 * w1 - pd2 * w2).astype(BF16), v_ref[off:off + tq, :])
        if c > 0:
            o = o + _dot((po1 * w1 - po2 * w2).astype(BF16), v_ref[0:off, :])
        o_ref[...] = (_rms(o, subg_ref[...]) * (1.0 - lam_init)).astype(BF16)

    for c in range(nq):
        pl.when(i == c)(functools.partial(body, c))


def _da_prompt(q, kb, vb, lams, subg, bsz, t, lam_init):
    r, d = q.shape
    tq = min(t, ATT_TQ)
    nq = t // tq
    lam_spec = _const_spec((1, HEAD64))
    return pl.pallas_call(
        functools.partial(_da_prompt_kernel, tq=tq, nq=nq, lam_init=lam_init),
        grid=(bsz, d // LANES, nq),
        in_specs=[lam_spec, lam_spec, lam_spec, lam_spec, _const_spec((1, LANES)),
                  pl.BlockSpec((tq, LANES), lambda b, h, i: (b * nq + i, h)),
                  pl.BlockSpec((t, LANES), lambda b, h, i: (b, h)),
                  pl.BlockSpec((t, LANES), lambda b, h, i: (b, h))],
        out_specs=pl.BlockSpec((tq, LANES), lambda b, h, i: (b * nq + i, h)),
        out_shape=jax.ShapeDtypeStruct((r, d), BF16),
        compiler_params=_cparams("parallel", "parallel", "arbitrary"),
        name="da_prompt",
    )(*lams, subg, q, kb, vb)


def _online_update(s, valid, pv_fn, m_sc, l_sc, acc_sc):
    m_old = m_sc[...]
    m_new = jnp.maximum(m_old, jnp.max(s, axis=-1, keepdims=True))
    alpha = jnp.exp(m_old - m_new)
    p = jnp.exp(s - m_new)
    if valid is not None:
        p = jnp.where(valid, p, 0.0)
    l_sc[...] = alpha * l_sc[...] + jnp.sum(p, axis=-1, keepdims=True)
    acc_sc[...] = alpha * acc_sc[...] + pv_fn(p.astype(BF16))
    m_sc[...] = m_new


def _da_sample_kernel(pt_ref, lq1, lk1, lq2, lk2, subg_ref, q_ref, kn_ref, vn_ref, *rest,
                      tn, d, lam_init, nsteps, pp):
    del pt_ref
    kp_refs, vp_refs = rest[:pp], rest[pp:2 * pp]
    o_ref, qh_sc, own_sc, m_sc, l_sc, acc_sc = rest[2 * pp:]
    step = pl.program_id(1)
    heads = d // LANES
    gr = 2 * tn
    rows = heads * gr
    ncol = pp * PAGE * heads

    @pl.when(step == 0)
    def _():
        q = q_ref[0].astype(F32)
        first_map = lax.broadcasted_iota(I32, (gr, LANES), 0) < tn
        lo_half = _lane_iota((gr, LANES)) < HEAD64
        for h in range(heads):
            qc = q[:, h * LANES:(h + 1) * LANES]
            qh_sc[h * gr:(h + 1) * gr, :] = jnp.where(
                first_map == lo_half, jnp.concatenate([qc, qc], axis=0), 0.0).astype(BF16)
        row_head = lax.broadcasted_iota(I32, (rows, ncol), 0) >> int(math.log2(gr))
        col_head = lax.broadcasted_iota(I32, (rows, ncol), 1) & (heads - 1)
        own_sc[...] = jnp.where(row_head == col_head, 0.0, NEG_BIG)
        m_sc[...] = jnp.full((rows, 1), NEG_BIG, F32)
        l_sc[...] = jnp.zeros((rows, 1), F32)
        acc_sc[...] = jnp.zeros((rows, LANES), F32)

    flat = lambda refs: jnp.concatenate([r[...] for r in refs], axis=0).astype(BF16)
    s = _dot_nt(qh_sc[...], flat(kp_refs)) + own_sc[...]
    _online_update(s, None, lambda pb: _dot(pb, flat(vp_refs)), m_sc, l_sc, acc_sc)

    @pl.when(step == nsteps - 1)
    def _():
        fresh = lambda ref, h: _pad_rows(ref[0][:, h * LANES:(h + 1) * LANES], PAGE).astype(BF16)
        qpos = lax.broadcasted_iota(I32, (rows, PAGE), 0) & (tn - 1)
        valid = lax.broadcasted_iota(I32, (rows, PAGE), 1) <= qpos
        s_new = jnp.concatenate(
            [_dot_nt(qh_sc[h * gr:(h + 1) * gr, :], fresh(kn_ref, h)) for h in range(heads)], axis=0)
        pv = lambda pb: jnp.concatenate(
            [_dot(pb[h * gr:(h + 1) * gr, :], fresh(vn_ref, h)) for h in range(heads)], axis=0)
        _online_update(jnp.where(valid, s_new, NEG_BIG), valid, pv, m_sc, l_sc, acc_sc)
        lam = _da_lambda(lq1, lk1, lq2, lk2, lam_init)
        for h in range(heads):
            r1 = slice(h * gr, h * gr + tn)
            r2 = slice(h * gr + tn, (h + 1) * gr)
            o = acc_sc[r1, :] / l_sc[r1, :] - lam * (acc_sc[r2, :] / l_sc[r2, :])
            o_ref[0, :, h * LANES:(h + 1) * LANES] = (_rms(o, subg_ref[...]) * (1.0 - lam_init)).astype(BF16)


def _da_sample(page_table, q3, kn3, vn3, k_pool, v_pool, lams, subg, layer, lam_init):
    bd, tn, d = q3.shape
    n_pages = page_table.shape[1]
    heads = d // LANES
    pp = math.gcd(n_pages, SAMPLE_PP)
    nsteps = n_pages // pp
    rows = heads * 2 * tn
    assert heads & (heads - 1) == 0
    flat_shape = k_pool.shape[:2] + (PAGE * heads, LANES)
    k_pool, v_pool = k_pool.reshape(flat_shape), v_pool.reshape(flat_shape)
    seq = lambda: pl.BlockSpec((1, tn, d), lambda b, s, pt: (b, 0, 0))
    pool = lambda i: pl.BlockSpec((None, None, PAGE * heads, LANES),
                                  lambda b, s, pt: (layer, pt[b, s * pp + i], 0, 0))
    lam_spec = _const_spec((1, HEAD64))
    kern = functools.partial(_da_sample_kernel, tn=tn, d=d, lam_init=lam_init, nsteps=nsteps, pp=pp)
    return pl.pallas_call(
        kern,
        grid_spec=pltpu.PrefetchScalarGridSpec(
            num_scalar_prefetch=1,
            grid=(bd, nsteps),
            in_specs=[lam_spec, lam_spec, lam_spec, lam_spec, _const_spec((1, LANES)), seq(), seq(), seq()]
            + [pool(i) for i in range(pp)] + [pool(i) for i in range(pp)],
            out_specs=seq(),
            scratch_shapes=[pltpu.VMEM((rows, LANES), BF16), pltpu.VMEM((rows, pp * PAGE * heads), F32),
                            pltpu.VMEM((rows, 1), F32), pltpu.VMEM((rows, 1), F32),
                            pltpu.VMEM((rows, LANES), F32)]),
        out_shape=jax.ShapeDtypeStruct((bd, tn, d), BF16),
        compiler_params=_cparams("parallel", "arbitrary"),
        name="da_sample",
    )(page_table, *lams, subg, q3, kn3, vn3, *([k_pool] * pp), *([v_pool] * pp))


def _dsa_in_kernel(x_ref, g_ref, w_ref, cos_ref, sin_ref, q_ref, k_ref, v_ref, kb_ref, vb_ref,
                   qi_ref, misc_ref, *, d, token_minor):
    h = _rms(x_ref[...], g_ref[...]).astype(BF16)
    y = _dot(h, w_ref[...])
    cos, sin = cos_ref[...], sin_ref[...]
    for c in range(d // LANES):
        sl = slice(c * LANES, (c + 1) * LANES)
        q = _rope(y[:, c * LANES:(c + 1) * LANES], cos, sin)
        q_ref[:, sl] = (q * HEAD64 ** -0.5).astype(BF16)
        k = _rope(y[:, d + c * LANES:d + (c + 1) * LANES], cos, sin)
        kb_ref[:, sl] = k.astype(BF16)
        v = y[:, 2 * d + c * LANES:2 * d + (c + 1) * LANES]
        vb_ref[:, sl] = v.astype(BF16)
        if token_minor:
            k_ref[sl, :] = k.T
            v_ref[sl, :] = v.T
        else:
            k_ref[:, sl] = k
            v_ref[:, sl] = v
    for c in range(IDX_HEADS * HEAD64 // LANES):
        qi = _rope(y[:, 3 * d + c * LANES:3 * d + (c + 1) * LANES], cos, sin)
        qi_ref[:, c * LANES:(c + 1) * LANES] = (qi * HEAD64 ** -0.5).astype(BF16)
    is_key = _lane_iota() < HEAD64
    base = 3 * d + IDX_HEADS * HEAD64
    misc_ref[...] = _rope(y[:, base:base + LANES], jnp.where(is_key, cos, 1.0), jnp.where(is_key, sin, 0.0))


def _dsa_in(x, g1, w, cos, sin, seq_len=None):
    r, d = x.shape
    tm = min(r, ROW_TILE)
    ntab = cos.shape[0] // tm
    nqi = IDX_HEADS * HEAD64
    row = lambda w_: pl.BlockSpec((tm, w_), lambda i: (i, 0))
    tab = lambda: pl.BlockSpec((tm, LANES), lambda i: (i % ntab, 0))
    if seq_len is None:
        kv_spec, kv_shape = row(d), jax.ShapeDtypeStruct((r, d), F32)
    else:
        nt = seq_len // tm
        kv_spec = pl.BlockSpec((None, d, tm), lambda i: (i // nt, 0, i % nt))
        kv_shape = jax.ShapeDtypeStruct((r // seq_len, d, seq_len), F32)
    return pl.pallas_call(
        functools.partial(_dsa_in_kernel, d=d, token_minor=seq_len is not None),
        grid=(r // tm,),
        in_specs=[row(d), _const_spec((1, d)), _const_spec(w.shape), tab(), tab()],
        out_specs=[row(d), kv_spec, kv_spec, row(d), row(d), row(nqi), row(LANES)],
        out_shape=[jax.ShapeDtypeStruct((r, d), BF16), kv_shape, kv_shape,
                   jax.ShapeDtypeStruct((r, d), BF16),
                   jax.ShapeDtypeStruct((r, d), BF16), jax.ShapeDtypeStruct((r, nqi), BF16),
                   jax.ShapeDtypeStruct((r, LANES), F32)],
        compiler_params=_cparams("parallel"),
        name="dsa_in",
    )(x, g1, w, cos, sin)


def _sortable_key(score):
    bits = lax.bitcast_convert_type(score + 0.0, I32)
    return bits ^ ((bits >> 31) & 0x7FFFFFFF)


def _select_topk(key, pos, krow, count, pos_bits):
    def try_bit(ans, bit):
        cand = ans | lax.shift_left(jnp.int32(1), bit)
        return jnp.where(count(key >= cand) >= krow, cand, ans)

    def two_bit_step(it, ans):
        lo = lax.shift_left(jnp.int32(1), 28 - 2 * it)
        c1, c2, c3 = ans | lo, ans | (lo + lo), ans | (lo + lo) | lo
        n1, n2, n3 = count(key >= c1), count(key >= c2), count(key >= c3)
        return jnp.where(n3 >= krow, c3, jnp.where(n2 >= krow, c2, jnp.where(n1 >= krow, c1, ans)))

    ans = jnp.where(count(key >= 0) >= krow, jnp.int32(0), jnp.int32(INT_MIN))
    ans = try_bit(ans, 30)
    ans = lax.fori_loop(0, 15, two_bit_step, ans)
    gt = key > ans
    eq = key == ans
    need = krow - count(gt)

    def pos_step(it, cut):
        cand = cut | lax.shift_left(jnp.int32(1), pos_bits - 1 - it)
        return jnp.where(count(eq & (pos < cand)) < need, cand, cut)

    surplus = jnp.max(count(eq) - need) > 0.0
    cut = lax.cond(surplus,
                   lambda: lax.fori_loop(0, pos_bits, pos_step, jnp.zeros(krow.shape, I32)),
                   lambda: jnp.full(krow.shape, 2 ** pos_bits, I32))
    return gt | (eq & (pos <= cut))


def _dsa_prompt_kernel(qi_ref, wq_ref, kim_ref, q_ref, k_ref, v_ref, o_ref, kid_sc, *, tq, nq, d, ksel):
    i = pl.program_id(1)
    lane = _lane_iota()
    lo_half = lane < HEAD64

    @pl.when(i == 0)
    def _():
        lo = jnp.where(lo_half, kim_ref[...], 0.0)
        kid_sc[...] = (lo + pltpu.roll(lo, HEAD64, 1)).astype(BF16)

    def body(kv):
        wt = wq_ref[...] * IDX_HEADS ** -0.5
        kid = kid_sc[0:kv, :]
        score = jnp.zeros((tq, kv), F32)
        for h in range(IDX_HEADS):
            qc = qi_ref[:, (h // 2) * LANES:(h // 2 + 1) * LANES]
            qp = jnp.where(lo_half if h % 2 == 0 else ~lo_half, qc, jnp.zeros_like(qc))
            dots = _dot_nt(qp, kid)
            score = score + wt[:, HEAD64 + h:HEAD64 + h + 1] * jnp.maximum(dots, 0.0)

        qpos = i * tq + lax.broadcasted_iota(I32, (tq, 1), 0)
        kpos = lax.broadcasted_iota(I32, (tq, kv), 1)
        key = jnp.where(kpos <= qpos, _sortable_key(score), INT_MIN)
        krow = jnp.minimum(qpos + 1, ksel).astype(F32)
        count = lambda mask: jnp.sum(jnp.where(mask, 1.0, 0.0), axis=-1, keepdims=True)
        sel = _select_topk(key, kpos, krow, count, max(1, (kv - 1).bit_length()))
        bias = jnp.where(sel, 0.0, NEG_BIG)
        bias2 = jnp.concatenate([bias, bias], axis=0)

        for c in range(d // LANES):
            cols = slice(c * LANES, (c + 1) * LANES)
            qc = q_ref[:, cols]
            kc = k_ref[0:kv, cols]
            vc = v_ref[0:kv, cols]
            zero = jnp.zeros_like(qc)
            qq = jnp.concatenate([jnp.where(lo_half, qc, zero), jnp.where(lo_half, zero, qc)], axis=0)
            s = _dot_nt(qq, kc) + bias2
            p = jnp.exp(s - jnp.max(s, axis=-1, keepdims=True))
            l = jnp.sum(p, axis=-1, keepdims=True)
            o2 = _dot(p.astype(BF16), vc) / l
            o_ref[:, cols] = jnp.where(lo_half, o2[:tq, :], o2[tq:, :]).astype(BF16)

    groups = math.gcd(nq, DSA_KV_GROUPS)
    per = nq // groups
    for g in range(groups):
        pl.when((i >= g * per) & (i < (g + 1) * per))(functools.partial(body, (g + 1) * per * tq))


def _dsa_prompt(qi, misc, q, kb, vb, bsz, t, ksel):
    r, d = q.shape
    tq = min(t, DSA_TQ)
    nq = t // tq
    nqi = qi.shape[1]
    qrow = lambda w: pl.BlockSpec((tq, w), lambda b, i: (b * nq + i, 0))
    seq = lambda w: pl.BlockSpec((t, w), lambda b, i: (b, 0))
    return pl.pallas_call(
        functools.partial(_dsa_prompt_kernel, tq=tq, nq=nq, d=d, ksel=ksel),
        grid=(bsz, nq),
        in_specs=[qrow(nqi), qrow(LANES), seq(LANES), qrow(d), seq(d), seq(d)],
        out_specs=qrow(d),
        out_shape=jax.ShapeDtypeStruct((r, d), BF16),
        scratch_shapes=[pltpu.VMEM((t, LANES), BF16)],
        compiler_params=_cparams("parallel", "arbitrary"),
        name="dsa_prompt",
    )(qi, misc, misc, q, kb, vb)


def _dsa_sample_scores_kernel(pt_ref, qi_ref, misc_ref, *rest, tn, nsteps, pp):
    del pt_ref
    pool_refs = rest[:pp]
    sc_ref, scn_ref, qst_sc, w_sc = rest[pp:]
    step = pl.program_id(1)

    @pl.when(step == 0)
    def _():
        qi = qi_ref[0].astype(F32)
        misc = misc_ref[0]
        qst_sc[...] = jnp.concatenate(
            [qi[:, h * HEAD64:(h + 1) * HEAD64] for h in range(IDX_HEADS)], axis=0).astype(BF16)
        w_sc[...] = jnp.concatenate(
            [misc[:, HEAD64 + h:HEAD64 + h + 1] for h in range(IDX_HEADS)], axis=0) * IDX_HEADS ** -0.5

    def head_sum(dots):
        sc = w_sc[...] * jnp.maximum(dots, 0.0)
        out = sc[0:tn, :]
        for h in range(1, IDX_HEADS):
            out = out + sc[h * tn:(h + 1) * tn, :]
        return out

    for i in range(pp):
        sc_ref[0, i] = head_sum(_dot(qst_sc[...], pool_refs[i][...].astype(BF16)))

    @pl.when(step == nsteps - 1)
    def _():
        fresh = _pad_rows(misc_ref[0][:, :HEAD64], PAGE).astype(BF16)
        scn_ref[0] = head_sum(_dot_nt(qst_sc[...], fresh))


def _dsa_sample_scores(page_table, qi3, misc3, idx_pool_t, layer):
    bd, tn, nqi = qi3.shape
    n_pages = page_table.shape[1]
    pp = math.gcd(n_pages, SCORE_PP)
    nsteps = n_pages // pp
    pool = lambda i: pl.BlockSpec((None, None, HEAD64, PAGE),
                                  lambda b, s, pt: (layer, pt[b, s * pp + i], 0, 0))
    kern = functools.partial(_dsa_sample_scores_kernel, tn=tn, nsteps=nsteps, pp=pp)
    return pl.pallas_call(
        kern,
        grid_spec=pltpu.PrefetchScalarGridSpec(
            num_scalar_prefetch=1,
            grid=(bd, nsteps),
            in_specs=[pl.BlockSpec((1, tn, nqi), lambda b, s, pt: (b, 0, 0)),
                      pl.BlockSpec((1, tn, LANES), lambda b, s, pt: (b, 0, 0))]
            + [pool(i) for i in range(pp)],
            out_specs=[pl.BlockSpec((1, pp, tn, PAGE), lambda b, s, pt: (b, s, 0, 0)),
                       pl.BlockSpec((1, tn, PAGE), lambda b, s, pt: (b, 0, 0))],
            scratch_shapes=[pltpu.VMEM((IDX_HEADS * tn, HEAD64), BF16),
                            pltpu.VMEM((IDX_HEADS * tn, 1), F32)]),
        out_shape=[jax.ShapeDtypeStruct((bd, n_pages, tn, PAGE), F32),
                   jax.ShapeDtypeStruct((bd, tn, PAGE), F32)],
        compiler_params=_cparams("parallel", "arbitrary"),
        name="dsa_sample_scores",
    )(page_table, qi3, misc3, *([idx_pool_t] * pp))


def _block_diag_queries(q, groups, tn):
    d = q.shape[1]
    qt = jnp.concatenate([q] * groups, axis=0)
    shift = int(math.log2(tn))
    rg = lax.broadcasted_iota(I32, (groups * tn, d), 0) >> shift
    cg = lax.broadcasted_iota(I32, (groups * tn, d), 1) >> 6
    return jnp.where(rg == cg, qt, 0.0).astype(BF16)


def _dsa_sample_select_kernel(sc_ref, scn_ref, bias_ref, *, nb, tn, n_pages, ksel):
    shape = (nb, n_pages + 1, tn, PAGE)
    page = lax.broadcasted_iota(I32, shape, 1)
    qidx = lax.broadcasted_iota(I32, shape, 2)
    lane = lax.broadcasted_iota(I32, shape, 3)
    score = jnp.concatenate([sc_ref[...], scn_ref[...].reshape(nb, 1, tn, PAGE)], axis=1)
    key = jnp.where((page < n_pages) | (lane <= qidx), _sortable_key(score), INT_MIN)
    krow = jnp.full((nb, 1, tn, 1), float(ksel), F32)
    count = lambda mask: jnp.sum(jnp.sum(jnp.where(mask, 1.0, 0.0), axis=1, keepdims=True),
                                 axis=-1, keepdims=True)
    sel = _select_topk(key, page * PAGE + lane, krow, count, ((n_pages + 1) * PAGE - 1).bit_length())
    bias_ref[...] = jnp.where(sel, 0.0, NEG_BIG)


def _dsa_sample_select(scores, scores_new, ksel):
    bd, n_pages, tn, _ = scores.shape
    nb = math.gcd(bd, SELECT_NB)
    return pl.pallas_call(
        functools.partial(_dsa_sample_select_kernel, nb=nb, tn=tn, n_pages=n_pages, ksel=ksel),
        grid=(bd // nb,),
        in_specs=[pl.BlockSpec((nb, n_pages, tn, PAGE), lambda i: (i, 0, 0, 0)),
                  pl.BlockSpec((nb, tn, PAGE), lambda i: (i, 0, 0))],
        out_specs=pl.BlockSpec((nb, n_pages + 1, tn, PAGE), lambda i: (i, 0, 0, 0)),
        out_shape=jax.ShapeDtypeStruct((bd, n_pages + 1, tn, PAGE), F32),
        compiler_params=_cparams("parallel"),
        name="dsa_sample_select",
    )(scores, scores_new)


def _dsa_sample_kernel(pt_ref, bias_ref, q_ref, kn_ref, vn_ref, *rest, tn, d, n_pages, nsteps, pp):
    del pt_ref
    kp_refs, vp_refs = rest[:pp], rest[pp:2 * pp]
    o_ref, qbd_sc, m_sc, l_sc, acc_sc = rest[2 * pp:]
    step = pl.program_id(1)
    groups = d // HEAD64
    rows = groups * tn

    @pl.when(step == 0)
    def _():
        qbd_sc[...] = _block_diag_queries(q_ref[0].astype(F32), groups, tn)
        m_sc[...] = jnp.full((rows, 1), NEG_BIG, F32)
        l_sc[...] = jnp.zeros((rows, 1), F32)
        acc_sc[...] = jnp.zeros((rows, d), F32)

    def tiled_bias(page):
        return jnp.concatenate([bias_ref[0, page]] * groups, axis=0)

    @pl.when(step < nsteps)
    def _():
        bias = jnp.concatenate([tiled_bias(step * pp + i) for i in range(pp)], axis=1)
        s = jnp.concatenate([_dot(qbd_sc[...], r[...].reshape(d, PAGE).astype(BF16)) for r in kp_refs],
                            axis=1) + bias
        pv = lambda pb: sum(_dot_nt(pb[:, i * PAGE:(i + 1) * PAGE], vp_refs[i][...].reshape(d, PAGE).astype(BF16))
                            for i in range(pp))
        _online_update(s, bias == 0.0, pv, m_sc, l_sc, acc_sc)

    @pl.when(step == nsteps)
    def _():
        bias = tiled_bias(n_pages)
        s = _dot_nt(qbd_sc[...], _pad_rows(kn_ref[0], PAGE).astype(BF16)) + bias
        pv = lambda pb: _dot(pb, _pad_rows(vn_ref[0], PAGE).astype(BF16))
        _online_update(s, bias == 0.0, pv, m_sc, l_sc, acc_sc)
        lo_half = _lane_iota() < HEAD64
        for c in range(d // LANES):
            cols = slice(c * LANES, (c + 1) * LANES)
            r0 = slice(2 * c * tn, (2 * c + 1) * tn)
            r1 = slice((2 * c + 1) * tn, (2 * c + 2) * tn)
            o = jnp.where(lo_half, acc_sc[r0, cols] / l_sc[r0, :], acc_sc[r1, cols] / l_sc[r1, :])
            o_ref[0, :, cols] = o.astype(BF16)


def _dsa_sample(page_table, bias, q3, kn3, vn3, k_pool_t, v_pool_t, layer):
    bd, tn, d = q3.shape
    n_pages = page_table.shape[1]
    groups = d // HEAD64
    rows = groups * tn
    pp = math.gcd(n_pages, DSA_SAMPLE_PP)
    nsteps = n_pages // pp
    seq = lambda: pl.BlockSpec((1, tn, d), lambda b, s, pt: (b, 0, 0))
    pool = lambda i: pl.BlockSpec(
        (None, None, groups, HEAD64, PAGE),
        lambda b, s, pt: (layer, pt[b, jnp.minimum(s, nsteps - 1) * pp + i], 0, 0, 0))
    kern = functools.partial(_dsa_sample_kernel, tn=tn, d=d, n_pages=n_pages, nsteps=nsteps, pp=pp)
    return pl.pallas_call(
        kern,
        grid_spec=pltpu.PrefetchScalarGridSpec(
            num_scalar_prefetch=1,
            grid=(bd, nsteps + 1),
            in_specs=[pl.BlockSpec((1, n_pages + 1, tn, PAGE), lambda b, s, pt: (b, 0, 0, 0)),
                      seq(), seq(), seq()] + [pool(i) for i in range(pp)] + [pool(i) for i in range(pp)],
            out_specs=seq(),
            scratch_shapes=[pltpu.VMEM((rows, d), BF16), pltpu.VMEM((rows, 1), F32),
                            pltpu.VMEM((rows, 1), F32), pltpu.VMEM((rows, d), F32)]),
        out_shape=jax.ShapeDtypeStruct((bd, tn, d), BF16),
        compiler_params=_cparams("parallel", "arbitrary"),
        name="dsa_sample",
    )(page_table, bias, q3, kn3, vn3, *([k_pool_t] * pp), *([v_pool_t] * pp))


def _hg_in_kernel(x_ref, g_ref, w_ref, lbw_ref, q_ref, k_ref, lf_ref, v_ref, gs_ref, *, d, layer):
    h = _rms(x_ref[...], g_ref[...]).astype(BF16)
    y = _dot(h, w_ref[...])
    lbw = lbw_ref[...]
    e = jnp.exp(lbw - jnp.max(lbw, axis=0, keepdims=True))
    sm = e / jnp.sum(e, axis=0, keepdims=True)
    lb = jnp.sum(sm[1:layer + 1, :], axis=0, keepdims=True)
    q, fz, v, g = y[:, :d], y[:, d:2 * d], y[:, 2 * d:3 * d], y[:, 3 * d:]
    f = lb + (1.0 - lb) * _sigmoid(fz)
    q_ref[...] = q * _sigmoid(q)
    k_ref[...] = 1.0 - f
    lf_ref[...] = jnp.log(f)
    v_ref[...] = v
    gs_ref[...] = g * _sigmoid(g)


def _hg_in(x, g1, w, lbw, layer):
    r, d = x.shape
    tm = min(r, ROW_TILE)
    row = lambda: pl.BlockSpec((tm, d), lambda i: (i, 0))
    return pl.pallas_call(
        functools.partial(_hg_in_kernel, d=d, layer=layer),
        grid=(r // tm,),
        in_specs=[row(), _const_spec((1, d)), _const_spec((d, 4 * d)), _const_spec(lbw.shape)],
        out_specs=[row()] * 5,
        out_shape=[jax.ShapeDtypeStruct((r, d), F32)] * 5,
        compiler_params=_cparams("parallel"),
        name="hg_in",
    )(x, g1, w, lbw)


def _hg_rec_kernel(q_ref, k_ref, lf_ref, v_ref, gs_ref, s0_ref, gn_ref, o_ref, s_ref, st_sc, *, tb, nt, hp):
    j = pl.program_id(2)

    @pl.when(j == 0)
    def _():
        for hh in range(hp):
            st_sc[hh] = s0_ref[hh].T

    c = HG_C
    r_io = lax.broadcasted_iota(I32, (c, c), 0)
    c_io = lax.broadcasted_iota(I32, (c, c), 1)
    causal = r_io >= c_io
    tri = jnp.where(causal, 1.0, 0.0)
    row = lax.broadcasted_iota(I32, (c, 1), 0)
    nch = max(1, tb // c)
    for ci, hh in [(ci, hh) for ci in range(nch) for hh in range(hp)]:
        cols = slice(hh * LANES, (hh + 1) * LANES)
        if tb >= c:
            rows = slice(ci * c, (ci + 1) * c)
            load = lambda ref: ref[rows, cols]
        else:
            load = lambda ref: _pad_rows(ref[:, cols], c)
        q, k, lf, v = load(q_ref), load(k_ref), load(lf_ref), load(v_ref)
        b = jnp.dot(tri, lf, preferred_element_type=F32, precision=lax.Precision.HIGHEST)
        st = st_sc[hh]
        o = _dot_nt((q * jnp.exp(b)).astype(BF16), st.astype(BF16))
        slabs = []
        for blk in range(c // HG_SUB):
            lo, hi = blk * HG_SUB, (blk + 1) * HG_SUB
            anchor = b[lo - 1:lo, :] if blk > 0 else jnp.zeros((1, LANES), F32)
            qb = q[lo:hi, :] * jnp.exp(b[lo:hi, :] - anchor)
            kb = k * jnp.exp(jnp.where(row < hi, anchor - b, -jnp.inf))
            slabs.append(_dot_nt(qb.astype(BF16), kb.astype(BF16)))
        a = jnp.where(causal, jnp.concatenate(slabs, axis=0), 0.0)
        o = o + _dot(a.astype(BF16), v.astype(BF16))
        b_last = b[c - 1:c, :]
        kd = k * jnp.exp(b_last - b)
        st_sc[hh] = st * jnp.exp(b_last) + _dot(v.T.astype(BF16), kd.astype(BF16))
        og = _rms(o, gn_ref[...])
        if tb >= c:
            o_ref[rows, cols] = (og * gs_ref[rows, cols]).astype(BF16)
        else:
            o_ref[:, cols] = (og[:tb, :] * gs_ref[:, cols]).astype(BF16)

    @pl.when(j == nt - 1)
    def _():
        for hh in range(hp):
            s_ref[hh] = st_sc[hh].T


def _hg_rec(q, k, lf, v, gs, s0, gn, bsz, t):
    r, d = q.shape
    heads = d // LANES
    hp = math.gcd(heads, HG_HP)
    tb = min(t, HG_TB)
    nt = t // tb
    blk = lambda: pl.BlockSpec((tb, hp * LANES), lambda b, h, j: (b * nt + j, h))
    state = lambda: pl.BlockSpec((None, hp, LANES, LANES), lambda b, h, j: (b, h, 0, 0))
    return pl.pallas_call(
        functools.partial(_hg_rec_kernel, tb=tb, nt=nt, hp=hp),
        grid=(bsz, heads // hp, nt),
        in_specs=[blk(), blk(), blk(), blk(), blk(), state(), _const_spec((1, LANES))],
        out_specs=[blk(), state()],
        out_shape=[jax.ShapeDtypeStruct((r, d), BF16),
                   jax.ShapeDtypeStruct((bsz, heads, LANES, LANES), F32)],
        scratch_shapes=[pltpu.VMEM((hp, LANES, LANES), F32)],
        compiler_params=_cparams("parallel", "parallel", "arbitrary"),
        name="hg_rec",
    )(q, k, lf, v, gs, s0, gn)


def kernel(x_prompt, x_sample, state_conv, cache_da_k, cache_da_v, cache_dsa_k, cache_dsa_v, cache_dsa_idx_k, state_hgrn, page_table, norm1_g, norm2_g, final_g, mlp_w1, mlp_w2, cv_w1, cv_b1, cv_dw, cv_dwb, cv_ln_g, cv_ln_b, cv_w2, cv_b2, da_w_in, da_lq1, da_lk1, da_lq2, da_lk2, da_subln_g, da_wo, dsa_w_in, dsa_wo, hg_w_in, hg_lb, hg_norm_g, hg_wo):
    bp, t, d = x_prompt.shape
    bd, tn, _ = x_sample.shape
    n_pages = page_table.shape[1]
    past_len = n_pages * PAGE
    depth = norm1_g.shape[0]
    assert d % LANES == 0 and t % ROW_TILE == 0 and tn & (tn - 1) == 0 and tn <= SUBLANES

    xp = x_prompt.reshape(bp * t, d)
    xs = x_sample.reshape(bd * tn, d)
    cos_p, sin_p = _rope_tables(jnp.arange(t))
    cos_s, sin_s = _rope_tables(past_len + jnp.arange(tn))
    cos_s, sin_s = jnp.tile(cos_s, (bd, 1)), jnp.tile(sin_s, (bd, 1))
    ksel_p = min(DSA_TOPK, t // 4)
    ksel_s = min(DSA_TOPK, (past_len + tn) // 4)
    zero_bias = jnp.zeros((1, d), F32)
    row1 = lambda a: a.reshape(1, -1)
    new = {n: [] for n in ('conv_p', 'conv_s', 'dak_p', 'dav_p', 'dak_s', 'dav_s', 'dsak_p', 'dsav_p', 'dsai_p',
                           'dsak_s', 'dsav_s', 'dsai_s', 'hg_p', 'hg_s')}

    for layer in range(depth):
        kind, j = layer % 4, layer // 4
        g1 = row1(norm1_g[layer])
        bo = zero_bias
        if kind == 0:
            pad = lambda a, n: jnp.concatenate([jnp.zeros(a.shape[:1] + (n,) + a.shape[2:], a.dtype), a], axis=1)
            dw = jnp.concatenate([cv_dw[j], jnp.zeros((CONV_HALO - CONV_W, d), F32)], axis=0)[:, None, :]
            cw = (g1, cv_w1[j].astype(BF16), row1(cv_b1[j]), dw, row1(cv_dwb[j]), row1(cv_ln_g[j]),
                  row1(cv_ln_b[j]))
            halo_pad = CONV_HALO - (CONV_W - 1)
            op, tail_p = _conv_mixer(xp.reshape(bp, t, d), jnp.zeros((bp, CONV_HALO, d), F32), *cw,
                                     nb=1, tt=CONV_TT)
            os_, tail_s = _conv_mixer(xs.reshape(bd, tn, d), pad(state_conv[j], halo_pad), *cw, nb=bd, tt=tn)
            op, os_ = op.reshape(bp * t, d), os_.reshape(bd * tn, d)
            new['conv_p'].append(tail_p[:, halo_pad:])
            new['conv_s'].append(tail_s[:, halo_pad:])
            wo, bo = cv_w2[j].astype(BF16), row1(cv_b2[j])
        elif kind == 1:
            lam_init = 0.8 - 0.6 * math.exp(-0.3 * layer)
            lams = (row1(da_lq1[j]), row1(da_lk1[j]), row1(da_lq2[j]), row1(da_lk2[j]))
            subg = row1(da_subln_g[j])
            w_in = da_w_in[j].astype(BF16)
            qp, kp, vp, kbp, vbp = _da_in(xp, g1, w_in, cos_p, sin_p)
            qs, ks, vs, _, _ = _da_in(xs, g1, w_in, cos_s, sin_s)
            op = _da_prompt(qp, kbp, vbp, lams, subg, bp, t, lam_init)
            os_ = _da_sample(page_table, qs.reshape(bd, tn, d), ks.reshape(bd, tn, d), vs.reshape(bd, tn, d),
                             cache_da_k, cache_da_v, lams, subg, j, lam_init).reshape(bd * tn, d)
            hd = (d // LANES, LANES)
            new['dak_p'].append(kp.reshape((bp, t) + hd))
            new['dav_p'].append(vp.reshape((bp, t) + hd))
            new['dak_s'].append(ks.reshape((bd, tn) + hd))
            new['dav_s'].append(vs.reshape((bd, tn) + hd))
            wo = da_wo[j].astype(BF16)
        elif kind == 2:
            w_in = dsa_w_in[j]
            w_in = jnp.concatenate([w_in, jnp.zeros((d, -w_in.shape[1] % LANES), F32)], axis=1).astype(BF16)
            qp, kp, vp, kbp, vbp, qip, mp_ = _dsa_in(xp, g1, w_in, cos_p, sin_p, seq_len=t)
            qs, ks, vs, _, _, qis, ms_ = _dsa_in(xs, g1, w_in, cos_s, sin_s)
            op = _dsa_prompt(qip, mp_, qp, kbp, vbp, bp, t, ksel_p)
            scores, scores_new = _dsa_sample_scores(
                page_table, qis.reshape(bd, tn, -1), ms_.reshape(bd, tn, LANES),
                jnp.transpose(cache_dsa_idx_k, (0, 1, 3, 2)), j)
            bias = _dsa_sample_select(scores, scores_new, ksel_s)
            os_ = _dsa_sample(page_table, bias, qs.reshape(bd, tn, d), ks.reshape(bd, tn, d),
                              vs.reshape(bd, tn, d), jnp.transpose(cache_dsa_k, (0, 1, 3, 4, 2)),
                              jnp.transpose(cache_dsa_v, (0, 1, 3, 4, 2)), j).reshape(bd * tn, d)
            hd = (d // HEAD64, HEAD64)
            new['dsak_p'].append(jnp.transpose(kp.reshape((bp,) + hd + (t,)), (0, 3, 1, 2)))
            new['dsav_p'].append(jnp.transpose(vp.reshape((bp,) + hd + (t,)), (0, 3, 1, 2)))
            new['dsai_p'].append(mp_[:, :HEAD64].reshape(bp, t, HEAD64))
            new['dsak_s'].append(ks.reshape((bd, tn) + hd))
            new['dsav_s'].append(vs.reshape((bd, tn) + hd))
            new['dsai_s'].append(ms_[:, :HEAD64].reshape(bd, tn, HEAD64))
            wo = dsa_wo[j].astype(BF16)
        else:
            w_in = hg_w_in[j].astype(BF16)
            gn = row1(hg_norm_g[j])
            heads = d // LANES
            hp = _hg_in(xp, g1, w_in, hg_lb, layer)
            hs = _hg_in(xs, g1, w_in, hg_lb, layer)
            op, sp = _hg_rec(*hp, jnp.zeros((bp, heads, LANES, LANES), F32), gn, bp, t)
            os_, ss = _hg_rec(*hs, state_hgrn[j], gn, bd, tn)
            new['hg_p'].append(sp.astype(state_hgrn.dtype))
            new['hg_s'].append(ss.astype(state_hgrn.dtype))
            wo = hg_wo[j].astype(BF16)

        final = layer == depth - 1
        post = (wo, bo, row1(norm2_g[layer]), mlp_w1[layer].astype(BF16), mlp_w2[layer].astype(BF16),
                row1(final_g))
        xp = _post(xp, op, *post, final=final)
        xs = _post(xs, os_, *post, final=final)

    return (xp.reshape(bp, t, d), xs.reshape(bd, tn, d),
            jnp.stack(new['conv_p']), jnp.stack(new['conv_s']),
            jnp.stack(new['dak_p']), jnp.stack(new['dav_p']), jnp.stack(new['dak_s']), jnp.stack(new['dav_s']),
            jnp.stack(new['dsak_p']), jnp.stack(new['dsav_p']), jnp.stack(new['dsai_p']),
            jnp.stack(new['dsak_s']), jnp.stack(new['dsav_s']), jnp.stack(new['dsai_s']),
            jnp.stack(new['hg_p']), jnp.stack(new['hg_s']))
```

```python
import functools
import math

import jax
import jax.numpy as jnp
from jax import lax
from jax.experimental import pallas as pl
from jax.experimental.pallas import tpu as pltpu

F32 = jnp.float32
BF16 = jnp.bfloat16
I32 = jnp.int32

NORM_EPS = 1e-6
LN_EPS = 1e-5
ROPE_THETA = 10000.0
ROPE_DIM = 64
CONV_W = 31
PAGE = 128
HEAD64 = 64
IDX_HEADS = 8
DSA_TOPK = 256

LANES = 128
SUBLANES = 8
CONV_HALO = 32
CONV_ROWS = 32
ROW_TILE = 512
CONV_TT = 256
ATT_TQ = 256
DA_HP = 4
DSA_TQ = 128
DSA_KV_GROUPS = 4
SAMPLE_PP = 8
DSA_SAMPLE_PP = 16
SCORE_PP = 8
HG_C = 128
HG_SUB = 16
HG_TB = 512
HG_HP = 8
SELECT_NB = 8
MLP_FCHUNK = 1024
VMEM_LIMIT = 56 * 1024 * 1024
NEG_BIG = -1e30
INT_MIN = -2 ** 31


def _cparams(*sem):
    return pltpu.CompilerParams(dimension_semantics=sem, vmem_limit_bytes=VMEM_LIMIT)


def _const_spec(shape):
    nd = len(shape)
    return pl.BlockSpec(shape, lambda *_: (0,) * nd, pipeline_mode=pl.Buffered(1))


def _dot(a, b):
    return jnp.dot(a, b, preferred_element_type=F32)


def _dot_nt(a, b):
    return lax.dot_general(a, b, (((1,), (1,)), ((), ())), preferred_element_type=F32)


def _rms(x, g):
    return x * lax.rsqrt(jnp.mean(x * x, axis=-1, keepdims=True) + NORM_EPS) * g


def _sigmoid(x):
    return 1.0 / (1.0 + jnp.exp(-x))


def _lane_iota(shape=(1, LANES)):
    return lax.broadcasted_iota(I32, shape, len(shape) - 1)


def _rope(xc, cos, sin_signed):
    first_half = (_lane_iota() & 32) == 0
    partner = jnp.where(first_half, pltpu.roll(xc, 96, 1), pltpu.roll(xc, 32, 1))
    return xc * cos + partner * sin_signed


def _rope_tables(pos):
    inv = ROPE_THETA ** (-jnp.arange(0, ROPE_DIM, 2, dtype=F32) / ROPE_DIM)
    ang = pos.astype(F32)[:, None] * inv[None, :]
    cos, sin = jnp.cos(ang), jnp.sin(ang)
    return jnp.tile(cos, (1, 4)), jnp.tile(jnp.concatenate([-sin, sin], axis=1), (1, 2))


def _pad_rows(x, rows):
    return jnp.concatenate([x, jnp.zeros((rows - x.shape[0], x.shape[1]), x.dtype)], axis=0)


def _post_kernel(x_ref, o_ref, wo_ref, bo_ref, g_ref, w1_ref, w2_ref, fg_ref, y_ref, h_ref, *, final):
    x1 = x_ref[...] + _dot(o_ref[...], wo_ref[...]) + bo_ref[...]
    h = _rms(x1, g_ref[...]).astype(BF16)
    f = w1_ref.shape[1]
    fc = min(f, MLP_FCHUNK)
    for c in range(f // fc):
        a = jnp.maximum(_dot(h, w1_ref[:, c * fc:(c + 1) * fc]), 0.0)
        h_ref[:, c * fc:(c + 1) * fc] = (a * a).astype(BF16)
    y = x1 + _dot(h_ref[...], w2_ref[...])
    if final:
        y = _rms(y, fg_ref[...])
    y_ref[...] = y


def _post(x, o, wo, bo, g2, w1_all, w2_all, fg, layer, final):
    r, d = x.shape
    f = w1_all.shape[2]
    tm = min(r, ROW_TILE)
    row = lambda w: pl.BlockSpec((tm, w), lambda i: (i, 0))
    of_layer = lambda a, b: pl.BlockSpec((None, a, b), lambda i: (layer, 0, 0), pipeline_mode=pl.Buffered(1))
    return pl.pallas_call(
        functools.partial(_post_kernel, final=final),
        grid=(r // tm,),
        in_specs=[row(d), row(d), _const_spec((d, d)), _const_spec((1, d)), _const_spec((1, d)),
                  of_layer(d, f), of_layer(f, d), _const_spec((1, d))],
        out_specs=row(d),
        out_shape=jax.ShapeDtypeStruct((r, d), F32),
        scratch_shapes=[pltpu.VMEM((tm, f), BF16)],
        compiler_params=_cparams("parallel"),
        name="post_mlp",
    )(x, o, wo, bo, g2, w1_all, w2_all, fg)


def _conv_kernel(x_ref, buf_ref, g_ref, w1_ref, b1_ref, dw_ref, dwb_ref, lng_ref, lnb_ref,
                 o_ref, tail_ref, ext_ref, y_ref, sh_ref, *, nb, tt, d, carry, preshift):
    @pl.when(pl.program_id(1) == 0)
    def _():
        ext_ref[:, 0:CONV_HALO, :] = buf_ref[...]

    x = x_ref[...].reshape(nb * tt, d)
    h = _rms(x, g_ref[...]).astype(BF16)
    ag = _dot(h, w1_ref[...]) + b1_ref[...]
    u = ag[:, :d] * _sigmoid(ag[:, d:])
    ext_ref[:, CONV_HALO:CONV_HALO + tt, :] = u.reshape(nb, tt, d)

    if preshift:
        for a in range(1, SUBLANES):
            sh_ref[a - 1] = ext_ref[0, a:a + sh_ref.shape[1], :]

    rs = min(tt, CONV_ROWS)
    for r in range(tt // rs):
        acc = jnp.zeros((nb, rs, d), F32)
        for k in range(CONV_W):
            off = k + CONV_HALO - (CONV_W - 1)
            a = off % SUBLANES
            if preshift and a:
                start = r * rs + off - a
                window = sh_ref[a - 1, start:start + rs, :][None]
            else:
                window = ext_ref[:, r * rs + off:r * rs + off + rs, :]
            acc = acc + dw_ref[k] * window
        y_ref[:, r * rs:(r + 1) * rs, :] = acc + dwb_ref[...]

    y = y_ref[...]
    mu = jnp.mean(y, axis=-1, keepdims=True)
    yc = y - mu
    var = jnp.mean(yc * yc, axis=-1, keepdims=True)
    z = yc * lax.rsqrt(var + LN_EPS) * lng_ref[...] + lnb_ref[...]
    o_ref[...] = (z * _sigmoid(z)).astype(BF16)
    tail = ext_ref[:, tt:tt + CONV_HALO, :]
    tail_ref[...] = tail
    if carry:
        ext_ref[:, 0:CONV_HALO, :] = tail


def _conv_mixer(x3, buf, g1, w1, b1, dw, dwb, lng, lnb, nb, tt):
    b, t, d = x3.shape
    nt = t // tt
    preshift = nb == 1 and tt > CONV_ROWS
    sh_shape = (SUBLANES - 1, tt + CONV_HALO - SUBLANES, d) if preshift else (1, SUBLANES, LANES)
    kern = functools.partial(_conv_kernel, nb=nb, tt=tt, d=d, carry=nt > 1, preshift=preshift)
    return pl.pallas_call(
        kern,
        grid=(b // nb, nt),
        in_specs=[pl.BlockSpec((nb, tt, d), lambda i, j: (i, j, 0)),
                  pl.BlockSpec((nb, CONV_HALO, d), lambda i, j: (i, 0, 0)),
                  _const_spec((1, d)), _const_spec((d, 2 * d)), _const_spec((1, 2 * d)),
                  _const_spec((CONV_HALO, 1, d)), _const_spec((1, d)), _const_spec((1, d)),
                  _const_spec((1, d))],
        out_specs=[pl.BlockSpec((nb, tt, d), lambda i, j: (i, j, 0)),
                   pl.BlockSpec((nb, CONV_HALO, d), lambda i, j: (i, 0, 0))],
        out_shape=[jax.ShapeDtypeStruct((b, t, d), BF16),
                   jax.ShapeDtypeStruct((b, CONV_HALO, d), F32)],
        scratch_shapes=[pltpu.VMEM((nb, CONV_HALO + tt, d), F32), pltpu.VMEM((nb, tt, d), F32),
                        pltpu.VMEM(sh_shape, F32)],
        compiler_params=_cparams("parallel", "arbitrary"),
        name="conv_mixer",
    )(x3, buf, g1, w1, b1, dw, dwb, lng, lnb)


def _da_in_kernel(x_ref, g_ref, w_ref, cos_ref, sin_ref, q_ref, k_ref, v_ref, kb_ref, vb_ref, *, d):
    h = _rms(x_ref[...], g_ref[...]).astype(BF16)
    y = _dot(h, w_ref[...])
    cos, sin = cos_ref[...], sin_ref[...]
    for c in range(d // LANES):
        sl = slice(c * LANES, (c + 1) * LANES)
        q = _rope(y[:, c * LANES:(c + 1) * LANES], cos, sin)
        q_ref[:, sl] = (q * HEAD64 ** -0.5).astype(BF16)
        k = _rope(y[:, d + c * LANES:d + (c + 1) * LANES], cos, sin)
        k_ref[:, sl] = k
        kb_ref[:, sl] = k.astype(BF16)
        v = y[:, 2 * d + c * LANES:2 * d + (c + 1) * LANES]
        v_ref[:, sl] = v
        vb_ref[:, sl] = v.astype(BF16)


def _da_in(x, g1, w, cos, sin):
    r, d = x.shape
    tm = min(r, ROW_TILE)
    ntab = cos.shape[0] // tm
    row = lambda: pl.BlockSpec((tm, d), lambda i: (i, 0))
    tab = lambda: pl.BlockSpec((tm, LANES), lambda i: (i % ntab, 0))
    return pl.pallas_call(
        functools.partial(_da_in_kernel, d=d),
        grid=(r // tm,),
        in_specs=[row(), _const_spec((1, d)), _const_spec((d, 3 * d)), tab(), tab()],
        out_specs=[row(), row(), row(), row(), row()],
        out_shape=[jax.ShapeDtypeStruct((r, d), BF16), jax.ShapeDtypeStruct((r, d), F32),
                   jax.ShapeDtypeStruct((r, d), F32), jax.ShapeDtypeStruct((r, d), BF16),
                   jax.ShapeDtypeStruct((r, d), BF16)],
        compiler_params=_cparams("parallel"),
        name="da_in",
    )(x, g1, w, cos, sin)


def _da_lambda(lq1, lk1, lq2, lk2, lam_init):
    return (jnp.exp(jnp.sum(lq1[...] * lk1[...], axis=-1, keepdims=True))
            - jnp.exp(jnp.sum(lq2[...] * lk2[...], axis=-1, keepdims=True)) + lam_init)


def _da_prompt_kernel(lq1, lk1, lq2, lk2, subg_ref, q_ref, k_ref, v_ref, o_ref, *, tq, nq, hp, lam_init):
    i = pl.program_id(2)
    lam = _da_lambda(lq1, lk1, lq2, lk2, lam_init)
    lane = _lane_iota()
    keep = (lax.broadcasted_iota(I32, (tq, tq), 0) >= lax.broadcasted_iota(I32, (tq, tq), 1))

    def body(c, hh):
        off = c * tq
        cols = slice(hh * LANES, (hh + 1) * LANES)
        q = q_ref[:, cols]
        zero = jnp.zeros_like(q)
        q1 = jnp.where(lane < HEAD64, q, zero)
        q2 = jnp.where(lane >= HEAD64, q, zero)

        def softmax_parts(qm):
            sd = jnp.where(keep, _dot_nt(qm, k_ref[off:off + tq, cols]), NEG_BIG)
            m = jnp.max(sd, axis=-1, keepdims=True)
            so = None
            if c > 0:
                so = _dot_nt(qm, k_ref[0:off, cols])
                m = jnp.maximum(m, jnp.max(so, axis=-1, keepdims=True))
            pd = jnp.exp(sd - m)
            l = jnp.sum(pd, axis=-1, keepdims=True)
            po = None
            if c > 0:
                po = jnp.exp(so - m)
                l = l + jnp.sum(po, axis=-1, keepdims=True)
            return pd, po, l

        pd1, po1, l1 = softmax_parts(q1)
        pd2, po2, l2 = softmax_parts(q2)
        w1 = 1.0 / l1
        w2 = lam / l2
        o = _dot((pd1 * w1 - pd2 * w2).astype(BF16), v_ref[off:off + tq, cols])
        if c > 0:
            o = o + _dot((po1 * w1 - po2 * w2).astype(BF16), v_ref[0:off, cols])
        o_ref[:, cols] = (_rms(o, subg_ref[...]) * (1.0 - lam_init)).astype(BF16)

    def block(c):
        for hh in range(hp):
            body(c, hh)

    for c in range(nq):
        pl.when(i == c)(functools.partial(block, c))


def _da_prompt(q, kb, vb, lams, subg, bsz, t, lam_init):
    r, d = q.shape
    tq = min(t, ATT_TQ)
    nq = t // tq
    hp = math.gcd(d // LANES, DA_HP)
    lam_spec = _const_spec((1, HEAD64))
    return pl.pallas_call(
        functools.partial(_da_prompt_kernel, tq=tq, nq=nq, hp=hp, lam_init=lam_init),
        grid=(bsz, d // (hp * LANES), nq),
        in_specs=[lam_spec, lam_spec, lam_spec, lam_spec, _const_spec((1, LANES)),
                  pl.BlockSpec((tq, hp * LANES), lambda b, h, i: (b * nq + i, h)),
                  pl.BlockSpec((t, hp * LANES), lambda b, h, i: (b, h)),
                  pl.BlockSpec((t, hp * LANES), lambda b, h, i: (b, h))],
        out_specs=pl.BlockSpec((tq, hp * LANES), lambda b, h, i: (b * nq + i, h)),
        out_shape=jax.ShapeDtypeStruct((r, d), BF16),
        compiler_params=_cparams("parallel", "parallel", "arbitrary"),
        name="da_prompt",
    )(*lams, subg, q, kb, vb)


def _online_update(s, valid, pv_fn, m_sc, l_sc, acc_sc):
    m_old = m_sc[...]
    m_new = jnp.maximum(m_old, jnp.max(s, axis=-1, keepdims=True))
    alpha = jnp.exp(m_old - m_new)
    p = jnp.exp(s - m_new)
    if valid is not None:
        p = jnp.where(valid, p, 0.0)
    l_sc[...] = alpha * l_sc[...] + jnp.sum(p, axis=-1, keepdims=True)
    acc_sc[...] = alpha * acc_sc[...] + pv_fn(p.astype(BF16))
    m_sc[...] = m_new


def _da_sample_kernel(pt_ref, lq1, lk1, lq2, lk2, subg_ref, q_ref, kn_ref, vn_ref, *rest,
                      tn, d, lam_init, nsteps, pp):
    del pt_ref
    kp_refs, vp_refs = rest[:pp], rest[pp:2 * pp]
    o_ref, qh_sc, own_sc, m_sc, l_sc, acc_sc = rest[2 * pp:]
    step = pl.program_id(1)
    heads = d // LANES
    gr = 2 * tn
    rows = heads * gr
    ncol = pp * PAGE * heads

    @pl.when(step == 0)
    def _():
        q = q_ref[0].astype(F32)
        first_map = lax.broadcasted_iota(I32, (gr, LANES), 0) < tn
        lo_half = _lane_iota((gr, LANES)) < HEAD64
        for h in range(heads):
            qc = q[:, h * LANES:(h + 1) * LANES]
            qh_sc[h * gr:(h + 1) * gr, :] = jnp.where(
                first_map == lo_half, jnp.concatenate([qc, qc], axis=0), 0.0).astype(BF16)
        row_head = lax.broadcasted_iota(I32, (rows, ncol), 0) >> int(math.log2(gr))
        col_head = lax.broadcasted_iota(I32, (rows, ncol), 1) & (heads - 1)
        own_sc[...] = jnp.where(row_head == col_head, 0.0, NEG_BIG)
        m_sc[...] = jnp.full((rows, 1), NEG_BIG, F32)
        l_sc[...] = jnp.zeros((rows, 1), F32)
        acc_sc[...] = jnp.zeros((rows, LANES), F32)

    flat = lambda refs: jnp.concatenate([r[...] for r in refs], axis=0).astype(BF16)
    s = _dot_nt(qh_sc[...], flat(kp_refs)) + own_sc[...]
    _online_update(s, None, lambda pb: _dot(pb, flat(vp_refs)), m_sc, l_sc, acc_sc)

    @pl.when(step == nsteps - 1)
    def _():
        fresh = lambda ref, h: _pad_rows(ref[0][:, h * LANES:(h + 1) * LANES], PAGE).astype(BF16)
        qpos = lax.broadcasted_iota(I32, (rows, PAGE), 0) & (tn - 1)
        valid = lax.broadcasted_iota(I32, (rows, PAGE), 1) <= qpos
        s_new = jnp.concatenate(
            [_dot_nt(qh_sc[h * gr:(h + 1) * gr, :], fresh(kn_ref, h)) for h in range(heads)], axis=0)
        pv = lambda pb: jnp.concatenate(
            [_dot(pb[h * gr:(h + 1) * gr, :], fresh(vn_ref, h)) for h in range(heads)], axis=0)
        _online_update(jnp.where(valid, s_new, NEG_BIG), valid, pv, m_sc, l_sc, acc_sc)
        lam = _da_lambda(lq1, lk1, lq2, lk2, lam_init)
        for h in range(heads):
            r1 = slice(h * gr, h * gr + tn)
            r2 = slice(h * gr + tn, (h + 1) * gr)
            o = acc_sc[r1, :] / l_sc[r1, :] - lam * (acc_sc[r2, :] / l_sc[r2, :])
            o_ref[0, :, h * LANES:(h + 1) * LANES] = (_rms(o, subg_ref[...]) * (1.0 - lam_init)).astype(BF16)


def _da_sample(page_table, q3, kn3, vn3, k_pool, v_pool, lams, subg, layer, lam_init):
    bd, tn, d = q3.shape
    n_pages = page_table.shape[1]
    heads = d // LANES
    pp = math.gcd(n_pages, SAMPLE_PP)
    nsteps = n_pages // pp
    rows = heads * 2 * tn
    assert heads & (heads - 1) == 0
    flat_shape = k_pool.shape[:2] + (PAGE * heads, LANES)
    k_pool, v_pool = k_pool.reshape(flat_shape), v_pool.reshape(flat_shape)
    seq = lambda: pl.BlockSpec((1, tn, d), lambda b, s, pt: (b, 0, 0))
    pool = lambda i: pl.BlockSpec((None, None, PAGE * heads, LANES),
                                  lambda b, s, pt: (layer, pt[b, s * pp + i], 0, 0))
    lam_spec = _const_spec((1, HEAD64))
    kern = functools.partial(_da_sample_kernel, tn=tn, d=d, lam_init=lam_init, nsteps=nsteps, pp=pp)
    return pl.pallas_call(
        kern,
        grid_spec=pltpu.PrefetchScalarGridSpec(
            num_scalar_prefetch=1,
            grid=(bd, nsteps),
            in_specs=[lam_spec, lam_spec, lam_spec, lam_spec, _const_spec((1, LANES)), seq(), seq(), seq()]
            + [pool(i) for i in range(pp)] + [pool(i) for i in range(pp)],
            out_specs=seq(),
            scratch_shapes=[pltpu.VMEM((rows, LANES), BF16), pltpu.VMEM((rows, pp * PAGE * heads), F32),
                            pltpu.VMEM((rows, 1), F32), pltpu.VMEM((rows, 1), F32),
                            pltpu.VMEM((rows, LANES), F32)]),
        out_shape=jax.ShapeDtypeStruct((bd, tn, d), BF16),
        compiler_params=_cparams("parallel", "arbitrary"),
        name="da_sample",
    )(page_table, *lams, subg, q3, kn3, vn3, *([k_pool] * pp), *([v_pool] * pp))


def _dsa_in_kernel(x_ref, g_ref, w_ref, cos_ref, sin_ref, q_ref, k_ref, v_ref, kb_ref, vb_ref,
                   qi_ref, misc_ref, *, d, token_minor):
    h = _rms(x_ref[...], g_ref[...]).astype(BF16)
    y = _dot(h, w_ref[...])
    cos, sin = cos_ref[...], sin_ref[...]
    for c in range(d // LANES):
        sl = slice(c * LANES, (c + 1) * LANES)
        q = _rope(y[:, c * LANES:(c + 1) * LANES], cos, sin)
        q_ref[:, sl] = (q * HEAD64 ** -0.5).astype(BF16)
        k = _rope(y[:, d + c * LANES:d + (c + 1) * LANES], cos, sin)
        kb_ref[:, sl] = k.astype(BF16)
        v = y[:, 2 * d + c * LANES:2 * d + (c + 1) * LANES]
        vb_ref[:, sl] = v.astype(BF16)
        if token_minor:
            k_ref[sl, :] = k.T
            v_ref[sl, :] = v.T
        else:
            k_ref[:, sl] = k
            v_ref[:, sl] = v
    for c in range(IDX_HEADS * HEAD64 // LANES):
        qi = _rope(y[:, 3 * d + c * LANES:3 * d + (c + 1) * LANES], cos, sin)
        qi_ref[:, c * LANES:(c + 1) * LANES] = (qi * HEAD64 ** -0.5).astype(BF16)
    is_key = _lane_iota() < HEAD64
    base = 3 * d + IDX_HEADS * HEAD64
    misc_ref[...] = _rope(y[:, base:base + LANES], jnp.where(is_key, cos, 1.0), jnp.where(is_key, sin, 0.0))


def _dsa_in(x, g1, w, cos, sin, seq_len=None):
    r, d = x.shape
    tm = min(r, ROW_TILE)
    ntab = cos.shape[0] // tm
    nqi = IDX_HEADS * HEAD64
    row = lambda w_: pl.BlockSpec((tm, w_), lambda i: (i, 0))
    tab = lambda: pl.BlockSpec((tm, LANES), lambda i: (i % ntab, 0))
    if seq_len is None:
        kv_spec, kv_shape = row(d), jax.ShapeDtypeStruct((r, d), F32)
    else:
        nt = seq_len // tm
        kv_spec = pl.BlockSpec((None, d, tm), lambda i: (i // nt, 0, i % nt))
        kv_shape = jax.ShapeDtypeStruct((r // seq_len, d, seq_len), F32)
    return pl.pallas_call(
        functools.partial(_dsa_in_kernel, d=d, token_minor=seq_len is not None),
        grid=(r // tm,),
        in_specs=[row(d), _const_spec((1, d)), _const_spec(w.shape), tab(), tab()],
        out_specs=[row(d), kv_spec, kv_spec, row(d), row(d), row(nqi), row(LANES)],
        out_shape=[jax.ShapeDtypeStruct((r, d), BF16), kv_shape, kv_shape,
                   jax.ShapeDtypeStruct((r, d), BF16),
                   jax.ShapeDtypeStruct((r, d), BF16), jax.ShapeDtypeStruct((r, nqi), BF16),
                   jax.ShapeDtypeStruct((r, LANES), F32)],
        compiler_params=_cparams("parallel"),
        name="dsa_in",
    )(x, g1, w, cos, sin)


def _sortable_key(score):
    bits = lax.bitcast_convert_type(score + 0.0, I32)
    return bits ^ ((bits >> 31) & 0x7FFFFFFF)


def _select_topk(key, pos, krow, count, pos_bits):
    def try_bit(ans, bit):
        cand = ans | lax.shift_left(jnp.int32(1), bit)
        return jnp.where(count(key >= cand) >= krow, cand, ans)

    def two_bit_step(it, ans):
        lo = lax.shift_left(jnp.int32(1), 28 - 2 * it)
        c1, c2, c3 = ans | lo, ans | (lo + lo), ans | (lo + lo) | lo
        n1, n2, n3 = count(key >= c1), count(key >= c2), count(key >= c3)
        return jnp.where(n3 >= krow, c3, jnp.where(n2 >= krow, c2, jnp.where(n1 >= krow, c1, ans)))

    ans = jnp.where(count(key >= 0) >= krow, jnp.int32(0), jnp.int32(INT_MIN))
    ans = try_bit(ans, 30)
    ans = lax.fori_loop(0, 15, two_bit_step, ans)
    gt = key > ans
    eq = key == ans
    need = krow - count(gt)

    def pos_step(it, cut):
        cand = cut | lax.shift_left(jnp.int32(1), pos_bits - 1 - it)
        return jnp.where(count(eq & (pos < cand)) < need, cand, cut)

    surplus = jnp.max(count(eq) - need) > 0.0
    cut = lax.cond(surplus,
                   lambda: lax.fori_loop(0, pos_bits, pos_step, jnp.zeros(krow.shape, I32)),
                   lambda: jnp.full(krow.shape, 2 ** pos_bits, I32))
    return gt | (eq & (pos <= cut))


def _dsa_prompt_kernel(qi_ref, wq_ref, kim_ref, q_ref, k_ref, v_ref, o_ref, kid_sc, *, tq, nq, d, ksel):
    i = pl.program_id(1)
    lane = _lane_iota()
    lo_half = lane < HEAD64

    @pl.when(i == 0)
    def _():
        lo = jnp.where(lo_half, kim_ref[...], 0.0)
        kid_sc[...] = (lo + pltpu.roll(lo, HEAD64, 1)).astype(BF16)

    def body(kv):
        wt = wq_ref[...] * IDX_HEADS ** -0.5
        kid = kid_sc[0:kv, :]
        score = jnp.zeros((tq, kv), F32)
        for c in range(IDX_HEADS // 2):
            qc = qi_ref[:, c * LANES:(c + 1) * LANES]
            zero = jnp.zeros_like(qc)
            dots = _dot_nt(jnp.concatenate([jnp.where(lo_half, qc, zero), jnp.where(lo_half, zero, qc)], axis=0),
                           kid)
            for half in range(2):
                h = 2 * c + half
                score = score + wt[:, HEAD64 + h:HEAD64 + h + 1] * jnp.maximum(dots[half * tq:(half + 1) * tq, :], 0.0)

        qpos = i * tq + lax.broadcasted_iota(I32, (tq, 1), 0)
        kpos = lax.broadcasted_iota(I32, (tq, kv), 1)
        key = jnp.where(kpos <= qpos, _sortable_key(score), INT_MIN)
        krow = jnp.minimum(qpos + 1, ksel).astype(F32)
        count = lambda mask: jnp.sum(jnp.where(mask, 1.0, 0.0), axis=-1, keepdims=True)
        sel = _select_topk(key, kpos, krow, count, max(1, (kv - 1).bit_length()))
        bias = jnp.where(sel, 0.0, NEG_BIG)
        bias2 = jnp.concatenate([bias, bias], axis=0)

        for c in range(d // LANES):
            cols = slice(c * LANES, (c + 1) * LANES)
            qc = q_ref[:, cols]
            kc = k_ref[0:kv, cols]
            vc = v_ref[0:kv, cols]
            zero = jnp.zeros_like(qc)
            qq = jnp.concatenate([jnp.where(lo_half, qc, zero), jnp.where(lo_half, zero, qc)], axis=0)
            s = _dot_nt(qq, kc) + bias2
            p = jnp.exp(s - jnp.max(s, axis=-1, keepdims=True))
            l = jnp.sum(p, axis=-1, keepdims=True)
            o2 = _dot(p.astype(BF16), vc) / l
            o_ref[:, cols] = jnp.where(lo_half, o2[:tq, :], o2[tq:, :]).astype(BF16)

    groups = math.gcd(nq, DSA_KV_GROUPS)
    per = nq // groups
    for g in range(groups):
        pl.when((i >= g * per) & (i < (g + 1) * per))(functools.partial(body, (g + 1) * per * tq))


def _dsa_prompt(qi, misc, q, kb, vb, bsz, t, ksel):
    r, d = q.shape
    tq = min(t, DSA_TQ)
    nq = t // tq
    nqi = qi.shape[1]
    qrow = lambda w: pl.BlockSpec((tq, w), lambda b, i: (b * nq + i, 0))
    seq = lambda w: pl.BlockSpec((t, w), lambda b, i: (b, 0))
    return pl.pallas_call(
        functools.partial(_dsa_prompt_kernel, tq=tq, nq=nq, d=d, ksel=ksel),
        grid=(bsz, nq),
        in_specs=[qrow(nqi), qrow(LANES), seq(LANES), qrow(d), seq(d), seq(d)],
        out_specs=qrow(d),
        out_shape=jax.ShapeDtypeStruct((r, d), BF16),
        scratch_shapes=[pltpu.VMEM((t, LANES), BF16)],
        compiler_params=_cparams("parallel", "arbitrary"),
        name="dsa_prompt",
    )(qi, misc, misc, q, kb, vb)


def _dsa_sample_scores_kernel(pt_ref, qi_ref, misc_ref, *rest, tn, nsteps, pp):
    del pt_ref
    pool_refs = rest[:pp]
    sc_ref, scn_ref, qst_sc, w_sc = rest[pp:]
    step = pl.program_id(1)

    @pl.when(step == 0)
    def _():
        qi = qi_ref[0].astype(F32)
        misc = misc_ref[0]
        qst_sc[...] = jnp.concatenate(
            [qi[:, h * HEAD64:(h + 1) * HEAD64] for h in range(IDX_HEADS)], axis=0).astype(BF16)
        w_sc[...] = jnp.concatenate(
            [misc[:, HEAD64 + h:HEAD64 + h + 1] for h in range(IDX_HEADS)], axis=0) * IDX_HEADS ** -0.5

    def head_sum(dots):
        sc = w_sc[...] * jnp.maximum(dots, 0.0)
        out = sc[0:tn, :]
        for h in range(1, IDX_HEADS):
            out = out + sc[h * tn:(h + 1) * tn, :]
        return out

    for i in range(pp):
        sc_ref[0, i] = head_sum(_dot(qst_sc[...], pool_refs[i][...].astype(BF16)))

    @pl.when(step == nsteps - 1)
    def _():
        fresh = _pad_rows(misc_ref[0][:, :HEAD64], PAGE).astype(BF16)
        scn_ref[0] = head_sum(_dot_nt(qst_sc[...], fresh))


def _dsa_sample_scores(page_table, qi3, misc3, idx_pool_t, layer):
    bd, tn, nqi = qi3.shape
    n_pages = page_table.shape[1]
    pp = math.gcd(n_pages, SCORE_PP)
    nsteps = n_pages // pp
    pool = lambda i: pl.BlockSpec((None, None, HEAD64, PAGE),
                                  lambda b, s, pt: (layer, pt[b, s * pp + i], 0, 0))
    kern = functools.partial(_dsa_sample_scores_kernel, tn=tn, nsteps=nsteps, pp=pp)
    return pl.pallas_call(
        kern,
        grid_spec=pltpu.PrefetchScalarGridSpec(
            num_scalar_prefetch=1,
            grid=(bd, nsteps),
            in_specs=[pl.BlockSpec((1, tn, nqi), lambda b, s, pt: (b, 0, 0)),
                      pl.BlockSpec((1, tn, LANES), lambda b, s, pt: (b, 0, 0))]
            + [pool(i) for i in range(pp)],
            out_specs=[pl.BlockSpec((1, pp, tn, PAGE), lambda b, s, pt: (b, s, 0, 0)),
                       pl.BlockSpec((1, tn, PAGE), lambda b, s, pt: (b, 0, 0))],
            scratch_shapes=[pltpu.VMEM((IDX_HEADS * tn, HEAD64), BF16),
                            pltpu.VMEM((IDX_HEADS * tn, 1), F32)]),
        out_shape=[jax.ShapeDtypeStruct((bd, n_pages, tn, PAGE), F32),
                   jax.ShapeDtypeStruct((bd, tn, PAGE), F32)],
        compiler_params=_cparams("parallel", "arbitrary"),
        name="dsa_sample_scores",
    )(page_table, qi3, misc3, *([idx_pool_t] * pp))


def _block_diag_queries(q, groups, tn):
    d = q.shape[1]
    qt = jnp.concatenate([q] * groups, axis=0)
    shift = int(math.log2(tn))
    rg = lax.broadcasted_iota(I32, (groups * tn, d), 0) >> shift
    cg = lax.broadcasted_iota(I32, (groups * tn, d), 1) >> 6
    return jnp.where(rg == cg, qt, 0.0).astype(BF16)


def _dsa_sample_select_kernel(sc_ref, scn_ref, bias_ref, *, nb, tn, n_pages, ksel):
    shape = (nb, n_pages + 1, tn, PAGE)
    page = lax.broadcasted_iota(I32, shape, 1)
    qidx = lax.broadcasted_iota(I32, shape, 2)
    lane = lax.broadcasted_iota(I32, shape, 3)
    score = jnp.concatenate([sc_ref[...], scn_ref[...].reshape(nb, 1, tn, PAGE)], axis=1)
    key = jnp.where((page < n_pages) | (lane <= qidx), _sortable_key(score), INT_MIN)
    krow = jnp.full((nb, 1, tn, 1), float(ksel), F32)
    count = lambda mask: jnp.sum(jnp.sum(jnp.where(mask, 1.0, 0.0), axis=1, keepdims=True),
                                 axis=-1, keepdims=True)
    sel = _select_topk(key, page * PAGE + lane, krow, count, ((n_pages + 1) * PAGE - 1).bit_length())
    bias_ref[...] = jnp.where(sel, 0.0, NEG_BIG)


def _dsa_sample_select(scores, scores_new, ksel):
    bd, n_pages, tn, _ = scores.shape
    nb = math.gcd(bd, SELECT_NB)
    return pl.pallas_call(
        functools.partial(_dsa_sample_select_kernel, nb=nb, tn=tn, n_pages=n_pages, ksel=ksel),
        grid=(bd // nb,),
        in_specs=[pl.BlockSpec((nb, n_pages, tn, PAGE), lambda i: (i, 0, 0, 0)),
                  pl.BlockSpec((nb, tn, PAGE), lambda i: (i, 0, 0))],
        out_specs=pl.BlockSpec((nb, n_pages + 1, tn, PAGE), lambda i: (i, 0, 0, 0)),
        out_shape=jax.ShapeDtypeStruct((bd, n_pages + 1, tn, PAGE), F32),
        compiler_params=_cparams("parallel"),
        name="dsa_sample_select",
    )(scores, scores_new)


def _dsa_sample_kernel(pt_ref, bias_ref, q_ref, kn_ref, vn_ref, *rest, tn, d, n_pages, nsteps, pp):
    del pt_ref
    kp_refs, vp_refs = rest[:pp], rest[pp:2 * pp]
    o_ref, qbd_sc, m_sc, l_sc, acc_sc = rest[2 * pp:]
    step = pl.program_id(1)
    groups = d // HEAD64
    rows = groups * tn

    @pl.when(step == 0)
    def _():
        qbd_sc[...] = _block_diag_queries(q_ref[0].astype(F32), groups, tn)
        m_sc[...] = jnp.full((rows, 1), NEG_BIG, F32)
        l_sc[...] = jnp.zeros((rows, 1), F32)
        acc_sc[...] = jnp.zeros((rows, d), F32)

    def tiled_bias(page):
        return jnp.concatenate([bias_ref[0, page]] * groups, axis=0)

    @pl.when(step < nsteps)
    def _():
        bias = jnp.concatenate([tiled_bias(step * pp + i) for i in range(pp)], axis=1)
        s = jnp.concatenate([_dot(qbd_sc[...], r[...].reshape(d, PAGE).astype(BF16)) for r in kp_refs],
                            axis=1) + bias
        pv = lambda pb: sum(_dot_nt(pb[:, i * PAGE:(i + 1) * PAGE], vp_refs[i][...].reshape(d, PAGE).astype(BF16))
                            for i in range(pp))
        _online_update(s, bias == 0.0, pv, m_sc, l_sc, acc_sc)

    @pl.when(step == nsteps)
    def _():
        bias = tiled_bias(n_pages)
        s = _dot_nt(qbd_sc[...], _pad_rows(kn_ref[0], PAGE).astype(BF16)) + bias
        pv = lambda pb: _dot(pb, _pad_rows(vn_ref[0], PAGE).astype(BF16))
        _online_update(s, bias == 0.0, pv, m_sc, l_sc, acc_sc)
        lo_half = _lane_iota() < HEAD64
        for c in range(d // LANES):
            cols = slice(c * LANES, (c + 1) * LANES)
            r0 = slice(2 * c * tn, (2 * c + 1) * tn)
            r1 = slice((2 * c + 1) * tn, (2 * c + 2) * tn)
            o = jnp.where(lo_half, acc_sc[r0, cols] / l_sc[r0, :], acc_sc[r1, cols] / l_sc[r1, :])
            o_ref[0, :, cols] = o.astype(BF16)


def _dsa_sample(page_table, bias, q3, kn3, vn3, k_pool_t, v_pool_t, layer):
    bd, tn, d = q3.shape
    n_pages = page_table.shape[1]
    groups = d // HEAD64
    rows = groups * tn
    pp = math.gcd(n_pages, DSA_SAMPLE_PP)
    nsteps = n_pages // pp
    seq = lambda: pl.BlockSpec((1, tn, d), lambda b, s, pt: (b, 0, 0))
    pool = lambda i: pl.BlockSpec(
        (None, None, groups, HEAD64, PAGE),
        lambda b, s, pt: (layer, pt[b, jnp.minimum(s, nsteps - 1) * pp + i], 0, 0, 0))
    kern = functools.partial(_dsa_sample_kernel, tn=tn, d=d, n_pages=n_pages, nsteps=nsteps, pp=pp)
    return pl.pallas_call(
        kern,
        grid_spec=pltpu.PrefetchScalarGridSpec(
            num_scalar_prefetch=1,
            grid=(bd, nsteps + 1),
            in_specs=[pl.BlockSpec((1, n_pages + 1, tn, PAGE), lambda b, s, pt: (b, 0, 0, 0)),
                      seq(), seq(), seq()] + [pool(i) for i in range(pp)] + [pool(i) for i in range(pp)],
            out_specs=seq(),
            scratch_shapes=[pltpu.VMEM((rows, d), BF16), pltpu.VMEM((rows, 1), F32),
                            pltpu.VMEM((rows, 1), F32), pltpu.VMEM((rows, d), F32)]),
        out_shape=jax.ShapeDtypeStruct((bd, tn, d), BF16),
        compiler_params=_cparams("parallel", "arbitrary"),
        name="dsa_sample",
    )(page_table, bias, q3, kn3, vn3, *([k_pool_t] * pp), *([v_pool_t] * pp))


def _hg_in_kernel(x_ref, g_ref, w_ref, lbw_ref, q_ref, k_ref, lf_ref, v_ref, gs_ref, *, d, layer):
    h = _rms(x_ref[...], g_ref[...]).astype(BF16)
    y = _dot(h, w_ref[...])
    lbw = lbw_ref[...]
    e = jnp.exp(lbw - jnp.max(lbw, axis=0, keepdims=True))
    sm = e / jnp.sum(e, axis=0, keepdims=True)
    lb = jnp.sum(sm[1:layer + 1, :], axis=0, keepdims=True)
    q, fz, v, g = y[:, :d], y[:, d:2 * d], y[:, 2 * d:3 * d], y[:, 3 * d:]
    f = lb + (1.0 - lb) * _sigmoid(fz)
    q_ref[...] = q * _sigmoid(q)
    k_ref[...] = 1.0 - f
    lf_ref[...] = jnp.log(f)
    v_ref[...] = v
    gs_ref[...] = g * _sigmoid(g)


def _hg_in(x, g1, w, lbw, layer):
    r, d = x.shape
    tm = min(r, ROW_TILE)
    row = lambda: pl.BlockSpec((tm, d), lambda i: (i, 0))
    return pl.pallas_call(
        functools.partial(_hg_in_kernel, d=d, layer=layer),
        grid=(r // tm,),
        in_specs=[row(), _const_spec((1, d)), _const_spec((d, 4 * d)), _const_spec(lbw.shape)],
        out_specs=[row()] * 5,
        out_shape=[jax.ShapeDtypeStruct((r, d), F32)] * 5,
        compiler_params=_cparams("parallel"),
        name="hg_in",
    )(x, g1, w, lbw)


def _hg_rec_kernel(q_ref, k_ref, lf_ref, v_ref, gs_ref, s0_ref, gn_ref, o_ref, s_ref, st_sc, *, tb, nt, hp):
    j = pl.program_id(2)

    @pl.when(j == 0)
    def _():
        for hh in range(hp):
            st_sc[hh] = s0_ref[hh].T

    c = HG_C
    r_io = lax.broadcasted_iota(I32, (c, c), 0)
    c_io = lax.broadcasted_iota(I32, (c, c), 1)
    causal = r_io >= c_io
    tri = jnp.where(causal, 1.0, 0.0)
    row = lax.broadcasted_iota(I32, (c, 1), 0)
    nch = max(1, tb // c)
    for ci, hh in [(ci, hh) for ci in range(nch) for hh in range(hp)]:
        cols = slice(hh * LANES, (hh + 1) * LANES)
        if tb >= c:
            rows = slice(ci * c, (ci + 1) * c)
            load = lambda ref: ref[rows, cols]
        else:
            load = lambda ref: _pad_rows(ref[:, cols], c)
        q, k, lf, v = load(q_ref), load(k_ref), load(lf_ref), load(v_ref)
        b = jnp.dot(tri, lf, preferred_element_type=F32, precision=lax.Precision.HIGHEST)
        st = st_sc[hh]
        o = _dot_nt((q * jnp.exp(b)).astype(BF16), st.astype(BF16))
        slabs = []
        for blk in range(c // HG_SUB):
            lo, hi = blk * HG_SUB, (blk + 1) * HG_SUB
            anchor = b[lo - 1:lo, :] if blk > 0 else jnp.zeros((1, LANES), F32)
            qb = q[lo:hi, :] * jnp.exp(b[lo:hi, :] - anchor)
            kb = k * jnp.exp(jnp.where(row < hi, anchor - b, -jnp.inf))
            slabs.append(_dot_nt(qb.astype(BF16), kb.astype(BF16)))
        a = jnp.where(causal, jnp.concatenate(slabs, axis=0), 0.0)
        o = o + _dot(a.astype(BF16), v.astype(BF16))
        b_last = b[c - 1:c, :]
        kd = k * jnp.exp(b_last - b)
        st_sc[hh] = st * jnp.exp(b_last) + _dot(v.T.astype(BF16), kd.astype(BF16))
        og = _rms(o, gn_ref[...])
        if tb >= c:
            o_ref[rows, cols] = (og * gs_ref[rows, cols]).astype(BF16)
        else:
            o_ref[:, cols] = (og[:tb, :] * gs_ref[:, cols]).astype(BF16)

    @pl.when(j == nt - 1)
    def _():
        for hh in range(hp):
            s_ref[hh] = st_sc[hh].T


def _hg_rec(q, k, lf, v, gs, s0, gn, bsz, t):
    r, d = q.shape
    heads = d // LANES
    hp = math.gcd(heads, HG_HP)
    tb = min(t, HG_TB)
    nt = t // tb
    blk = lambda: pl.BlockSpec((tb, hp * LANES), lambda b, h, j: (b * nt + j, h))
    state = lambda: pl.BlockSpec((None, hp, LANES, LANES), lambda b, h, j: (b, h, 0, 0))
    return pl.pallas_call(
        functools.partial(_hg_rec_kernel, tb=tb, nt=nt, hp=hp),
        grid=(bsz, heads // hp, nt),
        in_specs=[blk(), blk(), blk(), blk(), blk(), state(), _const_spec((1, LANES))],
        out_specs=[blk(), state()],
        out_shape=[jax.ShapeDtypeStruct((r, d), BF16),
                   jax.ShapeDtypeStruct((bsz, heads, LANES, LANES), F32)],
        scratch_shapes=[pltpu.VMEM((hp, LANES, LANES), F32)],
        compiler_params=_cparams("parallel", "parallel", "arbitrary"),
        name="hg_rec",
    )(q, k, lf, v, gs, s0, gn)


def kernel(x_prompt, x_sample, state_conv, cache_da_k, cache_da_v, cache_dsa_k, cache_dsa_v, cache_dsa_idx_k, state_hgrn, page_table, norm1_g, norm2_g, final_g, mlp_w1, mlp_w2, cv_w1, cv_b1, cv_dw, cv_dwb, cv_ln_g, cv_ln_b, cv_w2, cv_b2, da_w_in, da_lq1, da_lk1, da_lq2, da_lk2, da_subln_g, da_wo, dsa_w_in, dsa_wo, hg_w_in, hg_lb, hg_norm_g, hg_wo):
    bp, t, d = x_prompt.shape
    bd, tn, _ = x_sample.shape
    n_pages = page_table.shape[1]
    past_len = n_pages * PAGE
    depth = norm1_g.shape[0]
    assert d % LANES == 0 and t % ROW_TILE == 0 and tn & (tn - 1) == 0 and tn <= SUBLANES

    xp = x_prompt.reshape(bp * t, d)
    xs = x_sample.reshape(bd * tn, d)
    cos_p, sin_p = _rope_tables(jnp.arange(t))
    cos_s, sin_s = _rope_tables(past_len + jnp.arange(tn))
    cos_s, sin_s = jnp.tile(cos_s, (bd, 1)), jnp.tile(sin_s, (bd, 1))
    ksel_p = min(DSA_TOPK, t // 4)
    ksel_s = min(DSA_TOPK, (past_len + tn) // 4)
    zero_bias = jnp.zeros((1, d), F32)
    w1_all, w2_all = mlp_w1.astype(BF16), mlp_w2.astype(BF16)
    row1 = lambda a: a.reshape(1, -1)
    new = {n: [] for n in ('conv_p', 'conv_s', 'dak_p', 'dav_p', 'dak_s', 'dav_s', 'dsak_p', 'dsav_p', 'dsai_p',
                           'dsak_s', 'dsav_s', 'dsai_s', 'hg_p', 'hg_s')}

    for layer in range(depth):
        kind, j = layer % 4, layer // 4
        g1 = row1(norm1_g[layer])
        bo = zero_bias
        if kind == 0:
            pad = lambda a, n: jnp.concatenate([jnp.zeros(a.shape[:1] + (n,) + a.shape[2:], a.dtype), a], axis=1)
            dw = jnp.concatenate([cv_dw[j], jnp.zeros((CONV_HALO - CONV_W, d), F32)], axis=0)[:, None, :]
            cw = (g1, cv_w1[j].astype(BF16), row1(cv_b1[j]), dw, row1(cv_dwb[j]), row1(cv_ln_g[j]),
                  row1(cv_ln_b[j]))
            halo_pad = CONV_HALO - (CONV_W - 1)
            op, tail_p = _conv_mixer(xp.reshape(bp, t, d), jnp.zeros((bp, CONV_HALO, d), F32), *cw,
                                     nb=1, tt=CONV_TT)
            os_, tail_s = _conv_mixer(xs.reshape(bd, tn, d), pad(state_conv[j], halo_pad), *cw, nb=bd, tt=tn)
            op, os_ = op.reshape(bp * t, d), os_.reshape(bd * tn, d)
            new['conv_p'].append(tail_p[:, halo_pad:])
            new['conv_s'].append(tail_s[:, halo_pad:])
            wo, bo = cv_w2[j].astype(BF16), row1(cv_b2[j])
        elif kind == 1:
            lam_init = 0.8 - 0.6 * math.exp(-0.3 * layer)
            lams = (row1(da_lq1[j]), row1(da_lk1[j]), row1(da_lq2[j]), row1(da_lk2[j]))
            subg = row1(da_subln_g[j])
            w_in = da_w_in[j].astype(BF16)
            qp, kp, vp, kbp, vbp = _da_in(xp, g1, w_in, cos_p, sin_p)
            qs, ks, vs, _, _ = _da_in(xs, g1, w_in, cos_s, sin_s)
            op = _da_prompt(qp, kbp, vbp, lams, subg, bp, t, lam_init)
            os_ = _da_sample(page_table, qs.reshape(bd, tn, d), ks.reshape(bd, tn, d), vs.reshape(bd, tn, d),
                             cache_da_k, cache_da_v, lams, subg, j, lam_init).reshape(bd * tn, d)
            hd = (d // LANES, LANES)
            new['dak_p'].append(kp.reshape((bp, t) + hd))
            new['dav_p'].append(vp.reshape((bp, t) + hd))
            new['dak_s'].append(ks.reshape((bd, tn) + hd))
            new['dav_s'].append(vs.reshape((bd, tn) + hd))
            wo = da_wo[j].astype(BF16)
        elif kind == 2:
            w_in = dsa_w_in[j]
            w_in = jnp.concatenate([w_in, jnp.zeros((d, -w_in.shape[1] % LANES), F32)], axis=1).astype(BF16)
            qp, kp, vp, kbp, vbp, qip, mp_ = _dsa_in(xp, g1, w_in, cos_p, sin_p, seq_len=t)
            qs, ks, vs, _, _, qis, ms_ = _dsa_in(xs, g1, w_in, cos_s, sin_s)
            op = _dsa_prompt(qip, mp_, qp, kbp, vbp, bp, t, ksel_p)
            scores, scores_new = _dsa_sample_scores(
                page_table, qis.reshape(bd, tn, -1), ms_.reshape(bd, tn, LANES),
                jnp.transpose(cache_dsa_idx_k, (0, 1, 3, 2)), j)
            bias = _dsa_sample_select(scores, scores_new, ksel_s)
            os_ = _dsa_sample(page_table, bias, qs.reshape(bd, tn, d), ks.reshape(bd, tn, d),
                              vs.reshape(bd, tn, d), jnp.transpose(cache_dsa_k, (0, 1, 3, 4, 2)),
                              jnp.transpose(cache_dsa_v, (0, 1, 3, 4, 2)), j).reshape(bd * tn, d)
            hd = (d // HEAD64, HEAD64)
            new['dsak_p'].append(jnp.transpose(kp.reshape((bp,) + hd + (t,)), (0, 3, 1, 2)))
            new['dsav_p'].append(jnp.transpose(vp.reshape((bp,) + hd + (t,)), (0, 3, 1, 2)))
            new['dsai_p'].append(mp_[:, :HEAD64].reshape(bp, t, HEAD64))
            new['dsak_s'].append(ks.reshape((bd, tn) + hd))
            new['dsav_s'].append(vs.reshape((bd, tn) + hd))
            new['dsai_s'].append(ms_[:, :HEAD64].reshape(bd, tn, HEAD64))
            wo = dsa_wo[j].astype(BF16)
        else:
            w_in = hg_w_in[j].astype(BF16)
            gn = row1(hg_norm_g[j])
            heads = d // LANES
            hp = _hg_in(xp, g1, w_in, hg_lb, layer)
            hs = _hg_in(xs, g1, w_in, hg_lb, layer)
            op, sp = _hg_rec(*hp, jnp.zeros((bp, heads, LANES, LANES), F32), gn, bp, t)
            os_, ss = _hg_rec(*hs, state_hgrn[j], gn, bd, tn)
            new['hg_p'].append(sp.astype(state_hgrn.dtype))
            new['hg_s'].append(ss.astype(state_hgrn.dtype))
            wo = hg_wo[j].astype(BF16)

        final = layer == depth - 1
        post = (wo, bo, row1(norm2_g[layer]), w1_all, w2_all, row1(final_g))
        xp = _post(xp, op, *post, layer=layer, final=final)
        xs = _post(xs, os_, *post, layer=layer, final=final)

    return (xp.reshape(bp, t, d), xs.reshape(bd, tn, d),
            jnp.stack(new['conv_p']), jnp.stack(new['conv_s']),
            jnp.stack(new['dak_p']), jnp.stack(new['dav_p']), jnp.stack(new['dak_s']), jnp.stack(new['dav_s']),
            jnp.stack(new['dsak_p']), jnp.stack(new['dsav_p']), jnp.stack(new['dsai_p']),
            jnp.stack(new['dsak_s']), jnp.stack(new['dsav_s']), jnp.stack(new['dsai_s']),
            jnp.stack(new['hg_p']), jnp.stack(new['hg_s']))
```

```python
import functools
import math

import jax
import jax.numpy as jnp
from jax import lax
from jax.experimental import pallas as pl
from jax.experimental.pallas import tpu as pltpu

F32 = jnp.float32
BF16 = jnp.bfloat16
I32 = jnp.int32

NORM_EPS = 1e-6
LN_EPS = 1e-5
ROPE_THETA = 10000.0
ROPE_DIM = 64
CONV_W = 31
PAGE = 128
HEAD64 = 64
IDX_HEADS = 8
DSA_TOPK = 256

LANES = 128
SUBLANES = 8
CONV_HALO = 32
CONV_ROWS = 32
ROW_TILE = 512
CONV_TT = 256
ATT_TQ = 256
DA_HP = 4
DSA_TQ = 128
DSA_KV_FINE = 8
DSA_KV_COARSE = 4
SAMPLE_PP = 8
DSA_SAMPLE_PP = 16
SCORE_PP = 16
HG_C = 128
HG_SUB = 16
HG_TB = 512
HG_HP = 8
SELECT_NB = 8
MLP_FCHUNK = 1024
VMEM_LIMIT = 56 * 1024 * 1024
NEG_BIG = -1e30
INT_MIN = -2 ** 31


def _cparams(*sem):
    return pltpu.CompilerParams(dimension_semantics=sem, vmem_limit_bytes=VMEM_LIMIT)


def _const_spec(shape):
    nd = len(shape)
    return pl.BlockSpec(shape, lambda *_: (0,) * nd, pipeline_mode=pl.Buffered(1))


def _dot(a, b):
    return jnp.dot(a, b, preferred_element_type=F32)


def _dot_nt(a, b):
    return lax.dot_general(a, b, (((1,), (1,)), ((), ())), preferred_element_type=F32)


def _rms(x, g):
    return x * lax.rsqrt(jnp.mean(x * x, axis=-1, keepdims=True) + NORM_EPS) * g


def _sigmoid(x):
    return 1.0 / (1.0 + jnp.exp(-x))


def _lane_iota(shape=(1, LANES)):
    return lax.broadcasted_iota(I32, shape, len(shape) - 1)


def _rope(xc, cos, sin_signed):
    first_half = (_lane_iota() & 32) == 0
    partner = jnp.where(first_half, pltpu.roll(xc, 96, 1), pltpu.roll(xc, 32, 1))
    return xc * cos + partner * sin_signed


def _rope_tables(pos):
    inv = ROPE_THETA ** (-jnp.arange(0, ROPE_DIM, 2, dtype=F32) / ROPE_DIM)
    ang = pos.astype(F32)[:, None] * inv[None, :]
    cos, sin = jnp.cos(ang), jnp.sin(ang)
    return jnp.tile(cos, (1, 4)), jnp.tile(jnp.concatenate([-sin, sin], axis=1), (1, 2))


def _pad_rows(x, rows):
    return jnp.concatenate([x, jnp.zeros((rows - x.shape[0], x.shape[1]), x.dtype)], axis=0)


def _post_kernel(x_ref, o_ref, wo_ref, bo_ref, g_ref, w1_ref, w2_ref, fg_ref, y_ref, h_ref, *, final):
    x1 = x_ref[...] + _dot(o_ref[...], wo_ref[...]) + bo_ref[...]
    h = _rms(x1, g_ref[...]).astype(BF16)
    f = w1_ref.shape[1]
    fc = min(f, MLP_FCHUNK)
    for c in range(f // fc):
        a = jnp.maximum(_dot(h, w1_ref[:, c * fc:(c + 1) * fc]), 0.0)
        h_ref[:, c * fc:(c + 1) * fc] = (a * a).astype(BF16)
    y = x1 + _dot(h_ref[...], w2_ref[...])
    if final:
        y = _rms(y, fg_ref[...])
    y_ref[...] = y


def _post(x, o, wo, bo, g2, w1_all, w2_all, fg, layer, final):
    r, d = x.shape
    f = w1_all.shape[2]
    tm = min(r, ROW_TILE)
    row = lambda w: pl.BlockSpec((tm, w), lambda i: (i, 0))
    of_layer = lambda a, b: pl.BlockSpec((None, a, b), lambda i: (layer, 0, 0), pipeline_mode=pl.Buffered(1))
    return pl.pallas_call(
        functools.partial(_post_kernel, final=final),
        grid=(r // tm,),
        in_specs=[row(d), row(d), _const_spec((d, d)), _const_spec((1, d)), _const_spec((1, d)),
                  of_layer(d, f), of_layer(f, d), _const_spec((1, d))],
        out_specs=row(d),
        out_shape=jax.ShapeDtypeStruct((r, d), F32),
        scratch_shapes=[pltpu.VMEM((tm, f), BF16)],
        compiler_params=_cparams("parallel"),
        name="post_mlp",
    )(x, o, wo, bo, g2, w1_all, w2_all, fg)


def _conv_kernel(x_ref, buf_ref, g_ref, w1_ref, b1_ref, dw_ref, dwb_ref, lng_ref, lnb_ref,
                 o_ref, tail_ref, ext_ref, y_ref, sh_ref, *, nb, tt, d, carry, preshift):
    @pl.when(pl.program_id(1) == 0)
    def _():
        ext_ref[:, 0:CONV_HALO, :] = buf_ref[...]

    x = x_ref[...].reshape(nb * tt, d)
    h = _rms(x, g_ref[...]).astype(BF16)
    ag = _dot(h, w1_ref[...]) + b1_ref[...]
    u = ag[:, :d] * _sigmoid(ag[:, d:])
    ext_ref[:, CONV_HALO:CONV_HALO + tt, :] = u.reshape(nb, tt, d)

    if preshift:
        for a in range(1, SUBLANES):
            sh_ref[a - 1] = ext_ref[0, a:a + sh_ref.shape[1], :]

    rs = min(tt, CONV_ROWS)
    for r in range(tt // rs):
        acc = jnp.zeros((nb, rs, d), F32)
        for k in range(CONV_W):
            off = k + CONV_HALO - (CONV_W - 1)
            a = off % SUBLANES
            if preshift and a:
                start = r * rs + off - a
                window = sh_ref[a - 1, start:start + rs, :][None]
            else:
                window = ext_ref[:, r * rs + off:r * rs + off + rs, :]
            acc = acc + dw_ref[k] * window
        y_ref[:, r * rs:(r + 1) * rs, :] = acc + dwb_ref[...]

    y = y_ref[...]
    mu = jnp.mean(y, axis=-1, keepdims=True)
    yc = y - mu
    var = jnp.mean(yc * yc, axis=-1, keepdims=True)
    z = yc * lax.rsqrt(var + LN_EPS) * lng_ref[...] + lnb_ref[...]
    o_ref[...] = (z * _sigmoid(z)).astype(BF16)
    tail = ext_ref[:, tt:tt + CONV_HALO, :]
    tail_ref[...] = tail
    if carry:
        ext_ref[:, 0:CONV_HALO, :] = tail


def _conv_mixer(x3, buf, g1, w1, b1, dw, dwb, lng, lnb, nb, tt):
    b, t, d = x3.shape
    nt = t // tt
    preshift = nb == 1 and tt > CONV_ROWS
    sh_shape = (SUBLANES - 1, tt + CONV_HALO - SUBLANES, d) if preshift else (1, SUBLANES, LANES)
    kern = functools.partial(_conv_kernel, nb=nb, tt=tt, d=d, carry=nt > 1, preshift=preshift)
    return pl.pallas_call(
        kern,
        grid=(b // nb, nt),
        in_specs=[pl.BlockSpec((nb, tt, d), lambda i, j: (i, j, 0)),
                  pl.BlockSpec((nb, CONV_HALO, d), lambda i, j: (i, 0, 0)),
                  _const_spec((1, d)), _const_spec((d, 2 * d)), _const_spec((1, 2 * d)),
                  _const_spec((CONV_HALO, 1, d)), _const_spec((1, d)), _const_spec((1, d)),
                  _const_spec((1, d))],
        out_specs=[pl.BlockSpec((nb, tt, d), lambda i, j: (i, j, 0)),
                   pl.BlockSpec((nb, CONV_HALO, d), lambda i, j: (i, 0, 0))],
        out_shape=[jax.ShapeDtypeStruct((b, t, d), BF16),
                   jax.ShapeDtypeStruct((b, CONV_HALO, d), F32)],
        scratch_shapes=[pltpu.VMEM((nb, CONV_HALO + tt, d), F32), pltpu.VMEM((nb, tt, d), F32),
                        pltpu.VMEM(sh_shape, F32)],
        compiler_params=_cparams("parallel", "arbitrary"),
        name="conv_mixer",
    )(x3, buf, g1, w1, b1, dw, dwb, lng, lnb)


def _da_in_kernel(x_ref, g_ref, w_ref, cos_ref, sin_ref, q_ref, k_ref, v_ref, kb_ref, vb_ref, *, d):
    h = _rms(x_ref[...], g_ref[...]).astype(BF16)
    y = _dot(h, w_ref[...])
    cos, sin = cos_ref[...], sin_ref[...]
    for c in range(d // LANES):
        sl = slice(c * LANES, (c + 1) * LANES)
        q = _rope(y[:, c * LANES:(c + 1) * LANES], cos, sin)
        q_ref[:, sl] = (q * HEAD64 ** -0.5).astype(BF16)
        k = _rope(y[:, d + c * LANES:d + (c + 1) * LANES], cos, sin)
        k_ref[:, sl] = k
        kb_ref[:, sl] = k.astype(BF16)
        v = y[:, 2 * d + c * LANES:2 * d + (c + 1) * LANES]
        v_ref[:, sl] = v
        vb_ref[:, sl] = v.astype(BF16)


def _da_in(x, g1, w, cos, sin):
    r, d = x.shape
    tm = min(r, ROW_TILE)
    ntab = cos.shape[0] // tm
    row = lambda: pl.BlockSpec((tm, d), lambda i: (i, 0))
    tab = lambda: pl.BlockSpec((tm, LANES), lambda i: (i % ntab, 0))
    return pl.pallas_call(
        functools.partial(_da_in_kernel, d=d),
        grid=(r // tm,),
        in_specs=[row(), _const_spec((1, d)), _const_spec((d, 3 * d)), tab(), tab()],
        out_specs=[row(), row(), row(), row(), row()],
        out_shape=[jax.ShapeDtypeStruct((r, d), BF16), jax.ShapeDtypeStruct((r, d), F32),
                   jax.ShapeDtypeStruct((r, d), F32), jax.ShapeDtypeStruct((r, d), BF16),
                   jax.ShapeDtypeStruct((r, d), BF16)],
        compiler_params=_cparams("parallel"),
        name="da_in",
    )(x, g1, w, cos, sin)


def _da_lambda(lq1, lk1, lq2, lk2, lam_init):
    return (jnp.exp(jnp.sum(lq1[...] * lk1[...], axis=-1, keepdims=True))
            - jnp.exp(jnp.sum(lq2[...] * lk2[...], axis=-1, keepdims=True)) + lam_init)


def _da_prompt_kernel(lq1, lk1, lq2, lk2, subg_ref, q_ref, k_ref, v_ref, o_ref, *, tq, nq, hp, lam_init):
    i = pl.program_id(2)
    lam = _da_lambda(lq1, lk1, lq2, lk2, lam_init)
    lane = _lane_iota()
    keep = (lax.broadcasted_iota(I32, (tq, tq), 0) >= lax.broadcasted_iota(I32, (tq, tq), 1))

    def body(c, hh):
        off = c * tq
        cols = slice(hh * LANES, (hh + 1) * LANES)
        q = q_ref[:, cols]
        zero = jnp.zeros_like(q)
        q1 = jnp.where(lane < HEAD64, q, zero)
        q2 = jnp.where(lane >= HEAD64, q, zero)

        def softmax_parts(qm):
            sd = jnp.where(keep, _dot_nt(qm, k_ref[off:off + tq, cols]), NEG_BIG)
            m = jnp.max(sd, axis=-1, keepdims=True)
            so = None
            if c > 0:
                so = _dot_nt(qm, k_ref[0:off, cols])
                m = jnp.maximum(m, jnp.max(so, axis=-1, keepdims=True))
            pd = jnp.exp(sd - m)
            l = jnp.sum(pd, axis=-1, keepdims=True)
            po = None
            if c > 0:
                po = jnp.exp(so - m)
                l = l + jnp.sum(po, axis=-1, keepdims=True)
            return pd, po, l

        pd1, po1, l1 = softmax_parts(q1)
        pd2, po2, l2 = softmax_parts(q2)
        w1 = 1.0 / l1
        w2 = lam / l2
        o = _dot((pd1 * w1 - pd2 * w2).astype(BF16), v_ref[off:off + tq, cols])
        if c > 0:
            o = o + _dot((po1 * w1 - po2 * w2).astype(BF16), v_ref[0:off, cols])
        o_ref[:, cols] = (_rms(o, subg_ref[...]) * (1.0 - lam_init)).astype(BF16)

    def block(c):
        for hh in range(hp):
            body(c, hh)

    for c in range(nq):
        pl.when(i == c)(functools.partial(block, c))


def _da_prompt(q, kb, vb, lams, subg, bsz, t, lam_init):
    r, d = q.shape
    tq = min(t, ATT_TQ)
    nq = t // tq
    hp = math.gcd(d // LANES, DA_HP)
    lam_spec = _const_spec((1, HEAD64))
    return pl.pallas_call(
        functools.partial(_da_prompt_kernel, tq=tq, nq=nq, hp=hp, lam_init=lam_init),
        grid=(bsz, d // (hp * LANES), nq),
        in_specs=[lam_spec, lam_spec, lam_spec, lam_spec, _const_spec((1, LANES)),
                  pl.BlockSpec((tq, hp * LANES), lambda b, h, i: (b * nq + i, h)),
                  pl.BlockSpec((t, hp * LANES), lambda b, h, i: (b, h)),
                  pl.BlockSpec((t, hp * LANES), lambda b, h, i: (b, h))],
        out_specs=pl.BlockSpec((tq, hp * LANES), lambda b, h, i: (b * nq + i, h)),
        out_shape=jax.ShapeDtypeStruct((r, d), BF16),
        compiler_params=_cparams("parallel", "parallel", "arbitrary"),
        name="da_prompt",
    )(*lams, subg, q, kb, vb)


def _online_update(s, valid, pv_fn, m_sc, l_sc, acc_sc):
    m_old = m_sc[...]
    m_new = jnp.maximum(m_old, jnp.max(s, axis=-1, keepdims=True))
    alpha = jnp.exp(m_old - m_new)
    p = jnp.exp(s - m_new)
    if valid is not None:
        p = jnp.where(valid, p, 0.0)
    l_sc[...] = alpha * l_sc[...] + jnp.sum(p, axis=-1, keepdims=True)
    acc_sc[...] = alpha * acc_sc[...] + pv_fn(p.astype(BF16))
    m_sc[...] = m_new


def _da_sample_kernel(pt_ref, lq1, lk1, lq2, lk2, subg_ref, q_ref, kn_ref, vn_ref, *rest,
                      tn, d, lam_init, nsteps, pp):
    del pt_ref
    kp_refs, vp_refs = rest[:pp], rest[pp:2 * pp]
    o_ref, qh_sc, own_sc, m_sc, l_sc, acc_sc = rest[2 * pp:]
    step = pl.program_id(1)
    heads = d // LANES
    gr = 2 * tn
    rows = heads * gr
    ncol = pp * PAGE * heads

    @pl.when(step == 0)
    def _():
        q = q_ref[0].astype(F32)
        first_map = lax.broadcasted_iota(I32, (gr, LANES), 0) < tn
        lo_half = _lane_iota((gr, LANES)) < HEAD64
        for h in range(heads):
            qc = q[:, h * LANES:(h + 1) * LANES]
            qh_sc[h * gr:(h + 1) * gr, :] = jnp.where(
                first_map == lo_half, jnp.concatenate([qc, qc], axis=0), 0.0).astype(BF16)
        row_head = lax.broadcasted_iota(I32, (rows, ncol), 0) >> int(math.log2(gr))
        col_head = lax.broadcasted_iota(I32, (rows, ncol), 1) & (heads - 1)
        own_sc[...] = jnp.where(row_head == col_head, 0.0, NEG_BIG)
        m_sc[...] = jnp.full((rows, 1), NEG_BIG, F32)
        l_sc[...] = jnp.zeros((rows, 1), F32)
        acc_sc[...] = jnp.zeros((rows, LANES), F32)

    flat = lambda refs: jnp.concatenate([r[...] for r in refs], axis=0).astype(BF16)
    s = _dot_nt(qh_sc[...], flat(kp_refs)) + own_sc[...]
    _online_update(s, None, lambda pb: _dot(pb, flat(vp_refs)), m_sc, l_sc, acc_sc)

    @pl.when(step == nsteps - 1)
    def _():
        fresh = lambda ref, h: _pad_rows(ref[0][:, h * LANES:(h + 1) * LANES], PAGE).astype(BF16)
        qpos = lax.broadcasted_iota(I32, (rows, PAGE), 0) & (tn - 1)
        valid = lax.broadcasted_iota(I32, (rows, PAGE), 1) <= qpos
        s_new = jnp.concatenate(
            [_dot_nt(qh_sc[h * gr:(h + 1) * gr, :], fresh(kn_ref, h)) for h in range(heads)], axis=0)
        pv = lambda pb: jnp.concatenate(
            [_dot(pb[h * gr:(h + 1) * gr, :], fresh(vn_ref, h)) for h in range(heads)], axis=0)
        _online_update(jnp.where(valid, s_new, NEG_BIG), valid, pv, m_sc, l_sc, acc_sc)
        lam = _da_lambda(lq1, lk1, lq2, lk2, lam_init)
        for h in range(heads):
            r1 = slice(h * gr, h * gr + tn)
            r2 = slice(h * gr + tn, (h + 1) * gr)
            o = acc_sc[r1, :] / l_sc[r1, :] - lam * (acc_sc[r2, :] / l_sc[r2, :])
            o_ref[0, :, h * LANES:(h + 1) * LANES] = (_rms(o, subg_ref[...]) * (1.0 - lam_init)).astype(BF16)


def _da_sample(page_table, q3, kn3, vn3, k_pool, v_pool, lams, subg, layer, lam_init):
    bd, tn, d = q3.shape
    n_pages = page_table.shape[1]
    heads = d // LANES
    pp = math.gcd(n_pages, SAMPLE_PP)
    nsteps = n_pages // pp
    rows = heads * 2 * tn
    assert heads & (heads - 1) == 0
    flat_shape = k_pool.shape[:2] + (PAGE * heads, LANES)
    k_pool, v_pool = k_pool.reshape(flat_shape), v_pool.reshape(flat_shape)
    seq = lambda: pl.BlockSpec((1, tn, d), lambda b, s, pt: (b, 0, 0))
    pool = lambda i: pl.BlockSpec((None, None, PAGE * heads, LANES),
                                  lambda b, s, pt: (layer, pt[b, s * pp + i], 0, 0))
    lam_spec = _const_spec((1, HEAD64))
    kern = functools.partial(_da_sample_kernel, tn=tn, d=d, lam_init=lam_init, nsteps=nsteps, pp=pp)
    return pl.pallas_call(
        kern,
        grid_spec=pltpu.PrefetchScalarGridSpec(
            num_scalar_prefetch=1,
            grid=(bd, nsteps),
            in_specs=[lam_spec, lam_spec, lam_spec, lam_spec, _const_spec((1, LANES)), seq(), seq(), seq()]
            + [pool(i) for i in range(pp)] + [pool(i) for i in range(pp)],
            out_specs=seq(),
            scratch_shapes=[pltpu.VMEM((rows, LANES), BF16), pltpu.VMEM((rows, pp * PAGE * heads), F32),
                            pltpu.VMEM((rows, 1), F32), pltpu.VMEM((rows, 1), F32),
                            pltpu.VMEM((rows, LANES), F32)]),
        out_shape=jax.ShapeDtypeStruct((bd, tn, d), BF16),
        compiler_params=_cparams("parallel", "arbitrary"),
        name="da_sample",
    )(page_table, *lams, subg, q3, kn3, vn3, *([k_pool] * pp), *([v_pool] * pp))


def _dsa_in_kernel(x_ref, g_ref, w_ref, cos_ref, sin_ref, q_ref, k_ref, v_ref, kb_ref, vb_ref,
                   qi_ref, misc_ref, *, d, token_minor):
    h = _rms(x_ref[...], g_ref[...]).astype(BF16)
    y = _dot(h, w_ref[...])
    cos, sin = cos_ref[...], sin_ref[...]
    for c in range(d // LANES):
        sl = slice(c * LANES, (c + 1) * LANES)
        q = _rope(y[:, c * LANES:(c + 1) * LANES], cos, sin)
        q_ref[:, sl] = (q * HEAD64 ** -0.5).astype(BF16)
        k = _rope(y[:, d + c * LANES:d + (c + 1) * LANES], cos, sin)
        kb_ref[:, sl] = k.astype(BF16)
        v = y[:, 2 * d + c * LANES:2 * d + (c + 1) * LANES]
        vb_ref[:, sl] = v.astype(BF16)
        if token_minor:
            k_ref[sl, :] = k.T
            v_ref[sl, :] = v.T
        else:
            k_ref[:, sl] = k
            v_ref[:, sl] = v
    for c in range(IDX_HEADS * HEAD64 // LANES):
        qi = _rope(y[:, 3 * d + c * LANES:3 * d + (c + 1) * LANES], cos, sin)
        qi_ref[:, c * LANES:(c + 1) * LANES] = (qi * HEAD64 ** -0.5).astype(BF16)
    is_key = _lane_iota() < HEAD64
    base = 3 * d + IDX_HEADS * HEAD64
    misc_ref[...] = _rope(y[:, base:base + LANES], jnp.where(is_key, cos, 1.0), jnp.where(is_key, sin, 0.0))


def _dsa_in(x, g1, w, cos, sin, seq_len=None):
    r, d = x.shape
    tm = min(r, ROW_TILE)
    ntab = cos.shape[0] // tm
    nqi = IDX_HEADS * HEAD64
    row = lambda w_: pl.BlockSpec((tm, w_), lambda i: (i, 0))
    tab = lambda: pl.BlockSpec((tm, LANES), lambda i: (i % ntab, 0))
    if seq_len is None:
        kv_spec, kv_shape = row(d), jax.ShapeDtypeStruct((r, d), F32)
    else:
        nt = seq_len // tm
        kv_spec = pl.BlockSpec((None, d, tm), lambda i: (i // nt, 0, i % nt))
        kv_shape = jax.ShapeDtypeStruct((r // seq_len, d, seq_len), F32)
    return pl.pallas_call(
        functools.partial(_dsa_in_kernel, d=d, token_minor=seq_len is not None),
        grid=(r // tm,),
        in_specs=[row(d), _const_spec((1, d)), _const_spec(w.shape), tab(), tab()],
        out_specs=[row(d), kv_spec, kv_spec, row(d), row(d), row(nqi), row(LANES)],
        out_shape=[jax.ShapeDtypeStruct((r, d), BF16), kv_shape, kv_shape,
                   jax.ShapeDtypeStruct((r, d), BF16),
                   jax.ShapeDtypeStruct((r, d), BF16), jax.ShapeDtypeStruct((r, nqi), BF16),
                   jax.ShapeDtypeStruct((r, LANES), F32)],
        compiler_params=_cparams("parallel"),
        name="dsa_in",
    )(x, g1, w, cos, sin)


def _sortable_key(score):
    bits = lax.bitcast_convert_type(score + 0.0, I32)
    return bits ^ ((bits >> 31) & 0x7FFFFFFF)


def _select_topk(key, pos, krow, count, pos_bits):
    def try_bit(ans, bit):
        cand = ans | lax.shift_left(jnp.int32(1), bit)
        return jnp.where(count(key >= cand) >= krow, cand, ans)

    def two_bit_step(it, ans):
        lo = lax.shift_left(jnp.int32(1), 28 - 2 * it)
        c1, c2, c3 = ans | lo, ans | (lo + lo), ans | (lo + lo) | lo
        n1, n2, n3 = count(key >= c1), count(key >= c2), count(key >= c3)
        return jnp.where(n3 >= krow, c3, jnp.where(n2 >= krow, c2, jnp.where(n1 >= krow, c1, ans)))

    ans = jnp.where(count(key >= 0) >= krow, jnp.int32(0), jnp.int32(INT_MIN))
    ans = try_bit(ans, 30)
    ans = lax.fori_loop(0, 15, two_bit_step, ans)
    gt = key > ans
    eq = key == ans
    need = krow - count(gt)

    def pos_step(it, cut):
        cand = cut | lax.shift_left(jnp.int32(1), pos_bits - 1 - it)
        return jnp.where(count(eq & (pos < cand)) < need, cand, cut)

    surplus = jnp.max(count(eq) - need) > 0.0
    cut = lax.cond(surplus,
                   lambda: lax.fori_loop(0, pos_bits, pos_step, jnp.zeros(krow.shape, I32)),
                   lambda: jnp.full(krow.shape, 2 ** pos_bits, I32))
    return gt | (eq & (pos <= cut))


def _dsa_prompt_kernel(qi_ref, wq_ref, kim_ref, q_ref, k_ref, v_ref, o_ref, kid_sc, *, tq, nq, d, ksel):
    i = pl.program_id(1)
    lane = _lane_iota()
    lo_half = lane < HEAD64

    @pl.when(i == 0)
    def _():
        lo = jnp.where(lo_half, kim_ref[...], 0.0)
        kid_sc[...] = (lo + pltpu.roll(lo, HEAD64, 1)).astype(BF16)

    def body(kv):
        wt = wq_ref[...] * IDX_HEADS ** -0.5
        kid = kid_sc[0:kv, :]
        score = jnp.zeros((tq, kv), F32)
        for c in range(IDX_HEADS // 2):
            qc = qi_ref[:, c * LANES:(c + 1) * LANES]
            zero = jnp.zeros_like(qc)
            dots = _dot_nt(jnp.concatenate([jnp.where(lo_half, qc, zero), jnp.where(lo_half, zero, qc)], axis=0),
                           kid)
            for half in range(2):
                h = 2 * c + half
                score = score + wt[:, HEAD64 + h:HEAD64 + h + 1] * jnp.maximum(dots[half * tq:(half + 1) * tq, :], 0.0)

        qpos = i * tq + lax.broadcasted_iota(I32, (tq, 1), 0)
        kpos = lax.broadcasted_iota(I32, (tq, kv), 1)
        key = jnp.where(kpos <= qpos, _sortable_key(score), INT_MIN)
        krow = jnp.minimum(qpos + 1, ksel).astype(F32)
        count = lambda mask: jnp.sum(jnp.where(mask, 1.0, 0.0), axis=-1, keepdims=True)
        sel = _select_topk(key, kpos, krow, count, max(1, (kv - 1).bit_length()))
        bias = jnp.where(sel, 0.0, NEG_BIG)
        bias2 = jnp.concatenate([bias, bias], axis=0)
        ones_blk = jnp.ones((kv, LANES), BF16)

        for c in range(d // LANES):
            cols = slice(c * LANES, (c + 1) * LANES)
            qc = q_ref[:, cols]
            kc = k_ref[0:kv, cols]
            vc = v_ref[0:kv, cols]
            zero = jnp.zeros_like(qc)
            qq = jnp.concatenate([jnp.where(lo_half, qc, zero), jnp.where(lo_half, zero, qc)], axis=0)
            s = _dot_nt(qq, kc) + bias2
            p = jnp.exp(s - jnp.max(s, axis=-1, keepdims=True))
            ov = _dot(p.astype(BF16), jnp.concatenate([vc, ones_blk], axis=1))
            o2 = ov[:, :LANES] / ov[:, LANES:LANES + 1]
            o_ref[:, cols] = jnp.where(lo_half, o2[:tq, :], o2[tq:, :]).astype(BF16)

    fine, coarse = max(1, nq // DSA_KV_FINE), max(1, nq // DSA_KV_COARSE)
    bounds = list(range(fine, nq // 2 + 1, fine)) + list(range(nq // 2 + coarse, nq + 1, coarse))
    for lo, hi in zip([0] + bounds[:-1], bounds):
        pl.when((i >= lo) & (i < hi))(functools.partial(body, hi * tq))


def _dsa_prompt(qi, misc, q, kb, vb, bsz, t, ksel):
    r, d = q.shape
    tq = min(t, DSA_TQ)
    nq = t // tq
    nqi = qi.shape[1]
    qrow = lambda w: pl.BlockSpec((tq, w), lambda b, i: (b * nq + i, 0))
    seq = lambda w: pl.BlockSpec((t, w), lambda b, i: (b, 0))
    return pl.pallas_call(
        functools.partial(_dsa_prompt_kernel, tq=tq, nq=nq, d=d, ksel=ksel),
        grid=(bsz, nq),
        in_specs=[qrow(nqi), qrow(LANES), seq(LANES), qrow(d), seq(d), seq(d)],
        out_specs=qrow(d),
        out_shape=jax.ShapeDtypeStruct((r, d), BF16),
        scratch_shapes=[pltpu.VMEM((t, LANES), BF16)],
        compiler_params=_cparams("parallel", "arbitrary"),
        name="dsa_prompt",
    )(qi, misc, misc, q, kb, vb)


def _dsa_sample_scores_kernel(pt_ref, qi_ref, misc_ref, *rest, tn, nsteps, pp):
    del pt_ref
    pool_refs = rest[:pp]
    sc_ref, scn_ref, qst_sc, w_sc = rest[pp:]
    step = pl.program_id(1)

    @pl.when(step == 0)
    def _():
        qi = qi_ref[0].astype(F32)
        misc = misc_ref[0]
        qst_sc[...] = jnp.concatenate(
            [qi[:, h * HEAD64:(h + 1) * HEAD64] for h in range(IDX_HEADS)], axis=0).astype(BF16)
        w_sc[...] = jnp.concatenate(
            [misc[:, HEAD64 + h:HEAD64 + h + 1] for h in range(IDX_HEADS)], axis=0) * IDX_HEADS ** -0.5

    def head_sum(dots):
        sc = w_sc[...] * jnp.maximum(dots, 0.0)
        out = sc[0:tn, :]
        for h in range(1, IDX_HEADS):
            out = out + sc[h * tn:(h + 1) * tn, :]
        return out

    for i in range(pp):
        sc_ref[0, i] = head_sum(_dot(qst_sc[...], pool_refs[i][...].astype(BF16)))

    @pl.when(step == nsteps - 1)
    def _():
        fresh = _pad_rows(misc_ref[0][:, :HEAD64], PAGE).astype(BF16)
        scn_ref[0] = head_sum(_dot_nt(qst_sc[...], fresh))


def _dsa_sample_scores(page_table, qi3, misc3, idx_pool_t, layer):
    bd, tn, nqi = qi3.shape
    n_pages = page_table.shape[1]
    pp = math.gcd(n_pages, SCORE_PP)
    nsteps = n_pages // pp
    pool = lambda i: pl.BlockSpec((None, None, HEAD64, PAGE),
                                  lambda b, s, pt: (layer, pt[b, s * pp + i], 0, 0))
    kern = functools.partial(_dsa_sample_scores_kernel, tn=tn, nsteps=nsteps, pp=pp)
    return pl.pallas_call(
        kern,
        grid_spec=pltpu.PrefetchScalarGridSpec(
            num_scalar_prefetch=1,
            grid=(bd, nsteps),
            in_specs=[pl.BlockSpec((1, tn, nqi), lambda b, s, pt: (b, 0, 0)),
                      pl.BlockSpec((1, tn, LANES), lambda b, s, pt: (b, 0, 0))]
            + [pool(i) for i in range(pp)],
            out_specs=[pl.BlockSpec((1, pp, tn, PAGE), lambda b, s, pt: (b, s, 0, 0)),
                       pl.BlockSpec((1, tn, PAGE), lambda b, s, pt: (b, 0, 0))],
            scratch_shapes=[pltpu.VMEM((IDX_HEADS * tn, HEAD64), BF16),
                            pltpu.VMEM((IDX_HEADS * tn, 1), F32)]),
        out_shape=[jax.ShapeDtypeStruct((bd, n_pages, tn, PAGE), F32),
                   jax.ShapeDtypeStruct((bd, tn, PAGE), F32)],
        compiler_params=_cparams("parallel", "arbitrary"),
        name="dsa_sample_scores",
    )(page_table, qi3, misc3, *([idx_pool_t] * pp))


def _block_diag_queries(q, groups, tn):
    d = q.shape[1]
    qt = jnp.concatenate([q] * groups, axis=0)
    shift = int(math.log2(tn))
    rg = lax.broadcasted_iota(I32, (groups * tn, d), 0) >> shift
    cg = lax.broadcasted_iota(I32, (groups * tn, d), 1) >> 6
    return jnp.where(rg == cg, qt, 0.0).astype(BF16)


def _dsa_sample_select_kernel(sc_ref, scn_ref, bias_ref, *, nb, tn, n_pages, ksel):
    shape = (nb, n_pages + 1, tn, PAGE)
    page = lax.broadcasted_iota(I32, shape, 1)
    qidx = lax.broadcasted_iota(I32, shape, 2)
    lane = lax.broadcasted_iota(I32, shape, 3)
    score = jnp.concatenate([sc_ref[...], scn_ref[...].reshape(nb, 1, tn, PAGE)], axis=1)
    key = jnp.where((page < n_pages) | (lane <= qidx), _sortable_key(score), INT_MIN)
    krow = jnp.full((nb, 1, tn, 1), float(ksel), F32)
    count = lambda mask: jnp.sum(jnp.sum(jnp.where(mask, 1.0, 0.0), axis=1, keepdims=True),
                                 axis=-1, keepdims=True)
    sel = _select_topk(key, page * PAGE + lane, krow, count, ((n_pages + 1) * PAGE - 1).bit_length())
    bias_ref[...] = jnp.where(sel, 0.0, NEG_BIG)


def _dsa_sample_select(scores, scores_new, ksel):
    bd, n_pages, tn, _ = scores.shape
    nb = math.gcd(bd, SELECT_NB)
    return pl.pallas_call(
        functools.partial(_dsa_sample_select_kernel, nb=nb, tn=tn, n_pages=n_pages, ksel=ksel),
        grid=(bd // nb,),
        in_specs=[pl.BlockSpec((nb, n_pages, tn, PAGE), lambda i: (i, 0, 0, 0)),
                  pl.BlockSpec((nb, tn, PAGE), lambda i: (i, 0, 0))],
        out_specs=pl.BlockSpec((nb, n_pages + 1, tn, PAGE), lambda i: (i, 0, 0, 0)),
        out_shape=jax.ShapeDtypeStruct((bd, n_pages + 1, tn, PAGE), F32),
        compiler_params=_cparams("parallel"),
        name="dsa_sample_select",
    )(scores, scores_new)


def _dsa_sample_kernel(pt_ref, bias_ref, q_ref, kn_ref, vn_ref, *rest, tn, d, n_pages, nsteps, pp):
    del pt_ref
    kp_refs, vp_refs = rest[:pp], rest[pp:2 * pp]
    o_ref, qbd_sc, m_sc, l_sc, acc_sc = rest[2 * pp:]
    step = pl.program_id(1)
    groups = d // HEAD64
    rows = groups * tn

    @pl.when(step == 0)
    def _():
        qbd_sc[...] = _block_diag_queries(q_ref[0].astype(F32), groups, tn)
        m_sc[...] = jnp.full((rows, 1), NEG_BIG, F32)
        l_sc[...] = jnp.zeros((rows, 1), F32)
        acc_sc[...] = jnp.zeros((rows, d), F32)

    def tiled_bias(page):
        return jnp.concatenate([bias_ref[0, page]] * groups, axis=0)

    @pl.when(step < nsteps)
    def _():
        bias = jnp.concatenate([tiled_bias(step * pp + i) for i in range(pp)], axis=1)
        s = jnp.concatenate([_dot(qbd_sc[...], r[...].reshape(d, PAGE).astype(BF16)) for r in kp_refs],
                            axis=1) + bias
        pv = lambda pb: sum(_dot_nt(pb[:, i * PAGE:(i + 1) * PAGE], vp_refs[i][...].reshape(d, PAGE).astype(BF16))
                            for i in range(pp))
        _online_update(s, bias == 0.0, pv, m_sc, l_sc, acc_sc)

    @pl.when(step == nsteps)
    def _():
        bias = tiled_bias(n_pages)
        s = _dot_nt(qbd_sc[...], _pad_rows(kn_ref[0], PAGE).astype(BF16)) + bias
        pv = lambda pb: _dot(pb, _pad_rows(vn_ref[0], PAGE).astype(BF16))
        _online_update(s, bias == 0.0, pv, m_sc, l_sc, acc_sc)
        lo_half = _lane_iota() < HEAD64
        for c in range(d // LANES):
            cols = slice(c * LANES, (c + 1) * LANES)
            r0 = slice(2 * c * tn, (2 * c + 1) * tn)
            r1 = slice((2 * c + 1) * tn, (2 * c + 2) * tn)
            o = jnp.where(lo_half, acc_sc[r0, cols] / l_sc[r0, :], acc_sc[r1, cols] / l_sc[r1, :])
            o_ref[0, :, cols] = o.astype(BF16)


def _dsa_sample(page_table, bias, q3, kn3, vn3, k_pool_t, v_pool_t, layer):
    bd, tn, d = q3.shape
    n_pages = page_table.shape[1]
    groups = d // HEAD64
    rows = groups * tn
    pp = math.gcd(n_pages, DSA_SAMPLE_PP)
    nsteps = n_pages // pp
    seq = lambda: pl.BlockSpec((1, tn, d), lambda b, s, pt: (b, 0, 0))
    pool = lambda i: pl.BlockSpec(
        (None, None, groups, HEAD64, PAGE),
        lambda b, s, pt: (layer, pt[b, jnp.minimum(s, nsteps - 1) * pp + i], 0, 0, 0))
    kern = functools.partial(_dsa_sample_kernel, tn=tn, d=d, n_pages=n_pages, nsteps=nsteps, pp=pp)
    return pl.pallas_call(
        kern,
        grid_spec=pltpu.PrefetchScalarGridSpec(
            num_scalar_prefetch=1,
            grid=(bd, nsteps + 1),
            in_specs=[pl.BlockSpec((1, n_pages + 1, tn, PAGE), lambda b, s, pt: (b, 0, 0, 0)),
                      seq(), seq(), seq()] + [pool(i) for i in range(pp)] + [pool(i) for i in range(pp)],
            out_specs=seq(),
            scratch_shapes=[pltpu.VMEM((rows, d), BF16), pltpu.VMEM((rows, 1), F32),
                            pltpu.VMEM((rows, 1), F32), pltpu.VMEM((rows, d), F32)]),
        out_shape=jax.ShapeDtypeStruct((bd, tn, d), BF16),
        compiler_params=_cparams("parallel", "arbitrary"),
        name="dsa_sample",
    )(page_table, bias, q3, kn3, vn3, *([k_pool_t] * pp), *([v_pool_t] * pp))


def _hg_in_kernel(x_ref, g_ref, w_ref, lbw_ref, q_ref, k_ref, lf_ref, v_ref, gs_ref, *, d, layer):
    h = _rms(x_ref[...], g_ref[...]).astype(BF16)
    y = _dot(h, w_ref[...])
    lbw = lbw_ref[...]
    e = jnp.exp(lbw - jnp.max(lbw, axis=0, keepdims=True))
    sm = e / jnp.sum(e, axis=0, keepdims=True)
    lb = jnp.sum(sm[1:layer + 1, :], axis=0, keepdims=True)
    q, fz, v, g = y[:, :d], y[:, d:2 * d], y[:, 2 * d:3 * d], y[:, 3 * d:]
    f = lb + (1.0 - lb) * _sigmoid(fz)
    q_ref[...] = q * _sigmoid(q)
    k_ref[...] = 1.0 - f
    lf_ref[...] = jnp.log(f)
    v_ref[...] = v
    gs_ref[...] = g * _sigmoid(g)


def _hg_in(x, g1, w, lbw, layer):
    r, d = x.shape
    tm = min(r, ROW_TILE)
    row = lambda: pl.BlockSpec((tm, d), lambda i: (i, 0))
    return pl.pallas_call(
        functools.partial(_hg_in_kernel, d=d, layer=layer),
        grid=(r // tm,),
        in_specs=[row(), _const_spec((1, d)), _const_spec((d, 4 * d)), _const_spec(lbw.shape)],
        out_specs=[row()] * 5,
        out_shape=[jax.ShapeDtypeStruct((r, d), F32)] * 5,
        compiler_params=_cparams("parallel"),
        name="hg_in",
    )(x, g1, w, lbw)


def _hg_rec_kernel(q_ref, k_ref, lf_ref, v_ref, gs_ref, s0_ref, gn_ref, o_ref, s_ref, st_sc, *, tb, nt, hp):
    j = pl.program_id(2)

    @pl.when(j == 0)
    def _():
        for hh in range(hp):
            st_sc[hh] = s0_ref[hh].T

    c = HG_C
    r_io = lax.broadcasted_iota(I32, (c, c), 0)
    c_io = lax.broadcasted_iota(I32, (c, c), 1)
    causal = r_io >= c_io
    tri = jnp.where(causal, 1.0, 0.0)
    row = lax.broadcasted_iota(I32, (c, 1), 0)
    nch = max(1, tb // c)
    for ci, hh in [(ci, hh) for ci in range(nch) for hh in range(hp)]:
        cols = slice(hh * LANES, (hh + 1) * LANES)
        if tb >= c:
            rows = slice(ci * c, (ci + 1) * c)
            load = lambda ref: ref[rows, cols]
        else:
            load = lambda ref: _pad_rows(ref[:, cols], c)
        q, k, lf, v = load(q_ref), load(k_ref), load(lf_ref), load(v_ref)
        b = jnp.dot(tri, lf, preferred_element_type=F32, precision=lax.Precision.HIGHEST)
        st = st_sc[hh]
        o = _dot_nt((q * jnp.exp(b)).astype(BF16), st.astype(BF16))
        slabs = []
        for blk in range(c // HG_SUB):
            lo, hi = blk * HG_SUB, (blk + 1) * HG_SUB
            anchor = b[lo - 1:lo, :] if blk > 0 else jnp.zeros((1, LANES), F32)
            qb = q[lo:hi, :] * jnp.exp(b[lo:hi, :] - anchor)
            kb = k * jnp.exp(jnp.where(row < hi, anchor - b, -jnp.inf))
            slabs.append(_dot_nt(qb.astype(BF16), kb.astype(BF16)))
        a = jnp.where(causal, jnp.concatenate(slabs, axis=0), 0.0)
        o = o + _dot(a.astype(BF16), v.astype(BF16))
        b_last = b[c - 1:c, :]
        kd = k * jnp.exp(b_last - b)
        st_sc[hh] = st * jnp.exp(b_last) + _dot(v.T.astype(BF16), kd.astype(BF16))
        og = _rms(o, gn_ref[...])
        if tb >= c:
            o_ref[rows, cols] = (og * gs_ref[rows, cols]).astype(BF16)
        else:
            o_ref[:, cols] = (og[:tb, :] * gs_ref[:, cols]).astype(BF16)

    @pl.when(j == nt - 1)
    def _():
        for hh in range(hp):
            s_ref[hh] = st_sc[hh].T


def _hg_rec(q, k, lf, v, gs, s0, gn, bsz, t):
    r, d = q.shape
    heads = d // LANES
    hp = math.gcd(heads, HG_HP)
    tb = min(t, HG_TB)
    nt = t // tb
    blk = lambda: pl.BlockSpec((tb, hp * LANES), lambda b, h, j: (b * nt + j, h))
    state = lambda: pl.BlockSpec((None, hp, LANES, LANES), lambda b, h, j: (b, h, 0, 0))
    return pl.pallas_call(
        functools.partial(_hg_rec_kernel, tb=tb, nt=nt, hp=hp),
        grid=(bsz, heads // hp, nt),
        in_specs=[blk(), blk(), blk(), blk(), blk(), state(), _const_spec((1, LANES))],
        out_specs=[blk(), state()],
        out_shape=[jax.ShapeDtypeStruct((r, d), BF16),
                   jax.ShapeDtypeStruct((bsz, heads, LANES, LANES), F32)],
        scratch_shapes=[pltpu.VMEM((hp, LANES, LANES), F32)],
        compiler_params=_cparams("parallel", "parallel", "arbitrary"),
        name="hg_rec",
    )(q, k, lf, v, gs, s0, gn)


def kernel(x_prompt, x_sample, state_conv, cache_da_k, cache_da_v, cache_dsa_k, cache_dsa_v, cache_dsa_idx_k, state_hgrn, page_table, norm1_g, norm2_g, final_g, mlp_w1, mlp_w2, cv_w1, cv_b1, cv_dw, cv_dwb, cv_ln_g, cv_ln_b, cv_w2, cv_b2, da_w_in, da_lq1, da_lk1, da_lq2, da_lk2, da_subln_g, da_wo, dsa_w_in, dsa_wo, hg_w_in, hg_lb, hg_norm_g, hg_wo):
    bp, t, d = x_prompt.shape
    bd, tn, _ = x_sample.shape
    n_pages = page_table.shape[1]
    past_len = n_pages * PAGE
    depth = norm1_g.shape[0]
    assert d % LANES == 0 and t % ROW_TILE == 0 and tn & (tn - 1) == 0 and tn <= SUBLANES

    xp = x_prompt.reshape(bp * t, d)
    xs = x_sample.reshape(bd * tn, d)
    cos_p, sin_p = _rope_tables(jnp.arange(t))
    cos_s, sin_s = _rope_tables(past_len + jnp.arange(tn))
    cos_s, sin_s = jnp.tile(cos_s, (bd, 1)), jnp.tile(sin_s, (bd, 1))
    ksel_p = min(DSA_TOPK, t // 4)
    ksel_s = min(DSA_TOPK, (past_len + tn) // 4)
    zero_bias = jnp.zeros((1, d), F32)
    w1_all, w2_all = mlp_w1.astype(BF16), mlp_w2.astype(BF16)
    row1 = lambda a: a.reshape(1, -1)
    new = {n: [] for n in ('conv_p', 'conv_s', 'dak_p', 'dav_p', 'dak_s', 'dav_s', 'dsak_p', 'dsav_p', 'dsai_p',
                           'dsak_s', 'dsav_s', 'dsai_s', 'hg_p', 'hg_s')}

    for layer in range(depth):
        kind, j = layer % 4, layer // 4
        g1 = row1(norm1_g[layer])
        bo = zero_bias
        if kind == 0:
            pad = lambda a, n: jnp.concatenate([jnp.zeros(a.shape[:1] + (n,) + a.shape[2:], a.dtype), a], axis=1)
            dw = jnp.concatenate([cv_dw[j], jnp.zeros((CONV_HALO - CONV_W, d), F32)], axis=0)[:, None, :]
            cw = (g1, cv_w1[j].astype(BF16), row1(cv_b1[j]), dw, row1(cv_dwb[j]), row1(cv_ln_g[j]),
                  row1(cv_ln_b[j]))
            halo_pad = CONV_HALO - (CONV_W - 1)
            op, tail_p = _conv_mixer(xp.reshape(bp, t, d), jnp.zeros((bp, CONV_HALO, d), F32), *cw,
                                     nb=1, tt=CONV_TT)
            os_, tail_s = _conv_mixer(xs.reshape(bd, tn, d), pad(state_conv[j], halo_pad), *cw, nb=bd, tt=tn)
            op, os_ = op.reshape(bp * t, d), os_.reshape(bd * tn, d)
            new['conv_p'].append(tail_p[:, halo_pad:])
            new['conv_s'].append(tail_s[:, halo_pad:])
            wo, bo = cv_w2[j].astype(BF16), row1(cv_b2[j])
        elif kind == 1:
            lam_init = 0.8 - 0.6 * math.exp(-0.3 * layer)
            lams = (row1(da_lq1[j]), row1(da_lk1[j]), row1(da_lq2[j]), row1(da_lk2[j]))
            subg = row1(da_subln_g[j])
            w_in = da_w_in[j].astype(BF16)
            qp, kp, vp, kbp, vbp = _da_in(xp, g1, w_in, cos_p, sin_p)
            qs, ks, vs, _, _ = _da_in(xs, g1, w_in, cos_s, sin_s)
            op = _da_prompt(qp, kbp, vbp, lams, subg, bp, t, lam_init)
            os_ = _da_sample(page_table, qs.reshape(bd, tn, d), ks.reshape(bd, tn, d), vs.reshape(bd, tn, d),
                             cache_da_k, cache_da_v, lams, subg, j, lam_init).reshape(bd * tn, d)
            hd = (d // LANES, LANES)
            new['dak_p'].append(kp.reshape((bp, t) + hd))
            new['dav_p'].append(vp.reshape((bp, t) + hd))
            new['dak_s'].append(ks.reshape((bd, tn) + hd))
            new['dav_s'].append(vs.reshape((bd, tn) + hd))
            wo = da_wo[j].astype(BF16)
        elif kind == 2:
            w_in = dsa_w_in[j]
            w_in = jnp.concatenate([w_in, jnp.zeros((d, -w_in.shape[1] % LANES), F32)], axis=1).astype(BF16)
            qp, kp, vp, kbp, vbp, qip, mp_ = _dsa_in(xp, g1, w_in, cos_p, sin_p, seq_len=t)
            qs, ks, vs, _, _, qis, ms_ = _dsa_in(xs, g1, w_in, cos_s, sin_s)
            op = _dsa_prompt(qip, mp_, qp, kbp, vbp, bp, t, ksel_p)
            scores, scores_new = _dsa_sample_scores(
                page_table, qis.reshape(bd, tn, -1), ms_.reshape(bd, tn, LANES),
                jnp.transpose(cache_dsa_idx_k, (0, 1, 3, 2)), j)
            bias = _dsa_sample_select(scores, scores_new, ksel_s)
            os_ = _dsa_sample(page_table, bias, qs.reshape(bd, tn, d), ks.reshape(bd, tn, d),
                              vs.reshape(bd, tn, d), jnp.transpose(cache_dsa_k, (0, 1, 3, 4, 2)),
                              jnp.transpose(cache_dsa_v, (0, 1, 3, 4, 2)), j).reshape(bd * tn, d)
            hd = (d // HEAD64, HEAD64)
            new['dsak_p'].append(jnp.transpose(kp.reshape((bp,) + hd + (t,)), (0, 3, 1, 2)))
            new['dsav_p'].append(jnp.transpose(vp.reshape((bp,) + hd + (t,)), (0, 3, 1, 2)))
            new['dsai_p'].append(mp_[:, :HEAD64].reshape(bp, t, HEAD64))
            new['dsak_s'].append(ks.reshape((bd, tn) + hd))
            new['dsav_s'].append(vs.reshape((bd, tn) + hd))
            new['dsai_s'].append(ms_[:, :HEAD64].reshape(bd, tn, HEAD64))
            wo = dsa_wo[j].astype(BF16)
        else:
            w_in = hg_w_in[j].astype(BF16)
            gn = row1(hg_norm_g[j])
            heads = d // LANES
            hp = _hg_in(xp, g1, w_in, hg_lb, layer)
            hs = _hg_in(xs, g1, w_in, hg_lb, layer)
            op, sp = _hg_rec(*hp, jnp.zeros((bp, heads, LANES, LANES), F32), gn, bp, t)
            os_, ss = _hg_rec(*hs, state_hgrn[j], gn, bd, tn)
            new['hg_p'].append(sp.astype(state_hgrn.dtype))
            new['hg_s'].append(ss.astype(state_hgrn.dtype))
            wo = hg_wo[j].astype(BF16)

        final = layer == depth - 1
        post = (wo, bo, row1(norm2_g[layer]), w1_all, w2_all, row1(final_g))
        xp = _post(xp, op, *post, layer=layer, final=final)
        xs = _post(xs, os_, *post, layer=layer, final=final)

    return (xp.reshape(bp, t, d), xs.reshape(bd, tn, d),
            jnp.stack(new['conv_p']), jnp.stack(new['conv_s']),
            jnp.stack(new['dak_p']), jnp.stack(new['dav_p']), jnp.stack(new['dak_s']), jnp.stack(new['dav_s']),
            jnp.stack(new['dsak_p']), jnp.stack(new['dsav_p']), jnp.stack(new['dsai_p']),
            jnp.stack(new['dsak_s']), jnp.stack(new['dsav_s']), jnp.stack(new['dsai_s']),
            jnp.stack(new['hg_p']), jnp.stack(new['hg_s']))
```

```python
import functools
import math

import jax
import jax.numpy as jnp
from jax import lax
from jax.experimental import pallas as pl
from jax.experimental.pallas import tpu as pltpu

F32 = jnp.float32
BF16 = jnp.bfloat16
I32 = jnp.int32

NORM_EPS = 1e-6
LN_EPS = 1e-5
ROPE_THETA = 10000.0
ROPE_DIM = 64
CONV_W = 31
PAGE = 128
HEAD64 = 64
IDX_HEADS = 8
DSA_TOPK = 256

LANES = 128
SUBLANES = 8
CONV_HALO = 32
CONV_ROWS = 32
ROW_TILE = 512
CONV_TT = 256
ATT_TQ = 256
DA_HP = 4
DSA_TQ = 128
DSA_KV_FINE = 8
DSA_KV_COARSE = 8
SAMPLE_PP = 8
DSA_SAMPLE_PP = 16
SCORE_PP = 16
HG_C = 128
HG_SUB = 16
HG_TB = 512
HG_HP = 8
SELECT_NB = 8
MLP_FCHUNK = 1024
VMEM_LIMIT = 56 * 1024 * 1024
NEG_BIG = -1e30
INT_MIN = -2 ** 31


def _cparams(*sem):
    return pltpu.CompilerParams(dimension_semantics=sem, vmem_limit_bytes=VMEM_LIMIT)


def _const_spec(shape):
    nd = len(shape)
    return pl.BlockSpec(shape, lambda *_: (0,) * nd, pipeline_mode=pl.Buffered(1))


def _dot(a, b):
    return jnp.dot(a, b, preferred_element_type=F32)


def _dot_nt(a, b):
    return lax.dot_general(a, b, (((1,), (1,)), ((), ())), preferred_element_type=F32)


def _rms(x, g):
    return x * lax.rsqrt(jnp.mean(x * x, axis=-1, keepdims=True) + NORM_EPS) * g


def _sigmoid(x):
    return 1.0 / (1.0 + jnp.exp(-x))


def _lane_iota(shape=(1, LANES)):
    return lax.broadcasted_iota(I32, shape, len(shape) - 1)


def _rope(xc, cos, sin_signed):
    first_half = (_lane_iota() & 32) == 0
    partner = jnp.where(first_half, pltpu.roll(xc, 96, 1), pltpu.roll(xc, 32, 1))
    return xc * cos + partner * sin_signed


def _rope_tables(pos):
    inv = ROPE_THETA ** (-jnp.arange(0, ROPE_DIM, 2, dtype=F32) / ROPE_DIM)
    ang = pos.astype(F32)[:, None] * inv[None, :]
    cos, sin = jnp.cos(ang), jnp.sin(ang)
    return jnp.tile(cos, (1, 4)), jnp.tile(jnp.concatenate([-sin, sin], axis=1), (1, 2))


def _pad_rows(x, rows):
    return jnp.concatenate([x, jnp.zeros((rows - x.shape[0], x.shape[1]), x.dtype)], axis=0)


def _post_kernel(x_ref, o_ref, wo_ref, bo_ref, g_ref, w1_ref, w2_ref, fg_ref, y_ref, h_ref, *, final):
    x1 = x_ref[...] + _dot(o_ref[...], wo_ref[...]) + bo_ref[...]
    h = _rms(x1, g_ref[...]).astype(BF16)
    f = w1_ref.shape[1]
    fc = min(f, MLP_FCHUNK)
    for c in range(f // fc):
        a = jnp.maximum(_dot(h, w1_ref[:, c * fc:(c + 1) * fc]), 0.0)
        h_ref[:, c * fc:(c + 1) * fc] = (a * a).astype(BF16)
    y = x1 + _dot(h_ref[...], w2_ref[...])
    if final:
        y = _rms(y, fg_ref[...])
    y_ref[...] = y


def _post(x, o, wo, bo, g2, w1_all, w2_all, fg, layer, final):
    r, d = x.shape
    f = w1_all.shape[2]
    tm = min(r, ROW_TILE)
    row = lambda w: pl.BlockSpec((tm, w), lambda i: (i, 0))
    of_layer = lambda a, b: pl.BlockSpec((None, a, b), lambda i: (layer, 0, 0), pipeline_mode=pl.Buffered(1))
    return pl.pallas_call(
        functools.partial(_post_kernel, final=final),
        grid=(r // tm,),
        in_specs=[row(d), row(d), _const_spec((d, d)), _const_spec((1, d)), _const_spec((1, d)),
                  of_layer(d, f), of_layer(f, d), _const_spec((1, d))],
        out_specs=row(d),
        out_shape=jax.ShapeDtypeStruct((r, d), F32),
        scratch_shapes=[pltpu.VMEM((tm, f), BF16)],
        compiler_params=_cparams("parallel"),
        name="post_mlp",
    )(x, o, wo, bo, g2, w1_all, w2_all, fg)


def _conv_kernel(x_ref, buf_ref, g_ref, w1_ref, b1_ref, dw_ref, dwb_ref, lng_ref, lnb_ref,
                 o_ref, tail_ref, ext_ref, y_ref, sh_ref, *, nb, tt, d, carry, preshift):
    @pl.when(pl.program_id(1) == 0)
    def _():
        ext_ref[:, 0:CONV_HALO, :] = buf_ref[...]

    x = x_ref[...].reshape(nb * tt, d)
    h = _rms(x, g_ref[...]).astype(BF16)
    ag = _dot(h, w1_ref[...]) + b1_ref[...]
    u = ag[:, :d] * _sigmoid(ag[:, d:])
    ext_ref[:, CONV_HALO:CONV_HALO + tt, :] = u.reshape(nb, tt, d)

    if preshift:
        for a in range(1, SUBLANES):
            sh_ref[a - 1] = ext_ref[0, a:a + sh_ref.shape[1], :]

    rs = min(tt, CONV_ROWS)
    for r in range(tt // rs):
        acc = jnp.zeros((nb, rs, d), F32)
        for k in range(CONV_W):
            off = k + CONV_HALO - (CONV_W - 1)
            a = off % SUBLANES
            if preshift and a:
                start = r * rs + off - a
                window = sh_ref[a - 1, start:start + rs, :][None]
            else:
                window = ext_ref[:, r * rs + off:r * rs + off + rs, :]
            acc = acc + dw_ref[k] * window
        y_ref[:, r * rs:(r + 1) * rs, :] = acc + dwb_ref[...]

    y = y_ref[...]
    mu = jnp.mean(y, axis=-1, keepdims=True)
    yc = y - mu
    var = jnp.mean(yc * yc, axis=-1, keepdims=True)
    z = yc * lax.rsqrt(var + LN_EPS) * lng_ref[...] + lnb_ref[...]
    o_ref[...] = (z * _sigmoid(z)).astype(BF16)
    tail = ext_ref[:, tt:tt + CONV_HALO, :]
    tail_ref[...] = tail
    if carry:
        ext_ref[:, 0:CONV_HALO, :] = tail


def _conv_mixer(x3, buf, g1, w1, b1, dw, dwb, lng, lnb, nb, tt):
    b, t, d = x3.shape
    nt = t // tt
    preshift = nb == 1 and tt > CONV_ROWS
    sh_shape = (SUBLANES - 1, tt + CONV_HALO - SUBLANES, d) if preshift else (1, SUBLANES, LANES)
    kern = functools.partial(_conv_kernel, nb=nb, tt=tt, d=d, carry=nt > 1, preshift=preshift)
    return pl.pallas_call(
        kern,
        grid=(b // nb, nt),
        in_specs=[pl.BlockSpec((nb, tt, d), lambda i, j: (i, j, 0)),
                  pl.BlockSpec((nb, CONV_HALO, d), lambda i, j: (i, 0, 0)),
                  _const_spec((1, d)), _const_spec((d, 2 * d)), _const_spec((1, 2 * d)),
                  _const_spec((CONV_HALO, 1, d)), _const_spec((1, d)), _const_spec((1, d)),
                  _const_spec((1, d))],
        out_specs=[pl.BlockSpec((nb, tt, d), lambda i, j: (i, j, 0)),
                   pl.BlockSpec((nb, CONV_HALO, d), lambda i, j: (i, 0, 0))],
        out_shape=[jax.ShapeDtypeStruct((b, t, d), BF16),
                   jax.ShapeDtypeStruct((b, CONV_HALO, d), F32)],
        scratch_shapes=[pltpu.VMEM((nb, CONV_HALO + tt, d), F32), pltpu.VMEM((nb, tt, d), F32),
                        pltpu.VMEM(sh_shape, F32)],
        compiler_params=_cparams("parallel", "arbitrary"),
        name="conv_mixer",
    )(x3, buf, g1, w1, b1, dw, dwb, lng, lnb)


def _da_in_kernel(x_ref, g_ref, w_ref, cos_ref, sin_ref, q_ref, k_ref, v_ref, kb_ref, vb_ref, *, d):
    h = _rms(x_ref[...], g_ref[...]).astype(BF16)
    y = _dot(h, w_ref[...])
    cos, sin = cos_ref[...], sin_ref[...]
    for c in range(d // LANES):
        sl = slice(c * LANES, (c + 1) * LANES)
        q = _rope(y[:, c * LANES:(c + 1) * LANES], cos, sin)
        q_ref[:, sl] = (q * HEAD64 ** -0.5).astype(BF16)
        k = _rope(y[:, d + c * LANES:d + (c + 1) * LANES], cos, sin)
        k_ref[:, sl] = k
        kb_ref[:, sl] = k.astype(BF16)
        v = y[:, 2 * d + c * LANES:2 * d + (c + 1) * LANES]
        v_ref[:, sl] = v
        vb_ref[:, sl] = v.astype(BF16)


def _da_in(x, g1, w, cos, sin):
    r, d = x.shape
    tm = min(r, ROW_TILE)
    ntab = cos.shape[0] // tm
    row = lambda: pl.BlockSpec((tm, d), lambda i: (i, 0))
    tab = lambda: pl.BlockSpec((tm, LANES), lambda i: (i % ntab, 0))
    return pl.pallas_call(
        functools.partial(_da_in_kernel, d=d),
        grid=(r // tm,),
        in_specs=[row(), _const_spec((1, d)), _const_spec((d, 3 * d)), tab(), tab()],
        out_specs=[row(), row(), row(), row(), row()],
        out_shape=[jax.ShapeDtypeStruct((r, d), BF16), jax.ShapeDtypeStruct((r, d), F32),
                   jax.ShapeDtypeStruct((r, d), F32), jax.ShapeDtypeStruct((r, d), BF16),
                   jax.ShapeDtypeStruct((r, d), BF16)],
        compiler_params=_cparams("parallel"),
        name="da_in",
    )(x, g1, w, cos, sin)


def _da_lambda(lq1, lk1, lq2, lk2, lam_init):
    return (jnp.exp(jnp.sum(lq1[...] * lk1[...], axis=-1, keepdims=True))
            - jnp.exp(jnp.sum(lq2[...] * lk2[...], axis=-1, keepdims=True)) + lam_init)


def _da_prompt_kernel(lq1, lk1, lq2, lk2, subg_ref, q_ref, k_ref, v_ref, o_ref, *, tq, nq, hp, lam_init):
    i = pl.program_id(2)
    lam = _da_lambda(lq1, lk1, lq2, lk2, lam_init)
    lane = _lane_iota()
    keep = (lax.broadcasted_iota(I32, (tq, tq), 0) >= lax.broadcasted_iota(I32, (tq, tq), 1))

    def body(c, hh):
        off = c * tq
        cols = slice(hh * LANES, (hh + 1) * LANES)
        q = q_ref[:, cols]
        zero = jnp.zeros_like(q)
        q1 = jnp.where(lane < HEAD64, q, zero)
        q2 = jnp.where(lane >= HEAD64, q, zero)

        def softmax_parts(qm):
            sd = jnp.where(keep, _dot_nt(qm, k_ref[off:off + tq, cols]), NEG_BIG)
            m = jnp.max(sd, axis=-1, keepdims=True)
            so = None
            if c > 0:
                so = _dot_nt(qm, k_ref[0:off, cols])
                m = jnp.maximum(m, jnp.max(so, axis=-1, keepdims=True))
            pd = jnp.exp(sd - m)
            l = jnp.sum(pd, axis=-1, keepdims=True)
            po = None
            if c > 0:
                po = jnp.exp(so - m)
                l = l + jnp.sum(po, axis=-1, keepdims=True)
            return pd, po, l

        pd1, po1, l1 = softmax_parts(q1)
        pd2, po2, l2 = softmax_parts(q2)
        w1 = 1.0 / l1
        w2 = lam / l2
        o = _dot((pd1 * w1 - pd2 * w2).astype(BF16), v_ref[off:off + tq, cols])
        if c > 0:
            o = o + _dot((po1 * w1 - po2 * w2).astype(BF16), v_ref[0:off, cols])
        o_ref[:, cols] = (_rms(o, subg_ref[...]) * (1.0 - lam_init)).astype(BF16)

    def block(c):
        for hh in range(hp):
            body(c, hh)

    for c in range(nq):
        pl.when(i == c)(functools.partial(block, c))


def _da_prompt(q, kb, vb, lams, subg, bsz, t, lam_init):
    r, d = q.shape
    tq = min(t, ATT_TQ)
    nq = t // tq
    hp = math.gcd(d // LANES, DA_HP)
    lam_spec = _const_spec((1, HEAD64))
    return pl.pallas_call(
        functools.partial(_da_prompt_kernel, tq=tq, nq=nq, hp=hp, lam_init=lam_init),
        grid=(bsz, d // (hp * LANES), nq),
        in_specs=[lam_spec, lam_spec, lam_spec, lam_spec, _const_spec((1, LANES)),
                  pl.BlockSpec((tq, hp * LANES), lambda b, h, i: (b * nq + i, h)),
                  pl.BlockSpec((t, hp * LANES), lambda b, h, i: (b, h)),
                  pl.BlockSpec((t, hp * LANES), lambda b, h, i: (b, h))],
        out_specs=pl.BlockSpec((tq, hp * LANES), lambda b, h, i: (b * nq + i, h)),
        out_shape=jax.ShapeDtypeStruct((r, d), BF16),
        compiler_params=_cparams("parallel", "parallel", "arbitrary"),
        name="da_prompt",
    )(*lams, subg, q, kb, vb)


def _online_update(s, valid, pv_fn, m_sc, l_sc, acc_sc):
    m_old = m_sc[...]
    m_new = jnp.maximum(m_old, jnp.max(s, axis=-1, keepdims=True))
    alpha = jnp.exp(m_old - m_new)
    p = jnp.exp(s - m_new)
    if valid is not None:
        p = jnp.where(valid, p, 0.0)
    l_sc[...] = alpha * l_sc[...] + jnp.sum(p, axis=-1, keepdims=True)
    acc_sc[...] = alpha * acc_sc[...] + pv_fn(p.astype(BF16))
    m_sc[...] = m_new


def _da_sample_kernel(pt_ref, lq1, lk1, lq2, lk2, subg_ref, q_ref, kn_ref, vn_ref, *rest,
                      tn, d, lam_init, nsteps, pp):
    del pt_ref
    kp_refs, vp_refs = rest[:pp], rest[pp:2 * pp]
    o_ref, qh_sc, own_sc, m_sc, l_sc, acc_sc = rest[2 * pp:]
    step = pl.program_id(1)
    heads = d // LANES
    gr = 2 * tn
    rows = heads * gr
    ncol = pp * PAGE * heads

    @pl.when(step == 0)
    def _():
        q = q_ref[0].astype(F32)
        first_map = lax.broadcasted_iota(I32, (gr, LANES), 0) < tn
        lo_half = _lane_iota((gr, LANES)) < HEAD64
        for h in range(heads):
            qc = q[:, h * LANES:(h + 1) * LANES]
            qh_sc[h * gr:(h + 1) * gr, :] = jnp.where(
                first_map == lo_half, jnp.concatenate([qc, qc], axis=0), 0.0).astype(BF16)
        row_head = lax.broadcasted_iota(I32, (rows, ncol), 0) >> int(math.log2(gr))
        col_head = lax.broadcasted_iota(I32, (rows, ncol), 1) & (heads - 1)
        own_sc[...] = jnp.where(row_head == col_head, 0.0, NEG_BIG)
        m_sc[...] = jnp.full((rows, 1), NEG_BIG, F32)
        l_sc[...] = jnp.zeros((rows, 1), F32)
        acc_sc[...] = jnp.zeros((rows, LANES), F32)

    flat = lambda refs: jnp.concatenate([r[...] for r in refs], axis=0).astype(BF16)
    s = _dot_nt(qh_sc[...], flat(kp_refs)) + own_sc[...]
    _online_update(s, None, lambda pb: _dot(pb, flat(vp_refs)), m_sc, l_sc, acc_sc)

    @pl.when(step == nsteps - 1)
    def _():
        fresh = lambda ref, h: _pad_rows(ref[0][:, h * LANES:(h + 1) * LANES], PAGE).astype(BF16)
        qpos = lax.broadcasted_iota(I32, (rows, PAGE), 0) & (tn - 1)
        valid = lax.broadcasted_iota(I32, (rows, PAGE), 1) <= qpos
        s_new = jnp.concatenate(
            [_dot_nt(qh_sc[h * gr:(h + 1) * gr, :], fresh(kn_ref, h)) for h in range(heads)], axis=0)
        pv = lambda pb: jnp.concatenate(
            [_dot(pb[h * gr:(h + 1) * gr, :], fresh(vn_ref, h)) for h in range(heads)], axis=0)
        _online_update(jnp.where(valid, s_new, NEG_BIG), valid, pv, m_sc, l_sc, acc_sc)
        lam = _da_lambda(lq1, lk1, lq2, lk2, lam_init)
        for h in range(heads):
            r1 = slice(h * gr, h * gr + tn)
            r2 = slice(h * gr + tn, (h + 1) * gr)
            o = acc_sc[r1, :] / l_sc[r1, :] - lam * (acc_sc[r2, :] / l_sc[r2, :])
            o_ref[0, :, h * LANES:(h + 1) * LANES] = (_rms(o, subg_ref[...]) * (1.0 - lam_init)).astype(BF16)


def _da_sample(page_table, q3, kn3, vn3, k_pool, v_pool, lams, subg, layer, lam_init):
    bd, tn, d = q3.shape
    n_pages = page_table.shape[1]
    heads = d // LANES
    pp = math.gcd(n_pages, SAMPLE_PP)
    nsteps = n_pages // pp
    rows = heads * 2 * tn
    assert heads & (heads - 1) == 0
    flat_shape = k_pool.shape[:2] + (PAGE * heads, LANES)
    k_pool, v_pool = k_pool.reshape(flat_shape), v_pool.reshape(flat_shape)
    seq = lambda: pl.BlockSpec((1, tn, d), lambda b, s, pt: (b, 0, 0))
    pool = lambda i: pl.BlockSpec((None, None, PAGE * heads, LANES),
                                  lambda b, s, pt: (layer, pt[b, s * pp + i], 0, 0))
    lam_spec = _const_spec((1, HEAD64))
    kern = functools.partial(_da_sample_kernel, tn=tn, d=d, lam_init=lam_init, nsteps=nsteps, pp=pp)
    return pl.pallas_call(
        kern,
        grid_spec=pltpu.PrefetchScalarGridSpec(
            num_scalar_prefetch=1,
            grid=(bd, nsteps),
            in_specs=[lam_spec, lam_spec, lam_spec, lam_spec, _const_spec((1, LANES)), seq(), seq(), seq()]
            + [pool(i) for i in range(pp)] + [pool(i) for i in range(pp)],
            out_specs=seq(),
            scratch_shapes=[pltpu.VMEM((rows, LANES), BF16), pltpu.VMEM((rows, pp * PAGE * heads), F32),
                            pltpu.VMEM((rows, 1), F32), pltpu.VMEM((rows, 1), F32),
                            pltpu.VMEM((rows, LANES), F32)]),
        out_shape=jax.ShapeDtypeStruct((bd, tn, d), BF16),
        compiler_params=_cparams("parallel", "arbitrary"),
        name="da_sample",
    )(page_table, *lams, subg, q3, kn3, vn3, *([k_pool] * pp), *([v_pool] * pp))


def _dsa_in_kernel(x_ref, g_ref, w_ref, cos_ref, sin_ref, q_ref, k_ref, v_ref, kb_ref, vb_ref,
                   qi_ref, misc_ref, *, d, token_minor):
    h = _rms(x_ref[...], g_ref[...]).astype(BF16)
    y = _dot(h, w_ref[...])
    cos, sin = cos_ref[...], sin_ref[...]
    for c in range(d // LANES):
        sl = slice(c * LANES, (c + 1) * LANES)
        q = _rope(y[:, c * LANES:(c + 1) * LANES], cos, sin)
        q_ref[:, sl] = (q * HEAD64 ** -0.5).astype(BF16)
        k = _rope(y[:, d + c * LANES:d + (c + 1) * LANES], cos, sin)
        kb_ref[:, sl] = k.astype(BF16)
        v = y[:, 2 * d + c * LANES:2 * d + (c + 1) * LANES]
        vb_ref[:, sl] = v.astype(BF16)
        if token_minor:
            k_ref[sl, :] = k.T
            v_ref[sl, :] = v.T
        else:
            k_ref[:, sl] = k
            v_ref[:, sl] = v
    for c in range(IDX_HEADS * HEAD64 // LANES):
        qi = _rope(y[:, 3 * d + c * LANES:3 * d + (c + 1) * LANES], cos, sin)
        qi_ref[:, c * LANES:(c + 1) * LANES] = (qi * HEAD64 ** -0.5).astype(BF16)
    is_key = _lane_iota() < HEAD64
    base = 3 * d + IDX_HEADS * HEAD64
    misc_ref[...] = _rope(y[:, base:base + LANES], jnp.where(is_key, cos, 1.0), jnp.where(is_key, sin, 0.0))


def _dsa_in(x, g1, w, cos, sin, seq_len=None):
    r, d = x.shape
    tm = min(r, ROW_TILE)
    ntab = cos.shape[0] // tm
    nqi = IDX_HEADS * HEAD64
    row = lambda w_: pl.BlockSpec((tm, w_), lambda i: (i, 0))
    tab = lambda: pl.BlockSpec((tm, LANES), lambda i: (i % ntab, 0))
    if seq_len is None:
        kv_spec, kv_shape = row(d), jax.ShapeDtypeStruct((r, d), F32)
    else:
        nt = seq_len // tm
        kv_spec = pl.BlockSpec((None, d, tm), lambda i: (i // nt, 0, i % nt))
        kv_shape = jax.ShapeDtypeStruct((r // seq_len, d, seq_len), F32)
    return pl.pallas_call(
        functools.partial(_dsa_in_kernel, d=d, token_minor=seq_len is not None),
        grid=(r // tm,),
        in_specs=[row(d), _const_spec((1, d)), _const_spec(w.shape), tab(), tab()],
        out_specs=[row(d), kv_spec, kv_spec, row(d), row(d), row(nqi), row(LANES)],
        out_shape=[jax.ShapeDtypeStruct((r, d), BF16), kv_shape, kv_shape,
                   jax.ShapeDtypeStruct((r, d), BF16),
                   jax.ShapeDtypeStruct((r, d), BF16), jax.ShapeDtypeStruct((r, nqi), BF16),
                   jax.ShapeDtypeStruct((r, LANES), F32)],
        compiler_params=_cparams("parallel"),
        name="dsa_in",
    )(x, g1, w, cos, sin)


def _sortable_key(score):
    bits = lax.bitcast_convert_type(score + 0.0, I32)
    return bits ^ ((bits >> 31) & 0x7FFFFFFF)


def _select_topk(key, pos, krow, count, pos_bits):
    def try_bit(ans, bit):
        cand = ans | lax.shift_left(jnp.int32(1), bit)
        return jnp.where(count(key >= cand) >= krow, cand, ans)

    def two_bit_step(it, ans):
        lo = lax.shift_left(jnp.int32(1), 28 - 2 * it)
        c1, c2, c3 = ans | lo, ans | (lo + lo), ans | (lo + lo) | lo
        n1, n2, n3 = count(key >= c1), count(key >= c2), count(key >= c3)
        return jnp.where(n3 >= krow, c3, jnp.where(n2 >= krow, c2, jnp.where(n1 >= krow, c1, ans)))

    ans = jnp.where(count(key >= 0) >= krow, jnp.int32(0), jnp.int32(INT_MIN))
    ans = try_bit(ans, 30)
    ans = lax.fori_loop(0, 15, two_bit_step, ans)
    gt = key > ans
    eq = key == ans
    need = krow - count(gt)

    def pos_step(it, cut):
        cand = cut | lax.shift_left(jnp.int32(1), pos_bits - 1 - it)
        return jnp.where(count(eq & (pos < cand)) < need, cand, cut)

    surplus = jnp.max(count(eq) - need) > 0.0
    cut = lax.cond(surplus,
                   lambda: lax.fori_loop(0, pos_bits, pos_step, jnp.zeros(krow.shape, I32)),
                   lambda: jnp.full(krow.shape, 2 ** pos_bits, I32))
    return gt | (eq & (pos <= cut))


def _dsa_prompt_kernel(qi_ref, wq_ref, kim_ref, q_ref, k_ref, v_ref, o_ref, kid_sc, *, tq, nq, d, ksel):
    i = pl.program_id(1)
    lane = _lane_iota()
    lo_half = lane < HEAD64

    @pl.when(i == 0)
    def _():
        lo = jnp.where(lo_half, kim_ref[...], 0.0)
        kid_sc[...] = (lo + pltpu.roll(lo, HEAD64, 1)).astype(BF16)

    def body(kv):
        wt = wq_ref[...] * IDX_HEADS ** -0.5
        kid = kid_sc[0:kv, :]
        score = jnp.zeros((tq, kv), F32)
        for c in range(IDX_HEADS // 2):
            qc = qi_ref[:, c * LANES:(c + 1) * LANES]
            zero = jnp.zeros_like(qc)
            dots = _dot_nt(jnp.concatenate([jnp.where(lo_half, qc, zero), jnp.where(lo_half, zero, qc)], axis=0),
                           kid)
            for half in range(2):
                h = 2 * c + half
                score = score + wt[:, HEAD64 + h:HEAD64 + h + 1] * jnp.maximum(dots[half * tq:(half + 1) * tq, :], 0.0)

        qpos = i * tq + lax.broadcasted_iota(I32, (tq, 1), 0)
        kpos = lax.broadcasted_iota(I32, (tq, kv), 1)
        key = jnp.where(kpos <= qpos, _sortable_key(score), INT_MIN)
        krow = jnp.minimum(qpos + 1, ksel).astype(F32)
        count = lambda mask: jnp.sum(jnp.where(mask, 1.0, 0.0), axis=-1, keepdims=True)
        sel = _select_topk(key, kpos, krow, count, max(1, (kv - 1).bit_length()))
        bias = jnp.where(sel, 0.0, NEG_BIG)
        bias2 = jnp.concatenate([bias, bias], axis=0)
        ones_blk = jnp.ones((kv, LANES), BF16)

        for c in range(d // LANES):
            cols = slice(c * LANES, (c + 1) * LANES)
            qc = q_ref[:, cols]
            kc = k_ref[0:kv, cols]
            vc = v_ref[0:kv, cols]
            zero = jnp.zeros_like(qc)
            qq = jnp.concatenate([jnp.where(lo_half, qc, zero), jnp.where(lo_half, zero, qc)], axis=0)
            s = _dot_nt(qq, kc) + bias2
            p = jnp.exp(s - jnp.max(s, axis=-1, keepdims=True))
            ov = _dot(p.astype(BF16), jnp.concatenate([vc, ones_blk], axis=1))
            o2 = ov[:, :LANES] / ov[:, LANES:LANES + 1]
            o_ref[:, cols] = jnp.where(lo_half, o2[:tq, :], o2[tq:, :]).astype(BF16)

    fine, coarse = max(1, nq // DSA_KV_FINE), max(1, nq // DSA_KV_COARSE)
    bounds = list(range(fine, nq // 2 + 1, fine)) + list(range(nq // 2 + coarse, nq + 1, coarse))
    for lo, hi in zip([0] + bounds[:-1], bounds):
        pl.when((i >= lo) & (i < hi))(functools.partial(body, hi * tq))


def _dsa_prompt(qi, misc, q, kb, vb, bsz, t, ksel):
    r, d = q.shape
    tq = min(t, DSA_TQ)
    nq = t // tq
    nqi = qi.shape[1]
    qrow = lambda w: pl.BlockSpec((tq, w), lambda b, i: (b * nq + i, 0))
    seq = lambda w: pl.BlockSpec((t, w), lambda b, i: (b, 0))
    return pl.pallas_call(
        functools.partial(_dsa_prompt_kernel, tq=tq, nq=nq, d=d, ksel=ksel),
        grid=(bsz, nq),
        in_specs=[qrow(nqi), qrow(LANES), seq(LANES), qrow(d), seq(d), seq(d)],
        out_specs=qrow(d),
        out_shape=jax.ShapeDtypeStruct((r, d), BF16),
        scratch_shapes=[pltpu.VMEM((t, LANES), BF16)],
        compiler_params=_cparams("parallel", "arbitrary"),
        name="dsa_prompt",
    )(qi, misc, misc, q, kb, vb)


def _dsa_sample_scores_kernel(pt_ref, qi_ref, misc_ref, *rest, tn, nsteps, pp):
    del pt_ref
    pool_refs = rest[:pp]
    sc_ref, scn_ref, qst_sc, w_sc = rest[pp:]
    step = pl.program_id(1)

    @pl.when(step == 0)
    def _():
        qi = qi_ref[0].astype(F32)
        misc = misc_ref[0]
        qst_sc[...] = jnp.concatenate(
            [qi[:, h * HEAD64:(h + 1) * HEAD64] for h in range(IDX_HEADS)], axis=0).astype(BF16)
        w_sc[...] = jnp.concatenate(
            [misc[:, HEAD64 + h:HEAD64 + h + 1] for h in range(IDX_HEADS)], axis=0) * IDX_HEADS ** -0.5

    def head_sum(dots):
        sc = w_sc[...] * jnp.maximum(dots, 0.0)
        out = sc[0:tn, :]
        for h in range(1, IDX_HEADS):
            out = out + sc[h * tn:(h + 1) * tn, :]
        return out

    for i in range(pp):
        sc_ref[0, i] = head_sum(_dot(qst_sc[...], pool_refs[i][...].astype(BF16)))

    @pl.when(step == nsteps - 1)
    def _():
        fresh = _pad_rows(misc_ref[0][:, :HEAD64], PAGE).astype(BF16)
        scn_ref[0] = head_sum(_dot_nt(qst_sc[...], fresh))


def _dsa_sample_scores(page_table, qi3, misc3, idx_pool_t, layer):
    bd, tn, nqi = qi3.shape
    n_pages = page_table.shape[1]
    pp = math.gcd(n_pages, SCORE_PP)
    nsteps = n_pages // pp
    pool = lambda i: pl.BlockSpec((None, None, HEAD64, PAGE),
                                  lambda b, s, pt: (layer, pt[b, s * pp + i], 0, 0))
    kern = functools.partial(_dsa_sample_scores_kernel, tn=tn, nsteps=nsteps, pp=pp)
    return pl.pallas_call(
        kern,
        grid_spec=pltpu.PrefetchScalarGridSpec(
            num_scalar_prefetch=1,
            grid=(bd, nsteps),
            in_specs=[pl.BlockSpec((1, tn, nqi), lambda b, s, pt: (b, 0, 0)),
                      pl.BlockSpec((1, tn, LANES), lambda b, s, pt: (b, 0, 0))]
            + [pool(i) for i in range(pp)],
            out_specs=[pl.BlockSpec((1, pp, tn, PAGE), lambda b, s, pt: (b, s, 0, 0)),
                       pl.BlockSpec((1, tn, PAGE), lambda b, s, pt: (b, 0, 0))],
            scratch_shapes=[pltpu.VMEM((IDX_HEADS * tn, HEAD64), BF16),
                            pltpu.VMEM((IDX_HEADS * tn, 1), F32)]),
        out_shape=[jax.ShapeDtypeStruct((bd, n_pages, tn, PAGE), F32),
                   jax.ShapeDtypeStruct((bd, tn, PAGE), F32)],
        compiler_params=_cparams("parallel", "arbitrary"),
        name="dsa_sample_scores",
    )(page_table, qi3, misc3, *([idx_pool_t] * pp))


def _block_diag_queries(q, groups, tn):
    d = q.shape[1]
    qt = jnp.concatenate([q] * groups, axis=0)
    shift = int(math.log2(tn))
    rg = lax.broadcasted_iota(I32, (groups * tn, d), 0) >> shift
    cg = lax.broadcasted_iota(I32, (groups * tn, d), 1) >> 6
    return jnp.where(rg == cg, qt, 0.0).astype(BF16)


def _dsa_sample_select_kernel(sc_ref, scn_ref, bias_ref, *, nb, tn, n_pages, ksel):
    shape = (nb, n_pages + 1, tn, PAGE)
    page = lax.broadcasted_iota(I32, shape, 1)
    qidx = lax.broadcasted_iota(I32, shape, 2)
    lane = lax.broadcasted_iota(I32, shape, 3)
    score = jnp.concatenate([sc_ref[...], scn_ref[...].reshape(nb, 1, tn, PAGE)], axis=1)
    key = jnp.where((page < n_pages) | (lane <= qidx), _sortable_key(score), INT_MIN)
    krow = jnp.full((nb, 1, tn, 1), float(ksel), F32)
    count = lambda mask: jnp.sum(jnp.sum(jnp.where(mask, 1.0, 0.0), axis=1, keepdims=True),
                                 axis=-1, keepdims=True)
    sel = _select_topk(key, page * PAGE + lane, krow, count, ((n_pages + 1) * PAGE - 1).bit_length())
    bias_ref[...] = jnp.where(sel, 0.0, NEG_BIG)


def _dsa_sample_select(scores, scores_new, ksel):
    bd, n_pages, tn, _ = scores.shape
    nb = math.gcd(bd, SELECT_NB)
    return pl.pallas_call(
        functools.partial(_dsa_sample_select_kernel, nb=nb, tn=tn, n_pages=n_pages, ksel=ksel),
        grid=(bd // nb,),
        in_specs=[pl.BlockSpec((nb, n_pages, tn, PAGE), lambda i: (i, 0, 0, 0)),
                  pl.BlockSpec((nb, tn, PAGE), lambda i: (i, 0, 0))],
        out_specs=pl.BlockSpec((nb, n_pages + 1, tn, PAGE), lambda i: (i, 0, 0, 0)),
        out_shape=jax.ShapeDtypeStruct((bd, n_pages + 1, tn, PAGE), F32),
        compiler_params=_cparams("parallel"),
        name="dsa_sample_select",
    )(scores, scores_new)


def _dsa_sample_kernel(pt_ref, bias_ref, q_ref, kn_ref, vn_ref, *rest, tn, d, n_pages, nsteps, pp):
    del pt_ref
    kp_refs, vp_refs = rest[:pp], rest[pp:2 * pp]
    o_ref, qbd_sc, m_sc, l_sc, acc_sc = rest[2 * pp:]
    step = pl.program_id(1)
    groups = d // HEAD64
    rows = groups * tn

    @pl.when(step == 0)
    def _():
        qbd_sc[...] = _block_diag_queries(q_ref[0].astype(F32), groups, tn)
        m_sc[...] = jnp.full((rows, 1), NEG_BIG, F32)
        l_sc[...] = jnp.zeros((rows, 1), F32)
        acc_sc[...] = jnp.zeros((rows, d), F32)

    def tiled_bias(page):
        return jnp.concatenate([bias_ref[0, page]] * groups, axis=0)

    @pl.when(step < nsteps)
    def _():
        bias = jnp.concatenate([tiled_bias(step * pp + i) for i in range(pp)], axis=1)
        s = jnp.concatenate([_dot(qbd_sc[...], r[...].reshape(d, PAGE).astype(BF16)) for r in kp_refs],
                            axis=1) + bias
        pv = lambda pb: sum(_dot_nt(pb[:, i * PAGE:(i + 1) * PAGE], vp_refs[i][...].reshape(d, PAGE).astype(BF16))
                            for i in range(pp))
        _online_update(s, bias == 0.0, pv, m_sc, l_sc, acc_sc)

    @pl.when(step == nsteps)
    def _():
        bias = tiled_bias(n_pages)
        s = _dot_nt(qbd_sc[...], _pad_rows(kn_ref[0], PAGE).astype(BF16)) + bias
        pv = lambda pb: _dot(pb, _pad_rows(vn_ref[0], PAGE).astype(BF16))
        _online_update(s, bias == 0.0, pv, m_sc, l_sc, acc_sc)
        lo_half = _lane_iota() < HEAD64
        for c in range(d // LANES):
            cols = slice(c * LANES, (c + 1) * LANES)
            r0 = slice(2 * c * tn, (2 * c + 1) * tn)
            r1 = slice((2 * c + 1) * tn, (2 * c + 2) * tn)
            o = jnp.where(lo_half, acc_sc[r0, cols] / l_sc[r0, :], acc_sc[r1, cols] / l_sc[r1, :])
            o_ref[0, :, cols] = o.astype(BF16)


def _dsa_sample(page_table, bias, q3, kn3, vn3, k_pool_t, v_pool_t, layer):
    bd, tn, d = q3.shape
    n_pages = page_table.shape[1]
    groups = d // HEAD64
    rows = groups * tn
    pp = math.gcd(n_pages, DSA_SAMPLE_PP)
    nsteps = n_pages // pp
    seq = lambda: pl.BlockSpec((1, tn, d), lambda b, s, pt: (b, 0, 0))
    pool = lambda i: pl.BlockSpec(
        (None, None, groups, HEAD64, PAGE),
        lambda b, s, pt: (layer, pt[b, jnp.minimum(s, nsteps - 1) * pp + i], 0, 0, 0))
    kern = functools.partial(_dsa_sample_kernel, tn=tn, d=d, n_pages=n_pages, nsteps=nsteps, pp=pp)
    return pl.pallas_call(
        kern,
        grid_spec=pltpu.PrefetchScalarGridSpec(
            num_scalar_prefetch=1,
            grid=(bd, nsteps + 1),
            in_specs=[pl.BlockSpec((1, n_pages + 1, tn, PAGE), lambda b, s, pt: (b, 0, 0, 0)),
                      seq(), seq(), seq()] + [pool(i) for i in range(pp)] + [pool(i) for i in range(pp)],
            out_specs=seq(),
            scratch_shapes=[pltpu.VMEM((rows, d), BF16), pltpu.VMEM((rows, 1), F32),
                            pltpu.VMEM((rows, 1), F32), pltpu.VMEM((rows, d), F32)]),
        out_shape=jax.ShapeDtypeStruct((bd, tn, d), BF16),
        compiler_params=_cparams("parallel", "arbitrary"),
        name="dsa_sample",
    )(page_table, bias, q3, kn3, vn3, *([k_pool_t] * pp), *([v_pool_t] * pp))


def _hg_in_kernel(x_ref, g_ref, w_ref, lbw_ref, q_ref, k_ref, lf_ref, v_ref, gs_ref, *, d, layer):
    h = _rms(x_ref[...], g_ref[...]).astype(BF16)
    y = _dot(h, w_ref[...])
    lbw = lbw_ref[...]
    e = jnp.exp(lbw - jnp.max(lbw, axis=0, keepdims=True))
    sm = e / jnp.sum(e, axis=0, keepdims=True)
    lb = jnp.sum(sm[1:layer + 1, :], axis=0, keepdims=True)
    q, fz, v, g = y[:, :d], y[:, d:2 * d], y[:, 2 * d:3 * d], y[:, 3 * d:]
    f = lb + (1.0 - lb) * _sigmoid(fz)
    q_ref[...] = q * _sigmoid(q)
    k_ref[...] = 1.0 - f
    lf_ref[...] = jnp.log(f)
    v_ref[...] = v
    gs_ref[...] = g * _sigmoid(g)


def _hg_in(x, g1, w, lbw, layer):
    r, d = x.shape
    tm = min(r, ROW_TILE)
    row = lambda: pl.BlockSpec((tm, d), lambda i: (i, 0))
    return pl.pallas_call(
        functools.partial(_hg_in_kernel, d=d, layer=layer),
        grid=(r // tm,),
        in_specs=[row(), _const_spec((1, d)), _const_spec((d, 4 * d)), _const_spec(lbw.shape)],
        out_specs=[row()] * 5,
        out_shape=[jax.ShapeDtypeStruct((r, d), F32)] * 5,
        compiler_params=_cparams("parallel"),
        name="hg_in",
    )(x, g1, w, lbw)


def _hg_rec_kernel(q_ref, k_ref, lf_ref, v_ref, gs_ref, s0_ref, gn_ref, o_ref, s_ref, st_sc, *, tb, nt, hp):
    j = pl.program_id(2)

    @pl.when(j == 0)
    def _():
        for hh in range(hp):
            st_sc[hh] = s0_ref[hh].T

    c = HG_C
    r_io = lax.broadcasted_iota(I32, (c, c), 0)
    c_io = lax.broadcasted_iota(I32, (c, c), 1)
    causal = r_io >= c_io
    tri = jnp.where(causal, 1.0, 0.0)
    row = lax.broadcasted_iota(I32, (c, 1), 0)
    nch = max(1, tb // c)
    for ci, hh in [(ci, hh) for ci in range(nch) for hh in range(hp)]:
        cols = slice(hh * LANES, (hh + 1) * LANES)
        if tb >= c:
            rows = slice(ci * c, (ci + 1) * c)
            load = lambda ref: ref[rows, cols]
        else:
            load = lambda ref: _pad_rows(ref[:, cols], c)
        q, k, lf, v = load(q_ref), load(k_ref), load(lf_ref), load(v_ref)
        b = jnp.dot(tri, lf, preferred_element_type=F32, precision=lax.Precision.HIGHEST)
        st = st_sc[hh]
        o = _dot_nt((q * jnp.exp(b)).astype(BF16), st.astype(BF16))
        slabs = []
        for blk in range(c // HG_SUB):
            lo, hi = blk * HG_SUB, (blk + 1) * HG_SUB
            anchor = b[lo - 1:lo, :] if blk > 0 else jnp.zeros((1, LANES), F32)
            qb = q[lo:hi, :] * jnp.exp(b[lo:hi, :] - anchor)
            kb = k * jnp.exp(jnp.where(row < hi, anchor - b, -jnp.inf))
            slabs.append(_dot_nt(qb.astype(BF16), kb.astype(BF16)))
        a = jnp.where(causal, jnp.concatenate(slabs, axis=0), 0.0)
        o = o + _dot(a.astype(BF16), v.astype(BF16))
        b_last = b[c - 1:c, :]
        kd = k * jnp.exp(b_last - b)
        st_sc[hh] = st * jnp.exp(b_last) + _dot(v.T.astype(BF16), kd.astype(BF16))
        og = _rms(o, gn_ref[...])
        if tb >= c:
            o_ref[rows, cols] = (og * gs_ref[rows, cols]).astype(BF16)
        else:
            o_ref[:, cols] = (og[:tb, :] * gs_ref[:, cols]).astype(BF16)

    @pl.when(j == nt - 1)
    def _():
        for hh in range(hp):
            s_ref[hh] = st_sc[hh].T


def _hg_rec(q, k, lf, v, gs, s0, gn, bsz, t):
    r, d = q.shape
    heads = d // LANES
    hp = math.gcd(heads, HG_HP)
    tb = min(t, HG_TB)
    nt = t // tb
    blk = lambda: pl.BlockSpec((tb, hp * LANES), lambda b, h, j: (b * nt + j, h))
    state = lambda: pl.BlockSpec((None, hp, LANES, LANES), lambda b, h, j: (b, h, 0, 0))
    return pl.pallas_call(
        functools.partial(_hg_rec_kernel, tb=tb, nt=nt, hp=hp),
        grid=(bsz, heads // hp, nt),
        in_specs=[blk(), blk(), blk(), blk(), blk(), state(), _const_spec((1, LANES))],
        out_specs=[blk(), state()],
        out_shape=[jax.ShapeDtypeStruct((r, d), BF16),
                   jax.ShapeDtypeStruct((bsz, heads, LANES, LANES), F32)],
        scratch_shapes=[pltpu.VMEM((hp, LANES, LANES), F32)],
        compiler_params=_cparams("parallel", "parallel", "arbitrary"),
        name="hg_rec",
    )(q, k, lf, v, gs, s0, gn)


def kernel(x_prompt, x_sample, state_conv, cache_da_k, cache_da_v, cache_dsa_k, cache_dsa_v, cache_dsa_idx_k, state_hgrn, page_table, norm1_g, norm2_g, final_g, mlp_w1, mlp_w2, cv_w1, cv_b1, cv_dw, cv_dwb, cv_ln_g, cv_ln_b, cv_w2, cv_b2, da_w_in, da_lq1, da_lk1, da_lq2, da_lk2, da_subln_g, da_wo, dsa_w_in, dsa_wo, hg_w_in, hg_lb, hg_norm_g, hg_wo):
    bp, t, d = x_prompt.shape
    bd, tn, _ = x_sample.shape
    n_pages = page_table.shape[1]
    past_len = n_pages * PAGE
    depth = norm1_g.shape[0]
    assert d % LANES == 0 and t % ROW_TILE == 0 and tn & (tn - 1) == 0 and tn <= SUBLANES

    xp = x_prompt.reshape(bp * t, d)
    xs = x_sample.reshape(bd * tn, d)
    cos_p, sin_p = _rope_tables(jnp.arange(t))
    cos_s, sin_s = _rope_tables(past_len + jnp.arange(tn))
    cos_s, sin_s = jnp.tile(cos_s, (bd, 1)), jnp.tile(sin_s, (bd, 1))
    ksel_p = min(DSA_TOPK, t // 4)
    ksel_s = min(DSA_TOPK, (past_len + tn) // 4)
    zero_bias = jnp.zeros((1, d), F32)
    w1_all, w2_all = mlp_w1.astype(BF16), mlp_w2.astype(BF16)
    row1 = lambda a: a.reshape(1, -1)
    new = {n: [] for n in ('conv_p', 'conv_s', 'dak_p', 'dav_p', 'dak_s', 'dav_s', 'dsak_p', 'dsav_p', 'dsai_p',
                           'dsak_s', 'dsav_s', 'dsai_s', 'hg_p', 'hg_s')}

    for layer in range(depth):
        kind, j = layer % 4, layer // 4
        g1 = row1(norm1_g[layer])
        bo = zero_bias
        if kind == 0:
            pad = lambda a, n: jnp.concatenate([jnp.zeros(a.shape[:1] + (n,) + a.shape[2:], a.dtype), a], axis=1)
            dw = jnp.concatenate([cv_dw[j], jnp.zeros((CONV_HALO - CONV_W, d), F32)], axis=0)[:, None, :]
            cw = (g1, cv_w1[j].astype(BF16), row1(cv_b1[j]), dw, row1(cv_dwb[j]), row1(cv_ln_g[j]),
                  row1(cv_ln_b[j]))
            halo_pad = CONV_HALO - (CONV_W - 1)
            op, tail_p = _conv_mixer(xp.reshape(bp, t, d), jnp.zeros((bp, CONV_HALO, d), F32), *cw,
                                     nb=1, tt=CONV_TT)
            os_, tail_s = _conv_mixer(xs.reshape(bd, tn, d), pad(state_conv[j], halo_pad), *cw, nb=bd, tt=tn)
            op, os_ = op.reshape(bp * t, d), os_.reshape(bd * tn, d)
            new['conv_p'].append(tail_p[:, halo_pad:])
            new['conv_s'].append(tail_s[:, halo_pad:])
            wo, bo = cv_w2[j].astype(BF16), row1(cv_b2[j])
        elif kind == 1:
            lam_init = 0.8 - 0.6 * math.exp(-0.3 * layer)
            lams = (row1(da_lq1[j]), row1(da_lk1[j]), row1(da_lq2[j]), row1(da_lk2[j]))
            subg = row1(da_subln_g[j])
            w_in = da_w_in[j].astype(BF16)
            qp, kp, vp, kbp, vbp = _da_in(xp, g1, w_in, cos_p, sin_p)
            qs, ks, vs, _, _ = _da_in(xs, g1, w_in, cos_s, sin_s)
            op = _da_prompt(qp, kbp, vbp, lams, subg, bp, t, lam_init)
            os_ = _da_sample(page_table, qs.reshape(bd, tn, d), ks.reshape(bd, tn, d), vs.reshape(bd, tn, d),
                             cache_da_k, cache_da_v, lams, subg, j, lam_init).reshape(bd * tn, d)
            hd = (d // LANES, LANES)
            new['dak_p'].append(kp.reshape((bp, t) + hd))
            new['dav_p'].append(vp.reshape((bp, t) + hd))
            new['dak_s'].append(ks.reshape((bd, tn) + hd))
            new['dav_s'].append(vs.reshape((bd, tn) + hd))
            wo = da_wo[j].astype(BF16)
        elif kind == 2:
            w_in = dsa_w_in[j]
            w_in = jnp.concatenate([w_in, jnp.zeros((d, -w_in.shape[1] % LANES), F32)], axis=1).astype(BF16)
            qp, kp, vp, kbp, vbp, qip, mp_ = _dsa_in(xp, g1, w_in, cos_p, sin_p, seq_len=t)
            qs, ks, vs, _, _, qis, ms_ = _dsa_in(xs, g1, w_in, cos_s, sin_s)
            op = _dsa_prompt(qip, mp_, qp, kbp, vbp, bp, t, ksel_p)
            scores, scores_new = _dsa_sample_scores(
                page_table, qis.reshape(bd, tn, -1), ms_.reshape(bd, tn, LANES),
                jnp.transpose(cache_dsa_idx_k, (0, 1, 3, 2)), j)
            bias = _dsa_sample_select(scores, scores_new, ksel_s)
            os_ = _dsa_sample(page_table, bias, qs.reshape(bd, tn, d), ks.reshape(bd, tn, d),
                              vs.reshape(bd, tn, d), jnp.transpose(cache_dsa_k, (0, 1, 3, 4, 2)),
                              jnp.transpose(cache_dsa_v, (0, 1, 3, 4, 2)), j).reshape(bd * tn, d)
            hd = (d // HEAD64, HEAD64)
            new['dsak_p'].append(jnp.transpose(kp.reshape((bp,) + hd + (t,)), (0, 3, 1, 2)))
            new['dsav_p'].append(jnp.transpose(vp.reshape((bp,) + hd + (t,)), (0, 3, 1, 2)))
            new['dsai_p'].append(mp_[:, :HEAD64].reshape(bp, t, HEAD64))
            new['dsak_s'].append(ks.reshape((bd, tn) + hd))
            new['dsav_s'].append(vs.reshape((bd, tn) + hd))
            new['dsai_s'].append(ms_[:, :HEAD64].reshape(bd, tn, HEAD64))
            wo = dsa_wo[j].astype(BF16)
        else:
            w_in = hg_w_in[j].astype(BF16)
            gn = row1(hg_norm_g[j])
            heads = d // LANES
            hp = _hg_in(xp, g1, w_in, hg_lb, layer)
            hs = _hg_in(xs, g1, w_in, hg_lb, layer)
            op, sp = _hg_rec(*hp, jnp.zeros((bp, heads, LANES, LANES), F32), gn, bp, t)
            os_, ss = _hg_rec(*hs, state_hgrn[j], gn, bd, tn)
            new['hg_p'].append(sp.astype(state_hgrn.dtype))
            new['hg_s'].append(ss.astype(state_hgrn.dtype))
            wo = hg_wo[j].astype(BF16)

        final = layer == depth - 1
        post = (wo, bo, row1(norm2_g[layer]), w1_all, w2_all, row1(final_g))
        xp = _post(xp, op, *post, layer=layer, final=final)
        xs = _post(xs, os_, *post, layer=layer, final=final)

    return (xp.reshape(bp, t, d), xs.reshape(bd, tn, d),
            jnp.stack(new['conv_p']), jnp.stack(new['conv_s']),
            jnp.stack(new['dak_p']), jnp.stack(new['dav_p']), jnp.stack(new['dak_s']), jnp.stack(new['dav_s']),
            jnp.stack(new['dsak_p']), jnp.stack(new['dsav_p']), jnp.stack(new['dsai_p']),
            jnp.stack(new['dsak_s']), jnp.stack(new['dsav_s']), jnp.stack(new['dsai_s']),
            jnp.stack(new['hg_p']), jnp.stack(new['hg_s']))
```

```python
import functools
import math

import jax
import jax.numpy as jnp
from jax import lax
from jax.experimental import pallas as pl
from jax.experimental.pallas import tpu as pltpu

F32 = jnp.float32
BF16 = jnp.bfloat16
I32 = jnp.int32

NORM_EPS = 1e-6
LN_EPS = 1e-5
ROPE_THETA = 10000.0
ROPE_DIM = 64
CONV_W = 31
PAGE = 128
HEAD64 = 64
IDX_HEADS = 8
DSA_TOPK = 256

LANES = 128
SUBLANES = 8
CONV_HALO = 32
CONV_ROWS = 32
ROW_TILE = 512
CONV_TT = 256
ATT_TQ = 256
DA_HP = 4
DSA_TQ = 128
DSA_KV_FINE = 8
DSA_KV_COARSE = 4
SAMPLE_PP = 8
DSA_SAMPLE_PP = 16
SCORE_PP = 16
HG_C = 128
HG_SUB = 16
HG_SHORT_C = 32
HG_TB = 512
HG_HP = 8
SELECT_NB = 8
MLP_FCHUNK = 1024
VMEM_LIMIT = 56 * 1024 * 1024
NEG_BIG = -1e30
INT_MIN = -2 ** 31


def _cparams(*sem):
    return pltpu.CompilerParams(dimension_semantics=sem, vmem_limit_bytes=VMEM_LIMIT)


def _const_spec(shape):
    nd = len(shape)
    return pl.BlockSpec(shape, lambda *_: (0,) * nd, pipeline_mode=pl.Buffered(1))


def _dot(a, b):
    return jnp.dot(a, b, preferred_element_type=F32)


def _dot_nt(a, b):
    return lax.dot_general(a, b, (((1,), (1,)), ((), ())), preferred_element_type=F32)


def _rms(x, g):
    return x * lax.rsqrt(jnp.mean(x * x, axis=-1, keepdims=True) + NORM_EPS) * g


def _sigmoid(x):
    return 1.0 / (1.0 + jnp.exp(-x))


def _lane_iota(shape=(1, LANES)):
    return lax.broadcasted_iota(I32, shape, len(shape) - 1)


def _rope(xc, cos, sin_signed):
    first_half = (_lane_iota() & 32) == 0
    partner = jnp.where(first_half, pltpu.roll(xc, 96, 1), pltpu.roll(xc, 32, 1))
    return xc * cos + partner * sin_signed


def _rope_tables(pos):
    inv = ROPE_THETA ** (-jnp.arange(0, ROPE_DIM, 2, dtype=F32) / ROPE_DIM)
    ang = pos.astype(F32)[:, None] * inv[None, :]
    cos, sin = jnp.cos(ang), jnp.sin(ang)
    return jnp.tile(cos, (1, 4)), jnp.tile(jnp.concatenate([-sin, sin], axis=1), (1, 2))


def _pad_rows(x, rows):
    return jnp.concatenate([x, jnp.zeros((rows - x.shape[0], x.shape[1]), x.dtype)], axis=0)


def _post_kernel(x_ref, o_ref, wo_ref, bo_ref, g_ref, w1_ref, w2_ref, fg_ref, y_ref, h_ref, *, final):
    x1 = x_ref[...] + _dot(o_ref[...], wo_ref[...]) + bo_ref[...]
    h = _rms(x1, g_ref[...]).astype(BF16)
    f = w1_ref.shape[1]
    fc = min(f, MLP_FCHUNK)
    for c in range(f // fc):
        a = jnp.maximum(_dot(h, w1_ref[:, c * fc:(c + 1) * fc]), 0.0)
        h_ref[:, c * fc:(c + 1) * fc] = (a * a).astype(BF16)
    y = x1 + _dot(h_ref[...], w2_ref[...])
    if final:
        y = _rms(y, fg_ref[...])
    y_ref[...] = y


def _post(x, o, wo, bo, g2, w1_all, w2_all, fg, layer, final):
    r, d = x.shape
    f = w1_all.shape[2]
    tm = min(r, ROW_TILE)
    row = lambda w: pl.BlockSpec((tm, w), lambda i: (i, 0))
    of_layer = lambda a, b: pl.BlockSpec((None, a, b), lambda i: (layer, 0, 0), pipeline_mode=pl.Buffered(1))
    return pl.pallas_call(
        functools.partial(_post_kernel, final=final),
        grid=(r // tm,),
        in_specs=[row(d), row(d), _const_spec((d, d)), _const_spec((1, d)), _const_spec((1, d)),
                  of_layer(d, f), of_layer(f, d), _const_spec((1, d))],
        out_specs=row(d),
        out_shape=jax.ShapeDtypeStruct((r, d), F32),
        scratch_shapes=[pltpu.VMEM((tm, f), BF16)],
        compiler_params=_cparams("parallel"),
        name="post_mlp",
    )(x, o, wo, bo, g2, w1_all, w2_all, fg)


def _conv_kernel(x_ref, buf_ref, g_ref, w1_ref, b1_ref, dw_ref, dwb_ref, lng_ref, lnb_ref,
                 o_ref, tail_ref, ext_ref, y_ref, sh_ref, *, nb, tt, d, carry, preshift):
    @pl.when(pl.program_id(1) == 0)
    def _():
        ext_ref[:, 0:CONV_HALO, :] = buf_ref[...]

    x = x_ref[...].reshape(nb * tt, d)
    h = _rms(x, g_ref[...]).astype(BF16)
    ag = _dot(h, w1_ref[...]) + b1_ref[...]
    u = ag[:, :d] * _sigmoid(ag[:, d:])
    ext_ref[:, CONV_HALO:CONV_HALO + tt, :] = u.reshape(nb, tt, d)

    if preshift:
        for a in range(1, SUBLANES):
            sh_ref[a - 1] = ext_ref[0, a:a + sh_ref.shape[1], :]

    rs = min(tt, CONV_ROWS)
    for r in range(tt // rs):
        acc = jnp.zeros((nb, rs, d), F32)
        for k in range(CONV_W):
            off = k + CONV_HALO - (CONV_W - 1)
            a = off % SUBLANES
            if preshift and a:
                start = r * rs + off - a
                window = sh_ref[a - 1, start:start + rs, :][None]
            else:
                window = ext_ref[:, r * rs + off:r * rs + off + rs, :]
            acc = acc + dw_ref[k] * window
        y_ref[:, r * rs:(r + 1) * rs, :] = acc + dwb_ref[...]

    y = y_ref[...]
    mu = jnp.mean(y, axis=-1, keepdims=True)
    yc = y - mu
    var = jnp.mean(yc * yc, axis=-1, keepdims=True)
    z = yc * lax.rsqrt(var + LN_EPS) * lng_ref[...] + lnb_ref[...]
    o_ref[...] = (z * _sigmoid(z)).astype(BF16)
    tail = ext_ref[:, tt:tt + CONV_HALO, :]
    tail_ref[...] = tail
    if carry:
        ext_ref[:, 0:CONV_HALO, :] = tail


def _conv_mixer(x3, buf, g1, w1, b1, dw, dwb, lng, lnb, nb, tt):
    b, t, d = x3.shape
    nt = t // tt
    preshift = nb == 1 and tt > CONV_ROWS
    sh_shape = (SUBLANES - 1, tt + CONV_HALO - SUBLANES, d) if preshift else (1, SUBLANES, LANES)
    kern = functools.partial(_conv_kernel, nb=nb, tt=tt, d=d, carry=nt > 1, preshift=preshift)
    return pl.pallas_call(
        kern,
        grid=(b // nb, nt),
        in_specs=[pl.BlockSpec((nb, tt, d), lambda i, j: (i, j, 0)),
                  pl.BlockSpec((nb, CONV_HALO, d), lambda i, j: (i, 0, 0)),
                  _const_spec((1, d)), _const_spec((d, 2 * d)), _const_spec((1, 2 * d)),
                  _const_spec((CONV_HALO, 1, d)), _const_spec((1, d)), _const_spec((1, d)),
                  _const_spec((1, d))],
        out_specs=[pl.BlockSpec((nb, tt, d), lambda i, j: (i, j, 0)),
                   pl.BlockSpec((nb, CONV_HALO, d), lambda i, j: (i, 0, 0))],
        out_shape=[jax.ShapeDtypeStruct((b, t, d), BF16),
                   jax.ShapeDtypeStruct((b, CONV_HALO, d), F32)],
        scratch_shapes=[pltpu.VMEM((nb, CONV_HALO + tt, d), F32), pltpu.VMEM((nb, tt, d), F32),
                        pltpu.VMEM(sh_shape, F32)],
        compiler_params=_cparams("parallel", "arbitrary"),
        name="conv_mixer",
    )(x3, buf, g1, w1, b1, dw, dwb, lng, lnb)


def _da_in_kernel(x_ref, g_ref, w_ref, cos_ref, sin_ref, q_ref, k_ref, v_ref, kb_ref, vb_ref, *, d):
    h = _rms(x_ref[...], g_ref[...]).astype(BF16)
    y = _dot(h, w_ref[...])
    cos, sin = cos_ref[...], sin_ref[...]
    for c in range(d // LANES):
        sl = slice(c * LANES, (c + 1) * LANES)
        q = _rope(y[:, c * LANES:(c + 1) * LANES], cos, sin)
        q_ref[:, sl] = (q * HEAD64 ** -0.5).astype(BF16)
        k = _rope(y[:, d + c * LANES:d + (c + 1) * LANES], cos, sin)
        k_ref[:, sl] = k
        kb_ref[:, sl] = k.astype(BF16)
        v = y[:, 2 * d + c * LANES:2 * d + (c + 1) * LANES]
        v_ref[:, sl] = v
        vb_ref[:, sl] = v.astype(BF16)


def _da_in(x, g1, w, cos, sin):
    r, d = x.shape
    tm = min(r, ROW_TILE)
    ntab = cos.shape[0] // tm
    row = lambda: pl.BlockSpec((tm, d), lambda i: (i, 0))
    tab = lambda: pl.BlockSpec((tm, LANES), lambda i: (i % ntab, 0))
    return pl.pallas_call(
        functools.partial(_da_in_kernel, d=d),
        grid=(r // tm,),
        in_specs=[row(), _const_spec((1, d)), _const_spec((d, 3 * d)), tab(), tab()],
        out_specs=[row(), row(), row(), row(), row()],
        out_shape=[jax.ShapeDtypeStruct((r, d), BF16), jax.ShapeDtypeStruct((r, d), F32),
                   jax.ShapeDtypeStruct((r, d), F32), jax.ShapeDtypeStruct((r, d), BF16),
                   jax.ShapeDtypeStruct((r, d), BF16)],
        compiler_params=_cparams("parallel"),
        name="da_in",
    )(x, g1, w, cos, sin)


def _da_lambda(lq1, lk1, lq2, lk2, lam_init):
    return (jnp.exp(jnp.sum(lq1[...] * lk1[...], axis=-1, keepdims=True))
            - jnp.exp(jnp.sum(lq2[...] * lk2[...], axis=-1, keepdims=True)) + lam_init)


def _da_prompt_kernel(lq1, lk1, lq2, lk2, subg_ref, q_ref, k_ref, v_ref, o_ref, *, tq, nq, hp, lam_init):
    i = pl.program_id(2)
    lam = _da_lambda(lq1, lk1, lq2, lk2, lam_init)
    lane = _lane_iota()
    keep = (lax.broadcasted_iota(I32, (tq, tq), 0) >= lax.broadcasted_iota(I32, (tq, tq), 1))

    def body(c, hh):
        off = c * tq
        cols = slice(hh * LANES, (hh + 1) * LANES)
        q = q_ref[:, cols]
        zero = jnp.zeros_like(q)
        q1 = jnp.where(lane < HEAD64, q, zero)
        q2 = jnp.where(lane >= HEAD64, q, zero)

        def softmax_parts(qm):
            sd = jnp.where(keep, _dot_nt(qm, k_ref[off:off + tq, cols]), NEG_BIG)
            m = jnp.max(sd, axis=-1, keepdims=True)
            so = None
            if c > 0:
                so = _dot_nt(qm, k_ref[0:off, cols])
                m = jnp.maximum(m, jnp.max(so, axis=-1, keepdims=True))
            pd = jnp.exp(sd - m)
            l = jnp.sum(pd, axis=-1, keepdims=True)
            po = None
            if c > 0:
                po = jnp.exp(so - m)
                l = l + jnp.sum(po, axis=-1, keepdims=True)
            return pd, po, l

        pd1, po1, l1 = softmax_parts(q1)
        pd2, po2, l2 = softmax_parts(q2)
        w1 = 1.0 / l1
        w2 = lam / l2
        o = _dot((pd1 * w1 - pd2 * w2).astype(BF16), v_ref[off:off + tq, cols])
        if c > 0:
            o = o + _dot((po1 * w1 - po2 * w2).astype(BF16), v_ref[0:off, cols])
        o_ref[:, cols] = (_rms(o, subg_ref[...]) * (1.0 - lam_init)).astype(BF16)

    def block(c):
        for hh in range(hp):
            body(c, hh)

    for c in range(nq):
        pl.when(i == c)(functools.partial(block, c))


def _da_prompt(q, kb, vb, lams, subg, bsz, t, lam_init):
    r, d = q.shape
    tq = min(t, ATT_TQ)
    nq = t // tq
    hp = math.gcd(d // LANES, DA_HP)
    lam_spec = _const_spec((1, HEAD64))
    return pl.pallas_call(
        functools.partial(_da_prompt_kernel, tq=tq, nq=nq, hp=hp, lam_init=lam_init),
        grid=(bsz, d // (hp * LANES), nq),
        in_specs=[lam_spec, lam_spec, lam_spec, lam_spec, _const_spec((1, LANES)),
                  pl.BlockSpec((tq, hp * LANES), lambda b, h, i: (b * nq + i, h)),
                  pl.BlockSpec((t, hp * LANES), lambda b, h, i: (b, h)),
                  pl.BlockSpec((t, hp * LANES), lambda b, h, i: (b, h))],
        out_specs=pl.BlockSpec((tq, hp * LANES), lambda b, h, i: (b * nq + i, h)),
        out_shape=jax.ShapeDtypeStruct((r, d), BF16),
        compiler_params=_cparams("parallel", "parallel", "arbitrary"),
        name="da_prompt",
    )(*lams, subg, q, kb, vb)


def _online_update(s, valid, pv_fn, m_sc, l_sc, acc_sc):
    m_old = m_sc[...]
    m_new = jnp.maximum(m_old, jnp.max(s, axis=-1, keepdims=True))
    alpha = jnp.exp(m_old - m_new)
    p = jnp.exp(s - m_new)
    if valid is not None:
        p = jnp.where(valid, p, 0.0)
    l_sc[...] = alpha * l_sc[...] + jnp.sum(p, axis=-1, keepdims=True)
    acc_sc[...] = alpha * acc_sc[...] + pv_fn(p.astype(BF16))
    m_sc[...] = m_new


def _da_sample_kernel(pt_ref, lq1, lk1, lq2, lk2, subg_ref, q_ref, kn_ref, vn_ref, *rest,
                      tn, d, lam_init, nsteps, pp):
    del pt_ref
    kp_refs, vp_refs = rest[:pp], rest[pp:2 * pp]
    o_ref, qh_sc, own_sc, m_sc, l_sc, acc_sc = rest[2 * pp:]
    step = pl.program_id(1)
    heads = d // LANES
    gr = 2 * tn
    rows = heads * gr
    ncol = pp * PAGE * heads

    @pl.when(step == 0)
    def _():
        q = q_ref[0].astype(F32)
        first_map = lax.broadcasted_iota(I32, (gr, LANES), 0) < tn
        lo_half = _lane_iota((gr, LANES)) < HEAD64
        for h in range(heads):
            qc = q[:, h * LANES:(h + 1) * LANES]
            qh_sc[h * gr:(h + 1) * gr, :] = jnp.where(
                first_map == lo_half, jnp.concatenate([qc, qc], axis=0), 0.0).astype(BF16)
        row_head = lax.broadcasted_iota(I32, (rows, ncol), 0) >> int(math.log2(gr))
        col_head = lax.broadcasted_iota(I32, (rows, ncol), 1) & (heads - 1)
        own_sc[...] = jnp.where(row_head == col_head, 0.0, NEG_BIG)
        m_sc[...] = jnp.full((rows, 1), NEG_BIG, F32)
        l_sc[...] = jnp.zeros((rows, 1), F32)
        acc_sc[...] = jnp.zeros((rows, LANES), F32)

    flat = lambda refs: jnp.concatenate([r[...] for r in refs], axis=0).astype(BF16)
    s = _dot_nt(qh_sc[...], flat(kp_refs)) + own_sc[...]
    _online_update(s, None, lambda pb: _dot(pb, flat(vp_refs)), m_sc, l_sc, acc_sc)

    @pl.when(step == nsteps - 1)
    def _():
        fresh = lambda ref, h: _pad_rows(ref[0][:, h * LANES:(h + 1) * LANES], PAGE).astype(BF16)
        qpos = lax.broadcasted_iota(I32, (rows, PAGE), 0) & (tn - 1)
        valid = lax.broadcasted_iota(I32, (rows, PAGE), 1) <= qpos
        s_new = jnp.concatenate(
            [_dot_nt(qh_sc[h * gr:(h + 1) * gr, :], fresh(kn_ref, h)) for h in range(heads)], axis=0)
        pv = lambda pb: jnp.concatenate(
            [_dot(pb[h * gr:(h + 1) * gr, :], fresh(vn_ref, h)) for h in range(heads)], axis=0)
        _online_update(jnp.where(valid, s_new, NEG_BIG), valid, pv, m_sc, l_sc, acc_sc)
        lam = _da_lambda(lq1, lk1, lq2, lk2, lam_init)
        for h in range(heads):
            r1 = slice(h * gr, h * gr + tn)
            r2 = slice(h * gr + tn, (h + 1) * gr)
            o = acc_sc[r1, :] / l_sc[r1, :] - lam * (acc_sc[r2, :] / l_sc[r2, :])
            o_ref[0, :, h * LANES:(h + 1) * LANES] = (_rms(o, subg_ref[...]) * (1.0 - lam_init)).astype(BF16)


def _da_sample(page_table, q3, kn3, vn3, k_pool, v_pool, lams, subg, layer, lam_init):
    bd, tn, d = q3.shape
    n_pages = page_table.shape[1]
    heads = d // LANES
    pp = math.gcd(n_pages, SAMPLE_PP)
    nsteps = n_pages // pp
    rows = heads * 2 * tn
    assert heads & (heads - 1) == 0
    flat_shape = k_pool.shape[:2] + (PAGE * heads, LANES)
    k_pool, v_pool = k_pool.reshape(flat_shape), v_pool.reshape(flat_shape)
    seq = lambda: pl.BlockSpec((1, tn, d), lambda b, s, pt: (b, 0, 0))
    pool = lambda i: pl.BlockSpec((None, None, PAGE * heads, LANES),
                                  lambda b, s, pt: (layer, pt[b, s * pp + i], 0, 0))
    lam_spec = _const_spec((1, HEAD64))
    kern = functools.partial(_da_sample_kernel, tn=tn, d=d, lam_init=lam_init, nsteps=nsteps, pp=pp)
    return pl.pallas_call(
        kern,
        grid_spec=pltpu.PrefetchScalarGridSpec(
            num_scalar_prefetch=1,
            grid=(bd, nsteps),
            in_specs=[lam_spec, lam_spec, lam_spec, lam_spec, _const_spec((1, LANES)), seq(), seq(), seq()]
            + [pool(i) for i in range(pp)] + [pool(i) for i in range(pp)],
            out_specs=seq(),
            scratch_shapes=[pltpu.VMEM((rows, LANES), BF16), pltpu.VMEM((rows, pp * PAGE * heads), F32),
                            pltpu.VMEM((rows, 1), F32), pltpu.VMEM((rows, 1), F32),
                            pltpu.VMEM((rows, LANES), F32)]),
        out_shape=jax.ShapeDtypeStruct((bd, tn, d), BF16),
        compiler_params=_cparams("parallel", "arbitrary"),
        name="da_sample",
    )(page_table, *lams, subg, q3, kn3, vn3, *([k_pool] * pp), *([v_pool] * pp))


def _dsa_in_kernel(x_ref, g_ref, w_ref, cos_ref, sin_ref, q_ref, k_ref, v_ref, kb_ref, vb_ref,
                   qi_ref, misc_ref, *, d, token_minor):
    h = _rms(x_ref[...], g_ref[...]).astype(BF16)
    y = _dot(h, w_ref[...])
    cos, sin = cos_ref[...], sin_ref[...]
    for c in range(d // LANES):
        sl = slice(c * LANES, (c + 1) * LANES)
        q = _rope(y[:, c * LANES:(c + 1) * LANES], cos, sin)
        q_ref[:, sl] = (q * HEAD64 ** -0.5).astype(BF16)
        k = _rope(y[:, d + c * LANES:d + (c + 1) * LANES], cos, sin)
        kb_ref[:, sl] = k.astype(BF16)
        v = y[:, 2 * d + c * LANES:2 * d + (c + 1) * LANES]
        vb_ref[:, sl] = v.astype(BF16)
        if token_minor:
            k_ref[sl, :] = k.T
            v_ref[sl, :] = v.T
        else:
            k_ref[:, sl] = k
            v_ref[:, sl] = v
    for c in range(IDX_HEADS * HEAD64 // LANES):
        qi = _rope(y[:, 3 * d + c * LANES:3 * d + (c + 1) * LANES], cos, sin)
        qi_ref[:, c * LANES:(c + 1) * LANES] = (qi * HEAD64 ** -0.5).astype(BF16)
    is_key = _lane_iota() < HEAD64
    base = 3 * d + IDX_HEADS * HEAD64
    misc_ref[...] = _rope(y[:, base:base + LANES], jnp.where(is_key, cos, 1.0), jnp.where(is_key, sin, 0.0))


def _dsa_in(x, g1, w, cos, sin, seq_len=None):
    r, d = x.shape
    tm = min(r, ROW_TILE)
    ntab = cos.shape[0] // tm
    nqi = IDX_HEADS * HEAD64
    row = lambda w_: pl.BlockSpec((tm, w_), lambda i: (i, 0))
    tab = lambda: pl.BlockSpec((tm, LANES), lambda i: (i % ntab, 0))
    if seq_len is None:
        kv_spec, kv_shape = row(d), jax.ShapeDtypeStruct((r, d), F32)
    else:
        nt = seq_len // tm
        kv_spec = pl.BlockSpec((None, d, tm), lambda i: (i // nt, 0, i % nt))
        kv_shape = jax.ShapeDtypeStruct((r // seq_len, d, seq_len), F32)
    return pl.pallas_call(
        functools.partial(_dsa_in_kernel, d=d, token_minor=seq_len is not None),
        grid=(r // tm,),
        in_specs=[row(d), _const_spec((1, d)), _const_spec(w.shape), tab(), tab()],
        out_specs=[row(d), kv_spec, kv_spec, row(d), row(d), row(nqi), row(LANES)],
        out_shape=[jax.ShapeDtypeStruct((r, d), BF16), kv_shape, kv_shape,
                   jax.ShapeDtypeStruct((r, d), BF16),
                   jax.ShapeDtypeStruct((r, d), BF16), jax.ShapeDtypeStruct((r, nqi), BF16),
                   jax.ShapeDtypeStruct((r, LANES), F32)],
        compiler_params=_cparams("parallel"),
        name="dsa_in",
    )(x, g1, w, cos, sin)


def _sortable_key(score):
    bits = lax.bitcast_convert_type(score + 0.0, I32)
    return bits ^ ((bits >> 31) & 0x7FFFFFFF)


def _select_topk(key, pos, krow, count, pos_bits):
    def try_bit(ans, bit):
        cand = ans | lax.shift_left(jnp.int32(1), bit)
        return jnp.where(count(key >= cand) >= krow, cand, ans)

    def two_bit_step(it, ans):
        lo = lax.shift_left(jnp.int32(1), 28 - 2 * it)
        c1, c2, c3 = ans | lo, ans | (lo + lo), ans | (lo + lo) | lo
        n1, n2, n3 = count(key >= c1), count(key >= c2), count(key >= c3)
        return jnp.where(n3 >= krow, c3, jnp.where(n2 >= krow, c2, jnp.where(n1 >= krow, c1, ans)))

    ans = jnp.where(count(key >= 0) >= krow, jnp.int32(0), jnp.int32(INT_MIN))
    ans = try_bit(ans, 30)
    ans = lax.fori_loop(0, 15, two_bit_step, ans)
    gt = key > ans
    eq = key == ans
    need = krow - count(gt)

    def pos_step(it, cut):
        cand = cut | lax.shift_left(jnp.int32(1), pos_bits - 1 - it)
        return jnp.where(count(eq & (pos < cand)) < need, cand, cut)

    surplus = jnp.max(count(eq) - need) > 0.0
    cut = lax.cond(surplus,
                   lambda: lax.fori_loop(0, pos_bits, pos_step, jnp.zeros(krow.shape, I32)),
                   lambda: jnp.full(krow.shape, 2 ** pos_bits, I32))
    return gt | (eq & (pos <= cut))


def _dsa_prompt_kernel(qi_ref, wq_ref, kim_ref, q_ref, k_ref, v_ref, o_ref, kid_sc, *, tq, nq, d, ksel):
    i = pl.program_id(1)
    lane = _lane_iota()
    lo_half = lane < HEAD64

    @pl.when(i == 0)
    def _():
        lo = jnp.where(lo_half, kim_ref[...], 0.0)
        kid_sc[...] = (lo + pltpu.roll(lo, HEAD64, 1)).astype(BF16)

    def body(kv):
        wt = wq_ref[...] * IDX_HEADS ** -0.5
        kid = kid_sc[0:kv, :]
        score = jnp.zeros((tq, kv), F32)
        for c in range(IDX_HEADS // 2):
            qc = qi_ref[:, c * LANES:(c + 1) * LANES]
            zero = jnp.zeros_like(qc)
            dots = _dot_nt(jnp.concatenate([jnp.where(lo_half, qc, zero), jnp.where(lo_half, zero, qc)], axis=0),
                           kid)
            for half in range(2):
                h = 2 * c + half
                score = score + wt[:, HEAD64 + h:HEAD64 + h + 1] * jnp.maximum(dots[half * tq:(half + 1) * tq, :], 0.0)

        qpos = i * tq + lax.broadcasted_iota(I32, (tq, 1), 0)
        kpos = lax.broadcasted_iota(I32, (tq, kv), 1)
        key = jnp.where(kpos <= qpos, _sortable_key(score), INT_MIN)
        krow = jnp.minimum(qpos + 1, ksel).astype(F32)
        count = lambda mask: jnp.sum(jnp.where(mask, 1.0, 0.0), axis=-1, keepdims=True)
        sel = _select_topk(key, kpos, krow, count, max(1, (kv - 1).bit_length()))
        bias = jnp.where(sel, 0.0, NEG_BIG)
        bias2 = jnp.concatenate([bias, bias], axis=0)
        ones_blk = jnp.ones((kv, LANES), BF16)

        for c in range(d // LANES):
            cols = slice(c * LANES, (c + 1) * LANES)
            qc = q_ref[:, cols]
            kc = k_ref[0:kv, cols]
            vc = v_ref[0:kv, cols]
            zero = jnp.zeros_like(qc)
            qq = jnp.concatenate([jnp.where(lo_half, qc, zero), jnp.where(lo_half, zero, qc)], axis=0)
            s = _dot_nt(qq, kc) + bias2
            p = jnp.exp(s - jnp.max(s, axis=-1, keepdims=True))
            ov = _dot(p.astype(BF16), jnp.concatenate([vc, ones_blk], axis=1))
            o2 = ov[:, :LANES] / ov[:, LANES:LANES + 1]
            o_ref[:, cols] = jnp.where(lo_half, o2[:tq, :], o2[tq:, :]).astype(BF16)

    fine, coarse = max(1, nq // DSA_KV_FINE), max(1, nq // DSA_KV_COARSE)
    bounds = list(range(fine, nq // 2 + 1, fine)) + list(range(nq // 2 + coarse, nq + 1, coarse))
    for lo, hi in zip([0] + bounds[:-1], bounds):
        pl.when((i >= lo) & (i < hi))(functools.partial(body, hi * tq))


def _dsa_prompt(qi, misc, q, kb, vb, bsz, t, ksel):
    r, d = q.shape
    tq = min(t, DSA_TQ)
    nq = t // tq
    nqi = qi.shape[1]
    qrow = lambda w: pl.BlockSpec((tq, w), lambda b, i: (b * nq + i, 0))
    seq = lambda w: pl.BlockSpec((t, w), lambda b, i: (b, 0))
    return pl.pallas_call(
        functools.partial(_dsa_prompt_kernel, tq=tq, nq=nq, d=d, ksel=ksel),
        grid=(bsz, nq),
        in_specs=[qrow(nqi), qrow(LANES), seq(LANES), qrow(d), seq(d), seq(d)],
        out_specs=qrow(d),
        out_shape=jax.ShapeDtypeStruct((r, d), BF16),
        scratch_shapes=[pltpu.VMEM((t, LANES), BF16)],
        compiler_params=_cparams("parallel", "arbitrary"),
        name="dsa_prompt",
    )(qi, misc, misc, q, kb, vb)


def _dsa_sample_scores_kernel(pt_ref, qi_ref, misc_ref, *rest, tn, nsteps, pp):
    del pt_ref
    pool_refs = rest[:pp]
    sc_ref, scn_ref, qst_sc, w_sc = rest[pp:]
    step = pl.program_id(1)

    @pl.when(step == 0)
    def _():
        qi = qi_ref[0].astype(F32)
        misc = misc_ref[0]
        qst_sc[...] = jnp.concatenate(
            [qi[:, h * HEAD64:(h + 1) * HEAD64] for h in range(IDX_HEADS)], axis=0).astype(BF16)
        w_sc[...] = jnp.concatenate(
            [misc[:, HEAD64 + h:HEAD64 + h + 1] for h in range(IDX_HEADS)], axis=0) * IDX_HEADS ** -0.5

    def head_sum(dots):
        sc = w_sc[...] * jnp.maximum(dots, 0.0)
        out = sc[0:tn, :]
        for h in range(1, IDX_HEADS):
            out = out + sc[h * tn:(h + 1) * tn, :]
        return out

    for i in range(pp):
        sc_ref[0, i] = head_sum(_dot(qst_sc[...], pool_refs[i][...].astype(BF16)))

    @pl.when(step == nsteps - 1)
    def _():
        fresh = _pad_rows(misc_ref[0][:, :HEAD64], PAGE).astype(BF16)
        scn_ref[0] = head_sum(_dot_nt(qst_sc[...], fresh))


def _dsa_sample_scores(page_table, qi3, misc3, idx_pool_t, layer):
    bd, tn, nqi = qi3.shape
    n_pages = page_table.shape[1]
    pp = math.gcd(n_pages, SCORE_PP)
    nsteps = n_pages // pp
    pool = lambda i: pl.BlockSpec((None, None, HEAD64, PAGE),
                                  lambda b, s, pt: (layer, pt[b, s * pp + i], 0, 0))
    kern = functools.partial(_dsa_sample_scores_kernel, tn=tn, nsteps=nsteps, pp=pp)
    return pl.pallas_call(
        kern,
        grid_spec=pltpu.PrefetchScalarGridSpec(
            num_scalar_prefetch=1,
            grid=(bd, nsteps),
            in_specs=[pl.BlockSpec((1, tn, nqi), lambda b, s, pt: (b, 0, 0)),
                      pl.BlockSpec((1, tn, LANES), lambda b, s, pt: (b, 0, 0))]
            + [pool(i) for i in range(pp)],
            out_specs=[pl.BlockSpec((1, pp, tn, PAGE), lambda b, s, pt: (b, s, 0, 0)),
                       pl.BlockSpec((1, tn, PAGE), lambda b, s, pt: (b, 0, 0))],
            scratch_shapes=[pltpu.VMEM((IDX_HEADS * tn, HEAD64), BF16),
                            pltpu.VMEM((IDX_HEADS * tn, 1), F32)]),
        out_shape=[jax.ShapeDtypeStruct((bd, n_pages, tn, PAGE), F32),
                   jax.ShapeDtypeStruct((bd, tn, PAGE), F32)],
        compiler_params=_cparams("parallel", "arbitrary"),
        name="dsa_sample_scores",
    )(page_table, qi3, misc3, *([idx_pool_t] * pp))


def _block_diag_queries(q, groups, tn):
    d = q.shape[1]
    qt = jnp.concatenate([q] * groups, axis=0)
    shift = int(math.log2(tn))
    rg = lax.broadcasted_iota(I32, (groups * tn, d), 0) >> shift
    cg = lax.broadcasted_iota(I32, (groups * tn, d), 1) >> 6
    return jnp.where(rg == cg, qt, 0.0).astype(BF16)


def _dsa_sample_select_kernel(sc_ref, scn_ref, bias_ref, *, nb, tn, n_pages, ksel):
    shape = (nb, n_pages + 1, tn, PAGE)
    page = lax.broadcasted_iota(I32, shape, 1)
    qidx = lax.broadcasted_iota(I32, shape, 2)
    lane = lax.broadcasted_iota(I32, shape, 3)
    score = jnp.concatenate([sc_ref[...], scn_ref[...].reshape(nb, 1, tn, PAGE)], axis=1)
    key = jnp.where((page < n_pages) | (lane <= qidx), _sortable_key(score), INT_MIN)
    krow = jnp.full((nb, 1, tn, 1), float(ksel), F32)
    count = lambda mask: jnp.sum(jnp.sum(jnp.where(mask, 1.0, 0.0), axis=1, keepdims=True),
                                 axis=-1, keepdims=True)
    sel = _select_topk(key, page * PAGE + lane, krow, count, ((n_pages + 1) * PAGE - 1).bit_length())
    bias_ref[...] = jnp.where(sel, 0.0, NEG_BIG)


def _dsa_sample_select(scores, scores_new, ksel):
    bd, n_pages, tn, _ = scores.shape
    nb = math.gcd(bd, SELECT_NB)
    return pl.pallas_call(
        functools.partial(_dsa_sample_select_kernel, nb=nb, tn=tn, n_pages=n_pages, ksel=ksel),
        grid=(bd // nb,),
        in_specs=[pl.BlockSpec((nb, n_pages, tn, PAGE), lambda i: (i, 0, 0, 0)),
                  pl.BlockSpec((nb, tn, PAGE), lambda i: (i, 0, 0))],
        out_specs=pl.BlockSpec((nb, n_pages + 1, tn, PAGE), lambda i: (i, 0, 0, 0)),
        out_shape=jax.ShapeDtypeStruct((bd, n_pages + 1, tn, PAGE), F32),
        compiler_params=_cparams("parallel"),
        name="dsa_sample_select",
    )(scores, scores_new)


def _dsa_sample_kernel(pt_ref, bias_ref, q_ref, kn_ref, vn_ref, *rest, tn, d, n_pages, nsteps, pp):
    del pt_ref
    kp_refs, vp_refs = rest[:pp], rest[pp:2 * pp]
    o_ref, qbd_sc, m_sc, l_sc, acc_sc = rest[2 * pp:]
    step = pl.program_id(1)
    groups = d // HEAD64
    rows = groups * tn

    @pl.when(step == 0)
    def _():
        qbd_sc[...] = _block_diag_queries(q_ref[0].astype(F32), groups, tn)
        m_sc[...] = jnp.full((rows, 1), NEG_BIG, F32)
        l_sc[...] = jnp.zeros((rows, 1), F32)
        acc_sc[...] = jnp.zeros((rows, d), F32)

    def tiled_bias(page):
        return jnp.concatenate([bias_ref[0, page]] * groups, axis=0)

    @pl.when(step < nsteps)
    def _():
        bias = jnp.concatenate([tiled_bias(step * pp + i) for i in range(pp)], axis=1)
        s = jnp.concatenate([_dot(qbd_sc[...], r[...].reshape(d, PAGE).astype(BF16)) for r in kp_refs],
                            axis=1) + bias
        pv = lambda pb: sum(_dot_nt(pb[:, i * PAGE:(i + 1) * PAGE], vp_refs[i][...].reshape(d, PAGE).astype(BF16))
                            for i in range(pp))
        _online_update(s, bias == 0.0, pv, m_sc, l_sc, acc_sc)

    @pl.when(step == nsteps)
    def _():
        bias = tiled_bias(n_pages)
        s = _dot_nt(qbd_sc[...], _pad_rows(kn_ref[0], PAGE).astype(BF16)) + bias
        pv = lambda pb: _dot(pb, _pad_rows(vn_ref[0], PAGE).astype(BF16))
        _online_update(s, bias == 0.0, pv, m_sc, l_sc, acc_sc)
        lo_half = _lane_iota() < HEAD64
        for c in range(d // LANES):
            cols = slice(c * LANES, (c + 1) * LANES)
            r0 = slice(2 * c * tn, (2 * c + 1) * tn)
            r1 = slice((2 * c + 1) * tn, (2 * c + 2) * tn)
            o = jnp.where(lo_half, acc_sc[r0, cols] / l_sc[r0, :], acc_sc[r1, cols] / l_sc[r1, :])
            o_ref[0, :, cols] = o.astype(BF16)


def _dsa_sample(page_table, bias, q3, kn3, vn3, k_pool_t, v_pool_t, layer):
    bd, tn, d = q3.shape
    n_pages = page_table.shape[1]
    groups = d // HEAD64
    rows = groups * tn
    pp = math.gcd(n_pages, DSA_SAMPLE_PP)
    nsteps = n_pages // pp
    seq = lambda: pl.BlockSpec((1, tn, d), lambda b, s, pt: (b, 0, 0))
    pool = lambda i: pl.BlockSpec(
        (None, None, groups, HEAD64, PAGE),
        lambda b, s, pt: (layer, pt[b, jnp.minimum(s, nsteps - 1) * pp + i], 0, 0, 0))
    kern = functools.partial(_dsa_sample_kernel, tn=tn, d=d, n_pages=n_pages, nsteps=nsteps, pp=pp)
    return pl.pallas_call(
        kern,
        grid_spec=pltpu.PrefetchScalarGridSpec(
            num_scalar_prefetch=1,
            grid=(bd, nsteps + 1),
            in_specs=[pl.BlockSpec((1, n_pages + 1, tn, PAGE), lambda b, s, pt: (b, 0, 0, 0)),
                      seq(), seq(), seq()] + [pool(i) for i in range(pp)] + [pool(i) for i in range(pp)],
            out_specs=seq(),
            scratch_shapes=[pltpu.VMEM((rows, d), BF16), pltpu.VMEM((rows, 1), F32),
                            pltpu.VMEM((rows, 1), F32), pltpu.VMEM((rows, d), F32)]),
        out_shape=jax.ShapeDtypeStruct((bd, tn, d), BF16),
        compiler_params=_cparams("parallel", "arbitrary"),
        name="dsa_sample",
    )(page_table, bias, q3, kn3, vn3, *([k_pool_t] * pp), *([v_pool_t] * pp))


def _hg_in_kernel(x_ref, g_ref, w_ref, lbw_ref, q_ref, k_ref, lf_ref, v_ref, gs_ref, *, d, layer):
    h = _rms(x_ref[...], g_ref[...]).astype(BF16)
    y = _dot(h, w_ref[...])
    lbw = lbw_ref[...]
    e = jnp.exp(lbw - jnp.max(lbw, axis=0, keepdims=True))
    sm = e / jnp.sum(e, axis=0, keepdims=True)
    lb = jnp.sum(sm[1:layer + 1, :], axis=0, keepdims=True)
    q, fz, v, g = y[:, :d], y[:, d:2 * d], y[:, 2 * d:3 * d], y[:, 3 * d:]
    f = lb + (1.0 - lb) * _sigmoid(fz)
    q_ref[...] = q * _sigmoid(q)
    k_ref[...] = 1.0 - f
    lf_ref[...] = jnp.log(f)
    v_ref[...] = v
    gs_ref[...] = g * _sigmoid(g)


def _hg_in(x, g1, w, lbw, layer):
    r, d = x.shape
    tm = min(r, ROW_TILE)
    row = lambda: pl.BlockSpec((tm, d), lambda i: (i, 0))
    return pl.pallas_call(
        functools.partial(_hg_in_kernel, d=d, layer=layer),
        grid=(r // tm,),
        in_specs=[row(), _const_spec((1, d)), _const_spec((d, 4 * d)), _const_spec(lbw.shape)],
        out_specs=[row()] * 5,
        out_shape=[jax.ShapeDtypeStruct((r, d), F32)] * 5,
        compiler_params=_cparams("parallel"),
        name="hg_in",
    )(x, g1, w, lbw)


def _hg_rec_kernel(q_ref, k_ref, lf_ref, v_ref, gs_ref, s0_ref, gn_ref, o_ref, s_ref, st_sc, *, tb, nt, hp):
    j = pl.program_id(2)

    @pl.when(j == 0)
    def _():
        for hh in range(hp):
            st_sc[hh] = s0_ref[hh].T

    c = HG_C if tb >= HG_C else max(HG_SUB, HG_SHORT_C)
    r_io = lax.broadcasted_iota(I32, (c, c), 0)
    c_io = lax.broadcasted_iota(I32, (c, c), 1)
    causal = r_io >= c_io
    tri = jnp.where(causal, 1.0, 0.0)
    row = lax.broadcasted_iota(I32, (c, 1), 0)
    nch = max(1, tb // c)
    for ci, hh in [(ci, hh) for ci in range(nch) for hh in range(hp)]:
        cols = slice(hh * LANES, (hh + 1) * LANES)
        if tb >= c:
            rows = slice(ci * c, (ci + 1) * c)
            load = lambda ref: ref[rows, cols]
        else:
            load = lambda ref: _pad_rows(ref[:, cols], c)
        q, k, lf, v = load(q_ref), load(k_ref), load(lf_ref), load(v_ref)
        b = jnp.dot(tri, lf, preferred_element_type=F32, precision=lax.Precision.HIGHEST)
        st = st_sc[hh]
        o = _dot_nt((q * jnp.exp(b)).astype(BF16), st.astype(BF16))
        slabs = []
        for blk in range(c // HG_SUB):
            lo, hi = blk * HG_SUB, (blk + 1) * HG_SUB
            anchor = b[lo - 1:lo, :] if blk > 0 else jnp.zeros((1, LANES), F32)
            qb = q[lo:hi, :] * jnp.exp(b[lo:hi, :] - anchor)
            kb = k * jnp.exp(jnp.where(row < hi, anchor - b, -jnp.inf))
            slabs.append(_dot_nt(qb.astype(BF16), kb.astype(BF16)))
        a = jnp.where(causal, jnp.concatenate(slabs, axis=0), 0.0)
        o = o + _dot(a.astype(BF16), v.astype(BF16))
        b_last = b[c - 1:c, :]
        kd = k * jnp.exp(b_last - b)
        st_sc[hh] = st * jnp.exp(b_last) + _dot(v.T.astype(BF16), kd.astype(BF16))
        og = _rms(o, gn_ref[...])
        if tb >= c:
            o_ref[rows, cols] = (og * gs_ref[rows, cols]).astype(BF16)
        else:
            o_ref[:, cols] = (og[:tb, :] * gs_ref[:, cols]).astype(BF16)

    @pl.when(j == nt - 1)
    def _():
        for hh in range(hp):
            s_ref[hh] = st_sc[hh].T


def _hg_rec(q, k, lf, v, gs, s0, gn, bsz, t):
    r, d = q.shape
    heads = d // LANES
    hp = math.gcd(heads, HG_HP)
    tb = min(t, HG_TB)
    nt = t // tb
    blk = lambda: pl.BlockSpec((tb, hp * LANES), lambda b, h, j: (b * nt + j, h))
    state = lambda: pl.BlockSpec((None, hp, LANES, LANES), lambda b, h, j: (b, h, 0, 0))
    return pl.pallas_call(
        functools.partial(_hg_rec_kernel, tb=tb, nt=nt, hp=hp),
        grid=(bsz, heads // hp, nt),
        in_specs=[blk(), blk(), blk(), blk(), blk(), state(), _const_spec((1, LANES))],
        out_specs=[blk(), state()],
        out_shape=[jax.ShapeDtypeStruct((r, d), BF16),
                   jax.ShapeDtypeStruct((bsz, heads, LANES, LANES), F32)],
        scratch_shapes=[pltpu.VMEM((hp, LANES, LANES), F32)],
        compiler_params=_cparams("parallel", "parallel", "arbitrary"),
        name="hg_rec",
    )(q, k, lf, v, gs, s0, gn)


def kernel(x_prompt, x_sample, state_conv, cache_da_k, cache_da_v, cache_dsa_k, cache_dsa_v, cache_dsa_idx_k, state_hgrn, page_table, norm1_g, norm2_g, final_g, mlp_w1, mlp_w2, cv_w1, cv_b1, cv_dw, cv_dwb, cv_ln_g, cv_ln_b, cv_w2, cv_b2, da_w_in, da_lq1, da_lk1, da_lq2, da_lk2, da_subln_g, da_wo, dsa_w_in, dsa_wo, hg_w_in, hg_lb, hg_norm_g, hg_wo):
    bp, t, d = x_prompt.shape
    bd, tn, _ = x_sample.shape
    n_pages = page_table.shape[1]
    past_len = n_pages * PAGE
    depth = norm1_g.shape[0]
    assert d % LANES == 0 and t % ROW_TILE == 0 and tn & (tn - 1) == 0 and tn <= SUBLANES

    xp = x_prompt.reshape(bp * t, d)
    xs = x_sample.reshape(bd * tn, d)
    cos_p, sin_p = _rope_tables(jnp.arange(t))
    cos_s, sin_s = _rope_tables(past_len + jnp.arange(tn))
    cos_s, sin_s = jnp.tile(cos_s, (bd, 1)), jnp.tile(sin_s, (bd, 1))
    ksel_p = min(DSA_TOPK, t // 4)
    ksel_s = min(DSA_TOPK, (past_len + tn) // 4)
    zero_bias = jnp.zeros((1, d), F32)
    w1_all, w2_all = mlp_w1.astype(BF16), mlp_w2.astype(BF16)
    row1 = lambda a: a.reshape(1, -1)
    new = {n: [] for n in ('conv_p', 'conv_s', 'dak_p', 'dav_p', 'dak_s', 'dav_s', 'dsak_p', 'dsav_p', 'dsai_p',
                           'dsak_s', 'dsav_s', 'dsai_s', 'hg_p', 'hg_s')}

    for layer in range(depth):
        kind, j = layer % 4, layer // 4
        g1 = row1(norm1_g[layer])
        bo = zero_bias
        if kind == 0:
            pad = lambda a, n: jnp.concatenate([jnp.zeros(a.shape[:1] + (n,) + a.shape[2:], a.dtype), a], axis=1)
            dw = jnp.concatenate([cv_dw[j], jnp.zeros((CONV_HALO - CONV_W, d), F32)], axis=0)[:, None, :]
            cw = (g1, cv_w1[j].astype(BF16), row1(cv_b1[j]), dw, row1(cv_dwb[j]), row1(cv_ln_g[j]),
                  row1(cv_ln_b[j]))
            halo_pad = CONV_HALO - (CONV_W - 1)
            op, tail_p = _conv_mixer(xp.reshape(bp, t, d), jnp.zeros((bp, CONV_HALO, d), F32), *cw,
                                     nb=1, tt=CONV_TT)
            os_, tail_s = _conv_mixer(xs.reshape(bd, tn, d), pad(state_conv[j], halo_pad), *cw, nb=bd, tt=tn)
            op, os_ = op.reshape(bp * t, d), os_.reshape(bd * tn, d)
            new['conv_p'].append(tail_p[:, halo_pad:])
            new['conv_s'].append(tail_s[:, halo_pad:])
            wo, bo = cv_w2[j].astype(BF16), row1(cv_b2[j])
        elif kind == 1:
            lam_init = 0.8 - 0.6 * math.exp(-0.3 * layer)
            lams = (row1(da_lq1[j]), row1(da_lk1[j]), row1(da_lq2[j]), row1(da_lk2[j]))
            subg = row1(da_subln_g[j])
            w_in = da_w_in[j].astype(BF16)
            qp, kp, vp, kbp, vbp = _da_in(xp, g1, w_in, cos_p, sin_p)
            qs, ks, vs, _, _ = _da_in(xs, g1, w_in, cos_s, sin_s)
            op = _da_prompt(qp, kbp, vbp, lams, subg, bp, t, lam_init)
            os_ = _da_sample(page_table, qs.reshape(bd, tn, d), ks.reshape(bd, tn, d), vs.reshape(bd, tn, d),
                             cache_da_k, cache_da_v, lams, subg, j, lam_init).reshape(bd * tn, d)
            hd = (d // LANES, LANES)
            new['dak_p'].append(kp.reshape((bp, t) + hd))
            new['dav_p'].append(vp.reshape((bp, t) + hd))
            new['dak_s'].append(ks.reshape((bd, tn) + hd))
            new['dav_s'].append(vs.reshape((bd, tn) + hd))
            wo = da_wo[j].astype(BF16)
        elif kind == 2:
            w_in = dsa_w_in[j]
            w_in = jnp.concatenate([w_in, jnp.zeros((d, -w_in.shape[1] % LANES), F32)], axis=1).astype(BF16)
            qp, kp, vp, kbp, vbp, qip, mp_ = _dsa_in(xp, g1, w_in, cos_p, sin_p, seq_len=t)
            qs, ks, vs, _, _, qis, ms_ = _dsa_in(xs, g1, w_in, cos_s, sin_s)
            op = _dsa_prompt(qip, mp_, qp, kbp, vbp, bp, t, ksel_p)
            scores, scores_new = _dsa_sample_scores(
                page_table, qis.reshape(bd, tn, -1), ms_.reshape(bd, tn, LANES),
                jnp.transpose(cache_dsa_idx_k, (0, 1, 3, 2)), j)
            bias = _dsa_sample_select(scores, scores_new, ksel_s)
            os_ = _dsa_sample(page_table, bias, qs.reshape(bd, tn, d), ks.reshape(bd, tn, d),
                              vs.reshape(bd, tn, d), jnp.transpose(cache_dsa_k, (0, 1, 3, 4, 2)),
                              jnp.transpose(cache_dsa_v, (0, 1, 3, 4, 2)), j).reshape(bd * tn, d)
            hd = (d // HEAD64, HEAD64)
            new['dsak_p'].append(jnp.transpose(kp.reshape((bp,) + hd + (t,)), (0, 3, 1, 2)))
            new['dsav_p'].append(jnp.transpose(vp.reshape((bp,) + hd + (t,)), (0, 3, 1, 2)))
            new['dsai_p'].append(mp_[:, :HEAD64].reshape(bp, t, HEAD64))
            new['dsak_s'].append(ks.reshape((bd, tn) + hd))
            new['dsav_s'].append(vs.reshape((bd, tn) + hd))
            new['dsai_s'].append(ms_[:, :HEAD64].reshape(bd, tn, HEAD64))
            wo = dsa_wo[j].astype(BF16)
        else:
            w_in = hg_w_in[j].astype(BF16)
            gn = row1(hg_norm_g[j])
            heads = d // LANES
            hp = _hg_in(xp, g1, w_in, hg_lb, layer)
            hs = _hg_in(xs, g1, w_in, hg_lb, layer)
            op, sp = _hg_rec(*hp, jnp.zeros((bp, heads, LANES, LANES), F32), gn, bp, t)
            os_, ss = _hg_rec(*hs, state_hgrn[j], gn, bd, tn)
            new['hg_p'].append(sp.astype(state_hgrn.dtype))
            new['hg_s'].append(ss.astype(state_hgrn.dtype))
            wo = hg_wo[j].astype(BF16)

        final = layer == depth - 1
        post = (wo, bo, row1(norm2_g[layer]), w1_all, w2_all, row1(final_g))
        xp = _post(xp, op, *post, layer=layer, final=final)
        xs = _post(xs, os_, *post, layer=layer, final=final)

    return (xp.reshape(bp, t, d), xs.reshape(bd, tn, d),
            jnp.stack(new['conv_p']), jnp.stack(new['conv_s']),
            jnp.stack(new['dak_p']), jnp.stack(new['dav_p']), jnp.stack(new['dak_s']), jnp.stack(new['dav_s']),
            jnp.stack(new['dsak_p']), jnp.stack(new['dsav_p']), jnp.stack(new['dsai_p']),
            jnp.stack(new['dsak_s']), jnp.stack(new['dsav_s']), jnp.stack(new['dsai_s']),
            jnp.stack(new['hg_p']), jnp.stack(new['hg_s']))
```
